```python
import math
import jax, jax.numpy as jnp
from jax import lax
import numpy as np

D_MODEL = 2048
BATCH = 4
SEQ = 2048
DEPTH = 2
DEC_BATCH = 128
DEC_SEQ = 4
PAST_LEN = 2048
PAGE_SIZE = 128

N_BRANCH = 4
MIX_W = D_MODEL // N_BRANCH
N_HEADS = 4
HEAD_DIM = MIX_W // N_HEADS
ROT_DIM = HEAD_DIM // 4
IDX_HEADS = 8
IDX_DIM = 64
IDX_ROT = IDX_DIM // 4
TOPK_MAX = 256
QBLOCK = 128
ROPE_THETA = 500000.0
CONV_W = 3
CONV_DIM = MIX_W
SSM_P = 16
SSM_GROUPS = MIX_W // SSM_P
SSM_N = 64
DT_MIN = 0.001
DT_MAX = 0.1
GMLP_CHUNK = 128
GMLP_GROUPS = 4
GMLP_CH = MIX_W // GMLP_GROUPS
D_FF = -((-8 * D_MODEL) // (3 * 256)) * 256
IN_SIZES = (N_HEADS * HEAD_DIM, N_HEADS * HEAD_DIM, N_HEADS * HEAD_DIM, IDX_HEADS * IDX_DIM, IDX_DIM, IDX_HEADS,
            MIX_W, MIX_W, MIX_W, MIX_W, MIX_W, MIX_W, N_BRANCH * D_MODEL)
N_IN = 3 * N_HEADS * HEAD_DIM + IDX_HEADS * IDX_DIM + IDX_DIM + IDX_HEADS + 6 * MIX_W + N_BRANCH * D_MODEL
F32 = jnp.float32

kernel_name = 'hybrid_dsa_conv_s5_gmlp_step'


def _split_points():
    return [int(v) for v in np.cumsum(IN_SIZES)[:-1]]


def rmsnorm(x, g, eps=1e-6):
    xf = x.astype(F32)
    y = xf * lax.rsqrt(jnp.mean(xf * xf, axis=-1, keepdims=True) + eps)
    return (y * g.astype(F32)).astype(x.dtype)


def rope_partial(x, pos, rot):
    half = rot // 2
    inv = ROPE_THETA ** (-jnp.arange(half, dtype=F32) / half)
    ang = pos.astype(F32)[:, None] * inv[None, :]
    c = jnp.cos(ang)[:, None, :]
    s = jnp.sin(ang)[:, None, :]
    xr = x[..., :rot].astype(F32)
    x1, x2 = xr[..., :half], xr[..., half:]
    rotated = jnp.concatenate([x1 * c - x2 * s, x2 * c + x1 * s], axis=-1).astype(x.dtype)
    return jnp.concatenate([rotated, x[..., rot:]], axis=-1)


def take_rows(a, idx):
    return jax.vmap(lambda ab, ib: ab[ib])(a, idx)


def index_scores(qi, wi, ki):
    dots = jnp.einsum('bthd,bsd->bths', qi.astype(F32), ki.astype(F32)) * (IDX_DIM ** -0.5)
    return jnp.einsum('bths,bth->bts', jax.nn.relu(dots), wi.astype(F32) * (IDX_HEADS ** -0.5))


def attend_selected(q, ks, vs, valid):
    s = jnp.einsum('bthd,btkhd->bthk', q.astype(F32), ks.astype(F32)) * (HEAD_DIM ** -0.5)
    s = jnp.where(valid[:, :, None, :], s, -jnp.inf)
    p = jax.nn.softmax(s, axis=-1)
    return jnp.einsum('bthk,btkhd->bthd', p, vs.astype(F32)).astype(q.dtype)


def dsa_prompt(q, k, v, qi, wi, ki):
    B, L = q.shape[:2]
    topk = min(TOPK_MAX, L // 4)
    n_blocks = L // QBLOCK
    key_pos = jnp.arange(L)

    def block(i):
        t0 = i * QBLOCK
        qb = lax.dynamic_slice_in_dim(q, t0, QBLOCK, axis=1)
        qib = lax.dynamic_slice_in_dim(qi, t0, QBLOCK, axis=1)
        wib = lax.dynamic_slice_in_dim(wi, t0, QBLOCK, axis=1)
        tpos = t0 + jnp.arange(QBLOCK)
        sc = index_scores(qib, wib, ki)
        sc = jnp.where((key_pos[None, :] <= tpos[:, None])[None], sc, -jnp.inf)
        _, idx = lax.top_k(sc, topk)
        valid = idx <= tpos[None, :, None]
        return attend_selected(qb, take_rows(k, idx), take_rows(v, idx), valid)

    out = lax.map(block, jnp.arange(n_blocks))
    return out.transpose(1, 0, 2, 3, 4).reshape(B, L, N_HEADS * HEAD_DIM)


def dsa_sample(q, k, v, qi, wi, ki, cache_k, cache_v, cache_kidx, page_table, l):
    B, T = q.shape[:2]
    past = page_table.shape[1] * PAGE_SIZE
    L = past + T
    topk = min(TOPK_MAX, L // 4)
    ki_past = cache_kidx[l, page_table].reshape(B, past, IDX_DIM)
    ki_all = jnp.concatenate([ki_past.astype(ki.dtype), ki], axis=1)
    tpos = past + jnp.arange(T)
    sc = index_scores(qi, wi, ki_all)
    sc = jnp.where((jnp.arange(L)[None, :] <= tpos[:, None])[None], sc, -jnp.inf)
    _, idx = lax.top_k(sc, topk)
    valid = idx <= tpos[None, :, None]
    in_past = (idx < past)[..., None, None]
    pidx = jnp.minimum(idx, past - 1)
    phys = page_table[jnp.arange(B)[:, None, None], pidx // PAGE_SIZE]
    off = pidx % PAGE_SIZE
    nidx = jnp.clip(idx - past, 0, T - 1)
    ks = jnp.where(in_past, cache_k[l, phys, off].astype(k.dtype), take_rows(k, nidx))
    vs = jnp.where(in_past, cache_v[l, phys, off].astype(v.dtype), take_rows(v, nidx))
    out = attend_selected(q, ks, vs, valid)
    return out.reshape(B, T, N_HEADS * HEAD_DIM)


def short_conv(z, buf, w):
    T = z.shape[1]
    zf = jnp.concatenate([buf.astype(z.dtype), z], axis=1)
    y = w[0] * zf[:, 0:T]
    for j in range(1, CONV_W):
        y = y + w[j] * zf[:, j:j + T]
    return y, zf[:, -(CONV_W - 1):]


def s5_scan(u, h0_re, h0_im, a_re, a_im, log_dt, b_re, b_im, c_re, c_im, d_skip):
    B, T = u.shape[:2]
    uf = u.astype(F32).reshape(B, T, SSM_GROUPS, SSM_P)
    dt = jnp.exp(log_dt.astype(F32))[:, None]
    ar, ai = a_re.astype(F32), a_im.astype(F32)
    mag = jnp.exp(dt * ar)
    abar_re, abar_im = mag * jnp.cos(dt * ai), mag * jnp.sin(dt * ai)
    den = ar * ar + ai * ai
    nr, ni = abar_re - 1.0, abar_im
    coef_re = (nr * ar + ni * ai) / den
    coef_im = (ni * ar - nr * ai) / den
    br_, bi_ = b_re.astype(F32), b_im.astype(F32)
    bb_re = coef_re[..., None] * br_ - coef_im[..., None] * bi_
    bb_im = coef_re[..., None] * bi_ + coef_im[..., None] * br_
    bu_re = jnp.einsum('btgp,gnp->btgn', uf, bb_re)
    bu_im = jnp.einsum('btgp,gnp->btgn', uf, bb_im)
    hr0, hi0 = h0_re.astype(F32), h0_im.astype(F32)
    bu_re = bu_re.at[:, 0].add(abar_re * hr0 - abar_im * hi0)
    bu_im = bu_im.at[:, 0].add(abar_re * hi0 + abar_im * hr0)
    a_r = jnp.broadcast_to(abar_re, bu_re.shape)
    a_i = jnp.broadcast_to(abar_im, bu_im.shape)

    def combine(e1, e2):
        a1r, a1i, b1r, b1i = e1
        a2r, a2i, b2r, b2i = e2
        return (a2r * a1r - a2i * a1i, a2r * a1i + a2i * a1r,
                a2r * b1r - a2i * b1i + b2r, a2r * b1i + a2i * b1r + b2i)

    _, _, hr, hi = lax.associative_scan(combine, (a_r, a_i, bu_re, bu_im), axis=1)
    y = (jnp.einsum('btgn,gpn->btgp', hr, c_re.astype(F32))
         - jnp.einsum('btgn,gpn->btgp', hi, c_im.astype(F32))
         + d_skip.astype(F32).reshape(SSM_GROUPS, SSM_P) * uf)
    return y.reshape(B, T, MIX_W).astype(u.dtype), hr[:, -1], hi[:, -1]


def chunk_mlp(u, v, ws, bs):
    B, T = v.shape[:2]
    n_chunks = -(-T // GMLP_CHUNK)
    pad = n_chunks * GMLP_CHUNK - T
    vp = jnp.pad(v, ((0, 0), (0, pad), (0, 0))).reshape(B, n_chunks, GMLP_CHUNK, GMLP_GROUPS, GMLP_CH)
    wm = jnp.where(jnp.tril(jnp.ones((GMLP_CHUNK, GMLP_CHUNK), dtype=bool))[None], ws, 0.0).astype(v.dtype)
    z = jnp.einsum('gts,bcsgh->bctgh', wm, vp) + bs.T[None, None, :, :, None].astype(v.dtype)
    z = z.reshape(B, n_chunks * GMLP_CHUNK, MIX_W)[:, :T]
    return u * z


def setup_inputs(seed: int = 0) -> dict:
    key = jax.random.key(seed)
    k = jax.random.split(key, 32)
    n_pages = PAST_LEN // PAGE_SIZE
    n_pool = (DEC_BATCH * n_pages * 5) // 4

    def nrm(i, shape, scale):
        return jax.random.normal(k[i], shape, F32) * scale

    perm = jax.random.permutation(k[8], n_pool)
    page_table = perm[:DEC_BATCH * n_pages].reshape(DEC_BATCH, n_pages).astype(jnp.int32)
    a_im = jnp.pi * jnp.arange(SSM_N, dtype=F32)[None, None, :] + nrm(12, (DEPTH, SSM_GROUPS, SSM_N), 0.01)
    return {
        'x_prompt': nrm(0, (BATCH, SEQ, D_MODEL), 1.0),
        'x_sample': nrm(1, (DEC_BATCH, DEC_SEQ, D_MODEL), 1.0),
        'cache_k': nrm(2, (DEPTH, n_pool, PAGE_SIZE, N_HEADS, HEAD_DIM), 1.0),
        'cache_v': nrm(3, (DEPTH, n_pool, PAGE_SIZE, N_HEADS, HEAD_DIM), 1.0),
        'cache_kidx': nrm(4, (DEPTH, n_pool, PAGE_SIZE, IDX_DIM), 1.0),
        'state_conv': nrm(5, (DEPTH, DEC_BATCH, CONV_W - 1, CONV_DIM), 1.0),
        'state_ssm_re': nrm(6, (DEPTH, DEC_BATCH, SSM_GROUPS, SSM_N), 1.0),
        'state_ssm_im': nrm(7, (DEPTH, DEC_BATCH, SSM_GROUPS, SSM_N), 1.0),
        'page_table': page_table,
        'norm_mix': 1.0 + nrm(9, (DEPTH, D_MODEL), 0.01),
        'w_in': nrm(10, (DEPTH, D_MODEL, N_IN), D_MODEL ** -0.5),
        'conv_w': nrm(11, (DEPTH, CONV_W, CONV_DIM), CONV_W ** -0.5),
        'ssm_a_re': -0.5 + nrm(13, (DEPTH, SSM_GROUPS, SSM_N), 0.01),
        'ssm_a_im': a_im,
        'ssm_log_dt': jax.random.uniform(k[14], (DEPTH, SSM_GROUPS), F32, math.log(DT_MIN), math.log(DT_MAX)),
        'ssm_b_re': nrm(15, (DEPTH, SSM_GROUPS, SSM_N, SSM_P), (2 * SSM_P) ** -0.5),
        'ssm_b_im': nrm(16, (DEPTH, SSM_GROUPS, SSM_N, SSM_P), (2 * SSM_P) ** -0.5),
        'ssm_c_re': nrm(17, (DEPTH, SSM_GROUPS, SSM_P, SSM_N), SSM_N ** -0.5),
        'ssm_c_im': nrm(18, (DEPTH, SSM_GROUPS, SSM_P, SSM_N), SSM_N ** -0.5),
        'ssm_d': nrm(19, (DEPTH, MIX_W), 1.0),
        'w_glu': nrm(20, (DEPTH, MIX_W, MIX_W), MIX_W ** -0.5),
        'b_glu': nrm(21, (DEPTH, MIX_W), 0.01),
        'gmlp_w': nrm(22, (DEPTH, GMLP_GROUPS, GMLP_CHUNK, GMLP_CHUNK), GMLP_CHUNK ** -0.5),
        'gmlp_b': 1.0 + nrm(23, (DEPTH, GMLP_GROUPS, GMLP_CHUNK), 0.01),
        'w_branch': nrm(24, (DEPTH, N_BRANCH, MIX_W, D_MODEL), 0.5 * MIX_W ** -0.5),
        'w_out': nrm(25, (DEPTH, D_MODEL, D_MODEL), D_MODEL ** -0.5),
        'norm_ffn': 1.0 + nrm(26, (DEPTH, D_MODEL), 0.01),
        'w_ffn_in': nrm(27, (DEPTH, D_MODEL, 2 * D_FF), D_MODEL ** -0.5),
        'w_ffn_out': nrm(28, (DEPTH, D_FF, D_MODEL), D_FF ** -0.5),
        'norm_final': 1.0 + nrm(29, (D_MODEL,), 0.01),
    }


def reference(x_prompt, x_sample, cache_k, cache_v, cache_kidx, state_conv, state_ssm_re, state_ssm_im, page_table,
              norm_mix, w_in, conv_w, ssm_a_re, ssm_a_im, ssm_log_dt, ssm_b_re, ssm_b_im, ssm_c_re, ssm_c_im, ssm_d,
              w_glu, b_glu, gmlp_w, gmlp_b, w_branch, w_out, norm_ffn, w_ffn_in, w_ffn_out, norm_final):
    split_points = _split_points()

    def run_layer(x, l, pos, conv_buf, h_re, h_im, sample):
        B, T = x.shape[:2]
        h = rmsnorm(x, norm_mix[l])
        proj = h @ w_in[l]
        q, k, v, qi, ki, wi, cx, cb, cc, su, gu, gv, gt = jnp.split(proj, split_points, axis=-1)
        q = rope_partial(q.reshape(B, T, N_HEADS, HEAD_DIM), pos, ROT_DIM)
        k = rope_partial(k.reshape(B, T, N_HEADS, HEAD_DIM), pos, ROT_DIM)
        v = v.reshape(B, T, N_HEADS, HEAD_DIM)
        qi = rope_partial(qi.reshape(B, T, IDX_HEADS, IDX_DIM), pos, IDX_ROT)
        ki = rope_partial(ki[:, :, None, :], pos, IDX_ROT)[:, :, 0]
        if sample:
            ya = dsa_sample(q, k, v, qi, wi, ki, cache_k, cache_v, cache_kidx, page_table, l)
        else:
            ya = dsa_prompt(q, k, v, qi, wi, ki)
        zc, new_buf = short_conv(cc * cx, conv_buf, conv_w[l])
        yb = cb * zc
        ys, n_re, n_im = s5_scan(su, h_re, h_im, ssm_a_re[l], ssm_a_im[l], ssm_log_dt[l], ssm_b_re[l], ssm_b_im[l],
                                 ssm_c_re[l], ssm_c_im[l], ssm_d[l])
        g = jax.nn.gelu(ys)
        yc = g * jax.nn.sigmoid(g @ w_glu[l] + b_glu[l])
        yd = chunk_mlp(gu, gv, gmlp_w[l], gmlp_b[l])
        y_all = jnp.stack([ya, yb, yc, yd], axis=2)
        br = jnp.einsum('btkc,kcd->btkd', y_all, w_branch[l])
        gates = jax.nn.sigmoid(gt.reshape(B, T, N_BRANCH, D_MODEL))
        x = x + jnp.sum(gates * br, axis=2) @ w_out[l]
        h2 = rmsnorm(x, norm_ffn[l])
        a, b = jnp.split(h2 @ w_ffn_in[l], 2, axis=-1)
        x = x + (jax.nn.silu(a) * b) @ w_ffn_out[l]
        return x, (k, v, ki, new_buf, n_re, n_im, gv)

    bp, tp = x_prompt.shape[:2]
    tsm = x_sample.shape[1]
    past = page_table.shape[1] * PAGE_SIZE
    pos_p = jnp.arange(tp)
    pos_s = past + jnp.arange(tsm)
    conv0 = jnp.zeros((bp, CONV_W - 1, CONV_DIM), x_prompt.dtype)
    h0 = jnp.zeros((bp, SSM_GROUPS, SSM_N), F32)

    xp, xs = x_prompt, x_sample
    st_p, st_s = [], []
    for l in range(DEPTH):
        xp, sp = run_layer(xp, l, pos_p, conv0, h0, h0, False)
        st_p.append(sp)
        xs, ss = run_layer(xs, l, pos_s, state_conv[l], state_ssm_re[l], state_ssm_im[l], True)
        st_s.append(ss)

    y_prompt = rmsnorm(xp, norm_final)
    y_sample = rmsnorm(xs, norm_final)
    new_k_prompt = jnp.stack([s[0] for s in st_p])
    new_v_prompt = jnp.stack([s[1] for s in st_p])
    new_kidx_prompt = jnp.stack([s[2] for s in st_p])
    new_conv_prompt = jnp.stack([s[3] for s in st_p])
    new_ssm_re_prompt = jnp.stack([s[4] for s in st_p])
    new_ssm_im_prompt = jnp.stack([s[5] for s in st_p])
    new_k_sample = jnp.stack([s[0] for s in st_s])
    new_v_sample = jnp.stack([s[1] for s in st_s])
    new_kidx_sample = jnp.stack([s[2] for s in st_s])
    new_conv_sample = jnp.stack([s[3] for s in st_s])
    new_ssm_re_sample = jnp.stack([s[4] for s in st_s])
    new_ssm_im_sample = jnp.stack([s[5] for s in st_s])
    new_gmlp_v_sample = jnp.stack([s[6] for s in st_s])
    return (y_prompt, y_sample, new_k_prompt, new_v_prompt, new_kidx_prompt, new_conv_prompt, new_ssm_re_prompt,
            new_ssm_im_prompt, new_k_sample, new_v_sample, new_kidx_sample, new_conv_sample, new_ssm_re_sample,
            new_ssm_im_sample, new_gmlp_v_sample)
```

```python
import functools

import jax
import jax.numpy as jnp
from jax import lax
from jax.experimental import pallas as pl
from jax.experimental.pallas import tpu as pltpu

F32 = jnp.float32
BF16 = jnp.bfloat16
I32 = jnp.int32

D_MODEL = 2048
BATCH = 4
SEQ = 2048
DEPTH = 2
DEC_BATCH = 128
DEC_SEQ = 4
PAGE_SIZE = 128
N_PAGES = 16
PAST = N_PAGES * PAGE_SIZE
MIX_W = 512
N_HEADS = 4
HEAD_DIM = 128
ROT_DIM = 32
IDX_HEADS = 8
IDX_DIM = 64
IDX_ROT = 16
TOPK = 256
QBLOCK = 128
ROPE_THETA = 500000.0
SSM_P = 16
SSM_GROUPS = 32
SSM_N = 64
SSM_STATE = SSM_GROUPS * SSM_N
GMLP_CHUNK = 128
GMLP_GROUPS = 4
D_FF = 5632
N_BRANCH = 4

MP = BATCH * SEQ
MS = DEC_BATCH * DEC_SEQ
M = MP + MS
LANES = 128
TAIL_COL = 10 * MIX_W
N_MAIN = TAIL_COL + 2 * LANES
CB_Q, CB_K, CB_V, CB_QI, CB_CX, CB_CB, CB_CC, CB_SU, CB_GU, CB_GV = range(10)
SAMPLE_KEYS = PAST + LANES
NEG_INF = float("-inf")
INT_MIN = -2 ** 31
VMEM_LIMIT = 56 * 1024 * 1024


def _params(n_axes, vmem=VMEM_LIMIT):
    return pltpu.CompilerParams(dimension_semantics=("arbitrary",) * n_axes, vmem_limit_bytes=vmem)


def _rms(x, g):
    return x * lax.rsqrt(jnp.mean(x * x, axis=-1, keepdims=True) + 1e-6) * g


def _dot(a, b):
    return jnp.dot(a, b, preferred_element_type=F32)


def _dot_nt(a, b):
    return lax.dot_general(a, b, (((1,), (1,)), ((), ())), preferred_element_type=F32)


def _sigmoid(x):
    return 1.0 / (1.0 + jnp.exp(-x))


def _norm_mm_kernel(x_ref, g_ref, w_ref, o_ref, h_ref):
    @pl.when(pl.program_id(1) == 0)
    def _():
        h_ref[...] = _rms(x_ref[...], g_ref[...]).astype(BF16)

    o_ref[...] = _dot(h_ref[...], w_ref[...])


def _norm_mm(x, g, w, tm, tn):
    m, k = x.shape
    n = w.shape[1]
    return pl.pallas_call(
        _norm_mm_kernel,
        grid=(m // tm, n // tn),
        in_specs=[pl.BlockSpec((tm, k), lambda i, j: (i, 0)),
                  pl.BlockSpec((1, k), lambda i, j: (0, 0)),
                  pl.BlockSpec((k, tn), lambda i, j: (0, j))],
        out_specs=pl.BlockSpec((tm, tn), lambda i, j: (i, j)),
        out_shape=jax.ShapeDtypeStruct((m, n), F32),
        scratch_shapes=[pltpu.VMEM((tm, k), BF16)],
        compiler_params=_params(2),
        name="norm_mm",
    )(x, g, w)


def _gate_merge_kernel(x_ref, g_ref, wg0, wg1, wg2, wg3, y0, y1, y2, y3, wb_ref, o_ref, h_ref):
    @pl.when(pl.program_id(1) == 0)
    def _():
        h_ref[...] = _rms(x_ref[...], g_ref[...]).astype(BF16)

    h = h_ref[...]
    acc = None
    for kk, (wg, y) in enumerate(((wg0, y0), (wg1, y1), (wg2, y2), (wg3, y3))):
        gate = _sigmoid(_dot(h, wg[...]))
        term = gate * _dot(y[...], wb_ref[kk])
        acc = term if acc is None else acc + term
    o_ref[...] = acc.astype(BF16)


def _gate_merge(x, g, wg, ys, wb, tm, tn):
    nj = D_MODEL // tn
    wg_specs = [pl.BlockSpec((D_MODEL, tn), functools.partial(lambda i, j, kk: (0, kk * nj + j), kk=kk))
                for kk in range(N_BRANCH)]
    y_specs = [pl.BlockSpec((tm, MIX_W), lambda i, j: (i, 0)) for _ in range(N_BRANCH)]
    return pl.pallas_call(
        _gate_merge_kernel,
        grid=(M // tm, nj),
        in_specs=[pl.BlockSpec((tm, D_MODEL), lambda i, j: (i, 0)),
                  pl.BlockSpec((1, D_MODEL), lambda i, j: (0, 0))] + wg_specs + y_specs
                 + [pl.BlockSpec((N_BRANCH, MIX_W, tn), lambda i, j: (0, 0, j))],
        out_specs=pl.BlockSpec((tm, tn), lambda i, j: (i, j)),
        out_shape=jax.ShapeDtypeStruct((M, D_MODEL), BF16),
        scratch_shapes=[pltpu.VMEM((tm, D_MODEL), BF16)],
        compiler_params=_params(2),
        name="gate_merge",
    )(x, g, wg, wg, wg, wg, *ys, wb)


def _mm_res_kernel(a_ref, w_ref, r_ref, o_ref):
    o_ref[...] = r_ref[...] + _dot(a_ref[...], w_ref[...])


def _mm_res(a, w, res, tm, tn):
    m, k = a.shape
    n = w.shape[1]
    return pl.pallas_call(
        _mm_res_kernel,
        grid=(m // tm, n // tn),
        in_specs=[pl.BlockSpec((tm, k), lambda i, j: (i, 0)),
                  pl.BlockSpec((k, tn), lambda i, j: (0, j)),
                  pl.BlockSpec((tm, tn), lambda i, j: (i, j))],
        out_specs=pl.BlockSpec((tm, tn), lambda i, j: (i, j)),
        out_shape=jax.ShapeDtypeStruct((m, n), F32),
        compiler_params=_params(2),
        name="mm_res",
    )(a, w, res)


def _swiglu_kernel(x_ref, g_ref, wa_ref, wb_ref, o_ref, h_ref):
    @pl.when(pl.program_id(1) == 0)
    def _():
        h_ref[...] = _rms(x_ref[...], g_ref[...]).astype(BF16)

    h = h_ref[...]
    a = _dot(h, wa_ref[...])
    b = _dot(h, wb_ref[...])
    o_ref[...] = (a * _sigmoid(a) * b).astype(BF16)


def _swiglu(x, g, w, tm, tn):
    nj = D_FF // tn
    return pl.pallas_call(
        _swiglu_kernel,
        grid=(M // tm, nj),
        in_specs=[pl.BlockSpec((tm, D_MODEL), lambda i, j: (i, 0)),
                  pl.BlockSpec((1, D_MODEL), lambda i, j: (0, 0)),
                  pl.BlockSpec((D_MODEL, tn), lambda i, j: (0, j)),
                  pl.BlockSpec((D_MODEL, tn), lambda i, j: (0, nj + j))],
        out_specs=pl.BlockSpec((tm, tn), lambda i, j: (i, j)),
        out_shape=jax.ShapeDtypeStruct((M, D_FF), BF16),
        scratch_shapes=[pltpu.VMEM((tm, D_MODEL), BF16)],
        compiler_params=_params(2),
        name="swiglu",
    )(x, g, w, w)


def _final_norm_kernel(x_ref, g_ref, o_ref):
    o_ref[...] = _rms(x_ref[...], g_ref[...])


def _final_norm(x, g, tm):
    return pl.pallas_call(
        _final_norm_kernel,
        grid=(M // tm,),
        in_specs=[pl.BlockSpec((tm, D_MODEL), lambda i: (i, 0)),
                  pl.BlockSpec((1, D_MODEL), lambda i: (0, 0))],
        out_specs=pl.BlockSpec((tm, D_MODEL), lambda i: (i, 0)),
        out_shape=jax.ShapeDtypeStruct((M, D_MODEL), F32),
        compiler_params=_params(1),
        name="final_norm",
    )(x, g)


PREP_TM = 256


def _prep_kernel(q_ref, k_ref, v_ref, qi_ref, t_ref, inv_ref, qo, ko, kbo, vbo, qio, to, ki2o):
    i = pl.program_id(0)
    row = i * PREP_TM + lax.broadcasted_iota(I32, (PREP_TM, LANES), 0)
    lane = lax.broadcasted_iota(I32, (PREP_TM, LANES), 1)
    pos = jnp.where(row < MP, row & (SEQ - 1), PAST + (row & (DEC_SEQ - 1))).astype(F32)

    def rope(x, inv_row, half):
        ang = pos * inv_row
        c, s = jnp.cos(ang), jnp.sin(ang)
        upper = (lane & half) != 0
        partner = jnp.where(upper, pltpu.roll(x, half, 1), -pltpu.roll(x, LANES - half, 1))
        return x * c + partner * s

    inv_qk, inv_i, inv_t = inv_ref[0:1, :], inv_ref[1:2, :], inv_ref[2:3, :]
    for h in range(N_HEADS):
        sl = slice(h * LANES, (h + 1) * LANES)
        qo[:, sl] = rope(q_ref[:, sl], inv_qk, ROT_DIM // 2).astype(BF16)
        kr = rope(k_ref[:, sl], inv_qk, ROT_DIM // 2)
        ko[:, sl] = kr
        kbo[:, sl] = kr.astype(BF16)
        qio[:, sl] = rope(qi_ref[:, sl], inv_i, IDX_ROT // 2).astype(BF16)
    vbo[...] = v_ref[...].astype(BF16)
    tr = rope(t_ref[...], inv_t, IDX_ROT // 2)
    to[...] = tr
    ki = jnp.where(lane < IDX_DIM, tr, 0.0)
    ki2o[:, :LANES] = ki.astype(BF16)
    ki2o[:, LANES:] = pltpu.roll(ki, IDX_DIM, 1).astype(BF16)


def _prep(proj, inv):
    def cb(c):
        return pl.BlockSpec((PREP_TM, MIX_W), lambda i: (i, c))

    row512 = pl.BlockSpec((PREP_TM, MIX_W), lambda i: (i, 0))
    return pl.pallas_call(
        _prep_kernel,
        grid=(M // PREP_TM,),
        in_specs=[cb(CB_Q), cb(CB_K), cb(CB_V), cb(CB_QI),
                  pl.BlockSpec((PREP_TM, LANES), lambda i: (i, TAIL_COL // LANES)),
                  pl.BlockSpec((8, LANES), lambda i: (0, 0))],
        out_specs=[row512, row512, row512, row512, row512,
                   pl.BlockSpec((PREP_TM, LANES), lambda i: (i, 0)),
                   pl.BlockSpec((PREP_TM, 2 * LANES), lambda i: (i, 0))],
        out_shape=[jax.ShapeDtypeStruct((M, MIX_W), BF16),
                   jax.ShapeDtypeStruct((M, MIX_W), F32),
                   jax.ShapeDtypeStruct((M, MIX_W), BF16),
                   jax.ShapeDtypeStruct((M, MIX_W), BF16),
                   jax.ShapeDtypeStruct((M, MIX_W), BF16),
                   jax.ShapeDtypeStruct((M, LANES), F32),
                   jax.ShapeDtypeStruct((M, 2 * LANES), BF16)],
        compiler_params=_params(1),
        name="prep",
    )(proj, proj, proj, proj, proj, inv)


def _select_bias(sc, allowed, key_ref, k):
    rows, width = sc.shape
    kf = float(k)
    sc = jnp.where(sc == 0.0, 0.0, sc)
    bits = lax.bitcast_convert_type(sc, I32)
    key_ref[...] = jnp.where(bits < 0, bits ^ 0x7FFFFFFF, bits)

    def count_ge(c):
        return jnp.sum(jnp.where(key_ref[...] >= c, 1.0, 0.0), axis=-1, keepdims=True)

    t0 = jnp.where(count_ge(jnp.zeros((rows, 1), I32)) >= kf, 0, INT_MIN).astype(I32)

    def body(i, t):
        cand = t + lax.shift_left(jnp.int32(1), 30 - i)
        return jnp.where(count_ge(cand) >= kf, cand, t)

    t = lax.fori_loop(0, 31, body, t0)
    key = key_ref[...]
    gt = key > t
    eq = key == t
    need = kf - jnp.sum(jnp.where(gt, 1.0, 0.0), axis=-1, keepdims=True)
    r_io = lax.broadcasted_iota(I32, (LANES, LANES), 0)
    c_io = lax.broadcasted_iota(I32, (LANES, LANES), 1)
    tri = jnp.where(r_io <= c_io, 1.0, 0.0).astype(BF16)
    off = jnp.zeros((rows, 1), F32)
    pieces = []
    for c in range(width // LANES):
        sl = slice(c * LANES, (c + 1) * LANES)
        e = jnp.where(eq[:, sl], 1.0, 0.0)
        rank = _dot(e.astype(BF16), tri) + off
        off = off + jnp.sum(e, axis=-1, keepdims=True)
        take = jnp.where(gt[:, sl], 1.0, jnp.where(eq[:, sl], jnp.where(rank <= need, 1.0, 0.0), 0.0))
        take = jnp.where(allowed[:, sl], take, 0.0)
        pieces.append(jnp.where(take > 0.5, 0.0, NEG_INF))
    return jnp.concatenate(pieces, axis=-1)


def _index_scores(qi, w_scaled, ki_a, ki_b, n_keys):
    acc = None
    for p in range(IDX_HEADS // 2):
        qp = qi[:, p * LANES:(p + 1) * LANES]
        for half, ki in enumerate((ki_a, ki_b)):
            hh = 2 * p + half
            d = _dot_nt(qp, ki) * (IDX_DIM ** -0.5)
            term = jnp.maximum(d, 0.0) * w_scaled[:, IDX_DIM + hh:IDX_DIM + hh + 1]
            acc = term if acc is None else acc + term
    return acc


def _dsa_prompt_kernel(q_ref, qi_ref, t_ref, k_ref, v_ref, ki2_ref, o_ref, key_ref):
    i = pl.program_id(1)
    col = lax.broadcasted_iota(I32, (QBLOCK, SEQ), 1)
    row = i * QBLOCK + lax.broadcasted_iota(I32, (QBLOCK, SEQ), 0)
    causal = col <= row
    w_scaled = t_ref[...] * (IDX_HEADS ** -0.5)
    sc = _index_scores(qi_ref[...], w_scaled, ki2_ref[:, :LANES], ki2_ref[:, LANES:], SEQ)
    sc = jnp.where(causal, sc, NEG_INF)
    bias = _select_bias(sc, causal, key_ref, TOPK)
    for h in range(N_HEADS):
        sl = slice(h * HEAD_DIM, (h + 1) * HEAD_DIM)
        s = _dot_nt(q_ref[:, sl], k_ref[:, sl]) * (HEAD_DIM ** -0.5) + bias
        m = jnp.max(s, axis=-1, keepdims=True)
        p = jnp.exp(s - m)
        l = jnp.sum(p, axis=-1, keepdims=True)
        o = _dot(p.astype(BF16), v_ref[:, sl]) / l
        o_ref[:, sl] = o.astype(BF16)


def _dsa_prompt(q, qi, tail, kb, vb, ki2):
    nb = SEQ // QBLOCK

    def qspec(w):
        return pl.BlockSpec((QBLOCK, w), lambda b, i: (b * nb + i, 0))

    def kspec(w):
        return pl.BlockSpec((SEQ, w), lambda b, i: (b, 0))

    return pl.pallas_call(
        _dsa_prompt_kernel,
        grid=(BATCH, nb),
        in_specs=[qspec(MIX_W), qspec(MIX_W), qspec(LANES), kspec(MIX_W), kspec(MIX_W), kspec(2 * LANES)],
        out_specs=qspec(MIX_W),
        out_shape=jax.ShapeDtypeStruct((MP, MIX_W), BF16),
        scratch_shapes=[pltpu.VMEM((QBLOCK, SEQ), I32)],
        compiler_params=_params(2),
        name="dsa_prompt",
    )(q, qi, tail, kb, vb, ki2)


def _dsa_sample_scores_kernel(pt_ref, qi_ref, w_ref, *rest):
    page_refs, knew_ref, o_ref = rest[:N_PAGES], rest[N_PAGES], rest[N_PAGES + 1]
    qi = qi_ref[0]
    w = w_ref[0] * (IDX_HEADS ** -0.5)

    def chunk_scores(keys_bf):
        d = _dot_nt(qi, keys_bf) * (IDX_DIM ** -0.5)
        r = jnp.maximum(d, 0.0) * w
        return jnp.sum(r.reshape(DEC_SEQ, IDX_HEADS, LANES), axis=1)

    for p in range(N_PAGES):
        o_ref[0, :, p * LANES:(p + 1) * LANES] = chunk_scores(page_refs[p][0, 0].astype(BF16))
    new = chunk_scores(knew_ref[0])
    tq = lax.broadcasted_iota(I32, (DEC_SEQ, LANES), 0)
    jk = lax.broadcasted_iota(I32, (DEC_SEQ, LANES), 1)
    o_ref[0, :, PAST:] = jnp.where(jk <= tq, new, NEG_INF)


def _dsa_sample_scores(page_table, qi32, w32, cache_kidx, layer, knew_pad):
    page_specs = [pl.BlockSpec((1, 1, PAGE_SIZE, IDX_DIM),
                               functools.partial(lambda b, pt, p: (layer, pt[b, p], 0, 0), p=p))
                  for p in range(N_PAGES)]
    grid_spec = pltpu.PrefetchScalarGridSpec(
        num_scalar_prefetch=1,
        grid=(DEC_BATCH,),
        in_specs=[pl.BlockSpec((1, DEC_SEQ * IDX_HEADS, IDX_DIM), lambda b, pt: (b, 0, 0)),
                  pl.BlockSpec((1, DEC_SEQ * IDX_HEADS, 1), lambda b, pt: (b, 0, 0))] + page_specs
                 + [pl.BlockSpec((1, LANES, IDX_DIM), lambda b, pt: (b, 0, 0))],
        out_specs=pl.BlockSpec((1, DEC_SEQ, SAMPLE_KEYS), lambda b, pt: (b, 0, 0)),
    )
    return pl.pallas_call(
        _dsa_sample_scores_kernel,
        grid_spec=grid_spec,
        out_shape=jax.ShapeDtypeStruct((DEC_BATCH, DEC_SEQ, SAMPLE_KEYS), F32),
        compiler_params=_params(1),
        name="dsa_sample_scores",
    )(page_table, qi32, w32, *([cache_kidx] * N_PAGES), knew_pad)


SEL_ROWS = 128


def _sample_select_kernel(sc_ref, o_ref, key_ref):
    sc = sc_ref[...]
    o_ref[...] = _select_bias(sc, sc > NEG_INF, key_ref, TOPK)


def _sample_select(sc):
    return pl.pallas_call(
        _sample_select_kernel,
        grid=(MS // SEL_ROWS,),
        in_specs=[pl.BlockSpec((SEL_ROWS, SAMPLE_KEYS), lambda i: (i, 0))],
        out_specs=pl.BlockSpec((SEL_ROWS, SAMPLE_KEYS), lambda i: (i, 0)),
        out_shape=jax.ShapeDtypeStruct((MS, SAMPLE_KEYS), F32),
        scratch_shapes=[pltpu.VMEM((SEL_ROWS, SAMPLE_KEYS), I32)],
        compiler_params=_params(1),
        name="sample_select",
    )(sc)


def _dsa_sample_attn_kernel(pt_ref, q_ref, bias_ref, *rest):
    k_refs = rest[:N_PAGES]
    v_refs = rest[N_PAGES:2 * N_PAGES]
    knew_ref, vnew_ref, o_ref, s_ref = rest[2 * N_PAGES:]
    qbd = q_ref[0]
    for p in range(N_PAGES):
        s_ref[:, p * LANES:(p + 1) * LANES] = _dot_nt(qbd, k_refs[p][0, 0].astype(BF16))
    s_ref[:, PAST:] = _dot_nt(qbd, knew_ref[0])
    bias = bias_ref[0]
    s = s_ref[...] * (HEAD_DIM ** -0.5) + jnp.concatenate([bias] * N_HEADS, axis=0)
    m = jnp.max(s, axis=-1, keepdims=True)
    pr = jnp.exp(s - m)
    l = jnp.sum(pr, axis=-1, keepdims=True)
    pb = pr.astype(BF16)
    acc = _dot(pb[:, PAST:], vnew_ref[0])
    for p in range(N_PAGES):
        acc = acc + _dot(pb[:, p * LANES:(p + 1) * LANES], v_refs[p][0, 0].astype(BF16))
    acc = acc / l
    hrow = lax.broadcasted_iota(I32, (N_HEADS * DEC_SEQ, MIX_W), 0) // DEC_SEQ
    hcol = lax.broadcasted_iota(I32, (N_HEADS * DEC_SEQ, MIX_W), 1) // HEAD_DIM
    acc = jnp.where(hrow == hcol, acc, 0.0)
    out = acc[0:DEC_SEQ]
    for h in range(1, N_HEADS):
        out = out + acc[h * DEC_SEQ:(h + 1) * DEC_SEQ]
    o_ref[0] = out.astype(BF16)


def _dsa_sample_attn(page_table, qbd, bias, cache_k, cache_v, layer, knew_pad, vnew_pad):
    def page_spec(p):
        return pl.BlockSpec((1, 1, PAGE_SIZE, MIX_W),
                            functools.partial(lambda b, pt, p: (layer, pt[b, p], 0, 0), p=p))

    grid_spec = pltpu.PrefetchScalarGridSpec(
        num_scalar_prefetch=1,
        grid=(DEC_BATCH,),
        in_specs=[pl.BlockSpec((1, N_HEADS * DEC_SEQ, MIX_W), lambda b, pt: (b, 0, 0)),
                  pl.BlockSpec((1, DEC_SEQ, SAMPLE_KEYS), lambda b, pt: (b, 0, 0))]
                 + [page_spec(p) for p in range(N_PAGES)] + [page_spec(p) for p in range(N_PAGES)]
                 + [pl.BlockSpec((1, LANES, MIX_W), lambda b, pt: (b, 0, 0)),
                    pl.BlockSpec((1, LANES, MIX_W), lambda b, pt: (b, 0, 0))],
        out_specs=pl.BlockSpec((1, DEC_SEQ, MIX_W), lambda b, pt: (b, 0, 0)),
        scratch_shapes=[pltpu.VMEM((N_HEADS * DEC_SEQ, SAMPLE_KEYS), F32)],
    )
    return pl.pallas_call(
        _dsa_sample_attn_kernel,
        grid_spec=grid_spec,
        out_shape=jax.ShapeDtypeStruct((DEC_BATCH, DEC_SEQ, MIX_W), BF16),
        compiler_params=_params(1),
        name="dsa_sample_attn",
    )(page_table, qbd, bias, *([cache_k] * N_PAGES), *([cache_v] * N_PAGES), knew_pad, vnew_pad)


CONV_TT = 512


def _conv_prompt_kernel(cx_ref, cb_ref, cc_ref, w_ref, yb_ref, buf_ref, carry_ref):
    j = pl.program_id(1)

    @pl.when(j == 0)
    def _():
        carry_ref[...] = jnp.zeros_like(carry_ref)

    z = cc_ref[...] * cx_ref[...]
    row = lax.broadcasted_iota(I32, z.shape, 0)
    c0 = carry_ref[0:1, :]
    c1 = carry_ref[1:2, :]
    zm1 = jnp.where(row == 0, c1, pltpu.roll(z, 1, 0))
    zm2 = jnp.where(row == 0, c0, jnp.where(row == 1, c1, pltpu.roll(z, 2, 0)))
    zc = w_ref[0:1, :] * zm2 + w_ref[1:2, :] * zm1 + w_ref[2:3, :] * z
    yb_ref[...] = (cb_ref[...] * zc).astype(BF16)
    last = z[CONV_TT - 2:CONV_TT, :]
    carry_ref[0:2, :] = last
    buf_ref[0] = last


def _conv_prompt(proj, conv_w):
    nt = SEQ // CONV_TT

    def cb(c):
        return pl.BlockSpec((CONV_TT, MIX_W), lambda b, j: (b * nt + j, c))

    return pl.pallas_call(
        _conv_prompt_kernel,
        grid=(BATCH, nt),
        in_specs=[cb(CB_CX), cb(CB_CB), cb(CB_CC), pl.BlockSpec((3, MIX_W), lambda b, j: (0, 0))],
        out_specs=[pl.BlockSpec((CONV_TT, MIX_W), lambda b, j: (b * nt + j, 0)),
                   pl.BlockSpec((1, 2, MIX_W), lambda b, j: (b, 0, 0))],
        out_shape=[jax.ShapeDtypeStruct((MP, MIX_W), BF16),
                   jax.ShapeDtypeStruct((BATCH, 2, MIX_W), F32)],
        scratch_shapes=[pltpu.VMEM((8, MIX_W), F32)],
        compiler_params=_params(2),
        name="conv_prompt",
    )(proj, proj, proj, conv_w)


def _gmlp_prompt_kernel(gu_ref, gv_ref, w_ref, bt_ref, o_ref):
    r_io = lax.broadcasted_iota(I32, (GMLP_CHUNK, GMLP_CHUNK), 0)
    c_io = lax.broadcasted_iota(I32, (GMLP_CHUNK, GMLP_CHUNK), 1)
    tril = c_io <= r_io
    for g in range(GMLP_GROUPS):
        sl = slice(g * LANES, (g + 1) * LANES)
        wm = jnp.where(tril, w_ref[g], 0.0).astype(BF16)
        z = _dot(wm, gv_ref[:, sl].astype(BF16)) + bt_ref[:, g:g + 1]
        o_ref[:, sl] = (gu_ref[:, sl] * z).astype(BF16)


def _gmlp_prompt(proj, gmlp_w, gmlp_bt):
    def cb(c):
        return pl.BlockSpec((GMLP_CHUNK, MIX_W), lambda i: (i, c))

    return pl.pallas_call(
        _gmlp_prompt_kernel,
        grid=(MP // GMLP_CHUNK,),
        in_specs=[cb(CB_GU), cb(CB_GV),
                  pl.BlockSpec((GMLP_GROUPS, GMLP_CHUNK, GMLP_CHUNK), lambda i: (0, 0, 0)),
                  pl.BlockSpec((GMLP_CHUNK, GMLP_GROUPS), lambda i: (0, 0))],
        out_specs=pl.BlockSpec((GMLP_CHUNK, MIX_W), lambda i: (i, 0)),
        out_shape=jax.ShapeDtypeStruct((MP, MIX_W), BF16),
        compiler_params=_params(1),
        name="gmlp_prompt",
    )(proj, proj, gmlp_w, gmlp_bt)


def _mix_sample_kernel(cx_ref, cb_ref, cc_ref, gu_ref, gv_ref, buf_ref, cw_ref, gw_ref, gb_ref,
                       yb_ref, nbuf_ref, yd_ref):
    zf = [buf_ref[:, 0, :], buf_ref[:, 1, :]]
    for t in range(DEC_SEQ):
        zf.append(cc_ref[:, t, :] * cx_ref[:, t, :])
    for t in range(DEC_SEQ):
        zc = cw_ref[0:1, :] * zf[t] + cw_ref[1:2, :] * zf[t + 1] + cw_ref[2:3, :] * zf[t + 2]
        yb_ref[:, t, :] = (cb_ref[:, t, :] * zc).astype(BF16)
    nbuf_ref[:, 0, :] = zf[DEC_SEQ]
    nbuf_ref[:, 1, :] = zf[DEC_SEQ + 1]
    v = [gv_ref[:, s, :] for s in range(DEC_SEQ)]
    for t in range(DEC_SEQ):
        z = gb_ref[t:t + 1, :]
        for s in range(t + 1):
            z = z + gw_ref[t * DEC_SEQ + s:t * DEC_SEQ + s + 1, :] * v[s]
        yd_ref[:, t, :] = (gu_ref[:, t, :] * z).astype(BF16)


def _mix_sample(proj_s, state_conv_l, conv_w, gw_lane, gb_lane):
    def cb(c):
        return pl.BlockSpec((DEC_BATCH, DEC_SEQ, MIX_W), lambda i: (0, 0, c))

    full3 = pl.BlockSpec((DEC_BATCH, DEC_SEQ, MIX_W), lambda i: (0, 0, 0))
    buf3 = pl.BlockSpec((DEC_BATCH, 2, MIX_W), lambda i: (0, 0, 0))
    return pl.pallas_call(
        _mix_sample_kernel,
        grid=(1,),
        in_specs=[cb(CB_CX), cb(CB_CB), cb(CB_CC), cb(CB_GU), cb(CB_GV), buf3,
                  pl.BlockSpec((3, MIX_W), lambda i: (0, 0)),
                  pl.BlockSpec((DEC_SEQ * DEC_SEQ, MIX_W), lambda i: (0, 0)),
                  pl.BlockSpec((DEC_SEQ, MIX_W), lambda i: (0, 0))],
        out_specs=[full3, buf3, full3],
        out_shape=[jax.ShapeDtypeStruct((DEC_BATCH, DEC_SEQ, MIX_W), BF16),
                   jax.ShapeDtypeStruct((DEC_BATCH, 2, MIX_W), F32),
                   jax.ShapeDtypeStruct((DEC_BATCH, DEC_SEQ, MIX_W), BF16)],
        compiler_params=_params(1),
        name="mix_sample",
    )(proj_s, proj_s, proj_s, proj_s, proj_s, state_conv_l, conv_w, gw_lane, gb_lane)


def _ssm_params_kernel(ar_ref, ai_ref, ldt_ref, br_ref, bi_ref, ab_ref, bbr_ref, bbi_ref):
    ar, ai = ar_ref[...], ai_ref[...]
    dt = jnp.exp(ldt_ref[...])
    mag = jnp.exp(dt * ar)
    abar_re, abar_im = mag * jnp.cos(dt * ai), mag * jnp.sin(dt * ai)
    den = ar * ar + ai * ai
    nr, ni = abar_re - 1.0, abar_im
    coef_re = (nr * ar + ni * ai) / den
    coef_im = (ni * ar - nr * ai) / den
    ab_ref[...] = jnp.zeros_like(ab_ref)
    ab_ref[0:1, :] = abar_re
    ab_ref[1:2, :] = abar_im
    br, bi = br_ref[...], bi_ref[...]
    bbr_ref[...] = (coef_re * br - coef_im * bi).astype(BF16)
    bbi_ref[...] = (coef_re * bi + coef_im * br).astype(BF16)


def _ssm_params(a_re, a_im, ldt, bd_re, bd_im):
    row = pl.BlockSpec((1, SSM_STATE), lambda i: (0, 0))
    mat = pl.BlockSpec((MIX_W, SSM_STATE), lambda i: (0, 0))
    return pl.pallas_call(
        _ssm_params_kernel,
        grid=(1,),
        in_specs=[row, row, row, mat, mat],
        out_specs=[pl.BlockSpec((8, SSM_STATE), lambda i: (0, 0)), mat, mat],
        out_shape=[jax.ShapeDtypeStruct((8, SSM_STATE), F32),
                   jax.ShapeDtypeStruct((MIX_W, SSM_STATE), BF16),
                   jax.ShapeDtypeStruct((MIX_W, SSM_STATE), BF16)],
        compiler_params=_params(1),
        name="ssm_params",
    )(a_re, a_im, ldt, bd_re, bd_im)


def _gelu_tanh(x):
    return 0.5 * x * (1.0 + jnp.tanh(0.7978845608028654 * (x + 0.044715 * (x * x * x))))


def _ssm_readout(u, hr, hi, cr_ref, ci_ref, d_ref, wg_ref, bg_ref):
    y = _dot(hr.astype(BF16), cr_ref[...]) - _dot(hi.astype(BF16), ci_ref[...]) + d_ref[...] * u
    g = _gelu_tanh(y)
    return g * _sigmoid(_dot(g.astype(BF16), wg_ref[...]) + bg_ref[...])


SSM_TC = 256


def _ssm_prompt_kernel(u_ref, ab_ref, bbr_ref, bbi_ref, cr_ref, ci_ref, d_ref, wg_ref, bg_ref,
                       y_ref, st_ref, hr_ref, hi_ref, h_ref):
    j = pl.program_id(1)

    @pl.when(j == 0)
    def _():
        h_ref[...] = jnp.zeros_like(h_ref)

    u = u_ref[...]
    ub = u.astype(BF16)
    hr_ref[...] = _dot(ub, bbr_ref[...])
    hi_ref[...] = _dot(ub, bbi_ref[...])
    ar, ai = ab_ref[0:1, :], ab_ref[1:2, :]

    def step(t, carry):
        pr, pi = carry
        nr = ar * pr - ai * pi + hr_ref[pl.ds(t, 1), :]
        ni = ar * pi + ai * pr + hi_ref[pl.ds(t, 1), :]
        hr_ref[pl.ds(t, 1), :] = nr
        hi_ref[pl.ds(t, 1), :] = ni
        return nr, ni

    fr, fi = lax.fori_loop(0, SSM_TC, step, (h_ref[0:1, :], h_ref[1:2, :]), unroll=8)
    h_ref[0:1, :] = fr
    h_ref[1:2, :] = fi
    st_ref[0, 0:1, :] = fr
    st_ref[0, 1:2, :] = fi
    y_ref[...] = _ssm_readout(u, hr_ref[...], hi_ref[...], cr_ref, ci_ref, d_ref, wg_ref, bg_ref).astype(BF16)


def _ssm_prompt(proj, ab, bbr, bbi, cdr, cdi, d, wglu, bglu):
    nt = SEQ // SSM_TC

    def const(shape):
        return pl.BlockSpec(shape, lambda b, j: (0,) * len(shape))

    return pl.pallas_call(
        _ssm_prompt_kernel,
        grid=(BATCH, nt),
        in_specs=[pl.BlockSpec((SSM_TC, MIX_W), lambda b, j: (b * nt + j, CB_SU)),
                  const((8, SSM_STATE)), const((MIX_W, SSM_STATE)), const((MIX_W, SSM_STATE)),
                  const((SSM_STATE, MIX_W)), const((SSM_STATE, MIX_W)), const((1, MIX_W)),
                  const((MIX_W, MIX_W)), const((1, MIX_W))],
        out_specs=[pl.BlockSpec((SSM_TC, MIX_W), lambda b, j: (b * nt + j, 0)),
                   pl.BlockSpec((1, 2, SSM_STATE), lambda b, j: (b, 0, 0))],
        out_shape=[jax.ShapeDtypeStruct((MP, MIX_W), BF16),
                   jax.ShapeDtypeStruct((BATCH, 2, SSM_STATE), F32)],
        scratch_shapes=[pltpu.VMEM((SSM_TC, SSM_STATE), F32), pltpu.VMEM((SSM_TC, SSM_STATE), F32),
                        pltpu.VMEM((8, SSM_STATE), F32)],
        compiler_params=_params(2),
        name="ssm_prompt",
    )(proj, ab, bbr, bbi, cdr, cdi, d, wglu, bglu)


def _ssm_sample_kernel(u_ref, h0r_ref, h0i_ref, ab_ref, bbr_ref, bbi_ref, cr_ref, ci_ref, d_ref, wg_ref, bg_ref,
                       y_ref, nr_ref, ni_ref):
    ar, ai = ab_ref[0:1, :], ab_ref[1:2, :]
    hr, hi = h0r_ref[...], h0i_ref[...]
    for t in range(DEC_SEQ):
        u = u_ref[:, t, :]
        ub = u.astype(BF16)
        hr, hi = (ar * hr - ai * hi + _dot(ub, bbr_ref[...]),
                  ar * hi + ai * hr + _dot(ub, bbi_ref[...]))
        y_ref[:, t, :] = _ssm_readout(u, hr, hi, cr_ref, ci_ref, d_ref, wg_ref, bg_ref).astype(BF16)
    nr_ref[...] = hr
    ni_ref[...] = hi


def _ssm_sample(proj_s, h0r, h0i, ab, bbr, bbi, cdr, cdi, d, wglu, bglu):
    def const(shape):
        return pl.BlockSpec(shape, lambda i: (0,) * len(shape))

    st = const((DEC_BATCH, SSM_STATE))
    return pl.pallas_call(
        _ssm_sample_kernel,
        grid=(1,),
        in_specs=[pl.BlockSpec((DEC_BATCH, DEC_SEQ, MIX_W), lambda i: (0, 0, CB_SU)), st, st,
                  const((8, SSM_STATE)), const((MIX_W, SSM_STATE)), const((MIX_W, SSM_STATE)),
                  const((SSM_STATE, MIX_W)), const((SSM_STATE, MIX_W)), const((1, MIX_W)),
                  const((MIX_W, MIX_W)), const((1, MIX_W))],
        out_specs=[const((DEC_BATCH, DEC_SEQ, MIX_W)), st, st],
        out_shape=[jax.ShapeDtypeStruct((DEC_BATCH, DEC_SEQ, MIX_W), BF16),
                   jax.ShapeDtypeStruct((DEC_BATCH, SSM_STATE), F32),
                   jax.ShapeDtypeStruct((DEC_BATCH, SSM_STATE), F32)],
        compiler_params=_params(1),
        name="ssm_sample",
    )(proj_s, h0r, h0i, ab, bbr, bbi, cdr, cdi, d, wglu, bglu)


def _rope_inv_rows():
    lane = jnp.arange(LANES)

    def inv(half):
        return ROPE_THETA ** (-jnp.arange(half, dtype=F32) / half)

    inv_qk = jnp.where(lane < ROT_DIM, inv(ROT_DIM // 2)[lane % (ROT_DIM // 2)], 0.0)
    idx = inv(IDX_ROT // 2)[lane % (IDX_ROT // 2)]
    inv_i = jnp.where((lane % IDX_DIM) < IDX_ROT, idx, 0.0)
    inv_t = jnp.where(lane < IDX_ROT, idx, 0.0)
    rows = jnp.stack([inv_qk, inv_i, inv_t]).astype(F32)
    return jnp.concatenate([rows, jnp.zeros((5, LANES), F32)], axis=0)


def _pad_rows(a, rows):
    return jnp.pad(a, ((0, 0), (0, rows - a.shape[1]), (0, 0)))


def kernel(x_prompt, x_sample, cache_k, cache_v, cache_kidx, state_conv, state_ssm_re, state_ssm_im, page_table,
           norm_mix, w_in, conv_w, ssm_a_re, ssm_a_im, ssm_log_dt, ssm_b_re, ssm_b_im, ssm_c_re, ssm_c_im, ssm_d,
           w_glu, b_glu, gmlp_w, gmlp_b, w_branch, w_out, norm_ffn, w_ffn_in, w_ffn_out, norm_final):
    x = jnp.concatenate([x_prompt.reshape(MP, D_MODEL), x_sample.reshape(MS, D_MODEL)], axis=0)
    inv = _rope_inv_rows()
    eye_g = jnp.eye(SSM_GROUPS, dtype=F32)
    n_pool = cache_k.shape[1]
    cache_k2 = cache_k.reshape(DEPTH, n_pool, PAGE_SIZE, MIX_W)
    cache_v2 = cache_v.reshape(DEPTH, n_pool, PAGE_SIZE, MIX_W)
    n_gate0 = w_in.shape[2] - N_BRANCH * D_MODEL
    c_ki = 3 * MIX_W + IDX_HEADS * IDX_DIM
    c_cx = c_ki + IDX_DIM + IDX_HEADS

    st_p, st_s = [], []
    for l in range(DEPTH):
        wl = w_in[l]
        w_main = jnp.concatenate(
            [wl[:, :c_ki], wl[:, c_cx:n_gate0], wl[:, c_ki:c_cx],
             jnp.zeros((D_MODEL, N_MAIN - n_gate0), F32)], axis=1).astype(BF16)
        w_gate = wl[:, n_gate0:].astype(BF16)

        proj = _norm_mm(x, norm_mix[l][None, :], w_main, 1088, 896)
        q_r, k_r, k_b, v_b, qi_r, tail_r, ki2 = _prep(proj, inv)

        ya_p = _dsa_prompt(q_r, qi_r, tail_r, k_b, v_b, ki2)
        qi_s = qi_r[MP:].reshape(DEC_BATCH, DEC_SEQ * IDX_HEADS, IDX_DIM)
        tail_s = tail_r[MP:].reshape(DEC_BATCH, DEC_SEQ, LANES)
        w_s = tail_s[:, :, IDX_DIM:IDX_DIM + IDX_HEADS].reshape(DEC_BATCH, DEC_SEQ * IDX_HEADS, 1)
        kinew = _pad_rows(tail_s[:, :, :IDX_DIM].astype(BF16), LANES)
        sc_s = _dsa_sample_scores(page_table, qi_s, w_s, cache_kidx, l, kinew)
        bias_s = _sample_select(sc_s.reshape(MS, SAMPLE_KEYS)).reshape(DEC_BATCH, DEC_SEQ, SAMPLE_KEYS)
        q_s = q_r[MP:].reshape(DEC_BATCH, 1, DEC_SEQ, N_HEADS, HEAD_DIM)
        head_eye = jnp.eye(N_HEADS, dtype=BF16)[None, :, None, :, None]
        qbd = (q_s * head_eye).reshape(DEC_BATCH, N_HEADS * DEC_SEQ, MIX_W)
        knew = _pad_rows(k_b[MP:].reshape(DEC_BATCH, DEC_SEQ, MIX_W), LANES)
        vnew = _pad_rows(v_b[MP:].reshape(DEC_BATCH, DEC_SEQ, MIX_W), LANES)
        ya_s = _dsa_sample_attn(page_table, qbd, bias_s, cache_k2, cache_v2, l, knew, vnew)
        ya = jnp.concatenate([ya_p, ya_s.reshape(MS, MIX_W)], axis=0)

        proj_s = proj[MP:].reshape(DEC_BATCH, DEC_SEQ, N_MAIN)
        yb_p, buf_p = _conv_prompt(proj, conv_w[l])
        yd_p = _gmlp_prompt(proj, gmlp_w[l], gmlp_b[l].T)
        gw_lane = jnp.repeat(gmlp_w[l][:, :DEC_SEQ, :DEC_SEQ].transpose(1, 2, 0).reshape(DEC_SEQ * DEC_SEQ, GMLP_GROUPS),
                             LANES, axis=1)
        gb_lane = jnp.repeat(gmlp_b[l][:, :DEC_SEQ].T, LANES, axis=1)
        yb_s, buf_s, yd_s = _mix_sample(proj_s, state_conv[l], conv_w[l], gw_lane, gb_lane)
        yb = jnp.concatenate([yb_p, yb_s.reshape(MS, MIX_W)], axis=0)
        yd = jnp.concatenate([yd_p, yd_s.reshape(MS, MIX_W)], axis=0)

        bd_re = jnp.einsum('gnp,gh->gphn', ssm_b_re[l], eye_g).reshape(MIX_W, SSM_STATE)
        bd_im = jnp.einsum('gnp,gh->gphn', ssm_b_im[l], eye_g).reshape(MIX_W, SSM_STATE)
        cd_re = jnp.einsum('gpn,gh->gnhp', ssm_c_re[l], eye_g).reshape(SSM_STATE, MIX_W).astype(BF16)
        cd_im = jnp.einsum('gpn,gh->gnhp', ssm_c_im[l], eye_g).reshape(SSM_STATE, MIX_W).astype(BF16)
        ldt = jnp.repeat(ssm_log_dt[l], SSM_N)[None, :]
        ab, bbr, bbi = _ssm_params(ssm_a_re[l].reshape(1, SSM_STATE), ssm_a_im[l].reshape(1, SSM_STATE), ldt,
                                   bd_re, bd_im)
        ssm_consts = (ab, bbr, bbi, cd_re, cd_im, ssm_d[l][None, :], w_glu[l].astype(BF16), b_glu[l][None, :])
        yc_p, hst_p = _ssm_prompt(proj, *ssm_consts)
        yc_s, nre_s, nim_s = _ssm_sample(proj_s, state_ssm_re[l].reshape(DEC_BATCH, SSM_STATE),
                                         state_ssm_im[l].reshape(DEC_BATCH, SSM_STATE), *ssm_consts)
        yc = jnp.concatenate([yc_p, yc_s.reshape(MS, MIX_W)], axis=0)

        merged = _gate_merge(x, norm_mix[l][None, :], w_gate, (ya, yb, yc, yd), w_branch[l].astype(BF16), 544, 512)
        x = _mm_res(merged, w_out[l].astype(BF16), x, 1088, 512)
        act = _swiglu(x, norm_ffn[l][None, :], w_ffn_in[l].astype(BF16), 1088, 512)
        x = _mm_res(act, w_ffn_out[l].astype(BF16), x, 544, 512)

        v_f = proj[:, CB_V * MIX_W:(CB_V + 1) * MIX_W]
        gv_s = proj[MP:, CB_GV * MIX_W:(CB_GV + 1) * MIX_W]
        st_p.append((k_r[:MP].reshape(BATCH, SEQ, N_HEADS, HEAD_DIM), v_f[:MP].reshape(BATCH, SEQ, N_HEADS, HEAD_DIM),
                     tail_r[:MP, :IDX_DIM].reshape(BATCH, SEQ, IDX_DIM), buf_p,
                     hst_p[:, 0].reshape(BATCH, SSM_GROUPS, SSM_N), hst_p[:, 1].reshape(BATCH, SSM_GROUPS, SSM_N)))
        st_s.append((k_r[MP:].reshape(DEC_BATCH, DEC_SEQ, N_HEADS, HEAD_DIM),
                     v_f[MP:].reshape(DEC_BATCH, DEC_SEQ, N_HEADS, HEAD_DIM),
                     tail_r[MP:, :IDX_DIM].reshape(DEC_BATCH, DEC_SEQ, IDX_DIM), buf_s,
                     nre_s.reshape(DEC_BATCH, SSM_GROUPS, SSM_N), nim_s.reshape(DEC_BATCH, SSM_GROUPS, SSM_N),
                     gv_s.reshape(DEC_BATCH, DEC_SEQ, MIX_W)))

    y = _final_norm(x, norm_final[None, :], 1088)
    outs = [y[:MP].reshape(BATCH, SEQ, D_MODEL), y[MP:].reshape(DEC_BATCH, DEC_SEQ, D_MODEL)]
    outs += [jnp.stack([s[i] for s in st_p]) for i in range(6)]
    outs += [jnp.stack([s[i] for s in st_s]) for i in range(7)]
    return tuple(outs)
```

```python
import functools

import jax
import jax.numpy as jnp
from jax import lax
from jax.experimental import pallas as pl
from jax.experimental.pallas import tpu as pltpu

F32 = jnp.float32
BF16 = jnp.bfloat16
I32 = jnp.int32

D_MODEL = 2048
BATCH = 4
SEQ = 2048
DEPTH = 2
DEC_BATCH = 128
DEC_SEQ = 4
PAGE_SIZE = 128
N_PAGES = 16
PAST = N_PAGES * PAGE_SIZE
MIX_W = 512
N_HEADS = 4
HEAD_DIM = 128
ROT_DIM = 32
IDX_HEADS = 8
IDX_DIM = 64
IDX_ROT = 16
TOPK = 256
QBLOCK = 128
ROPE_THETA = 500000.0
SSM_P = 16
SSM_GROUPS = 32
SSM_N = 64
SSM_STATE = SSM_GROUPS * SSM_N
GMLP_CHUNK = 128
GMLP_GROUPS = 4
D_FF = 5632
N_BRANCH = 4

MP = BATCH * SEQ
MS = DEC_BATCH * DEC_SEQ
M = MP + MS
LANES = 128
TAIL_COL = 10 * MIX_W
N_MAIN = TAIL_COL + 2 * LANES
CB_Q, CB_K, CB_V, CB_QI, CB_CX, CB_CB, CB_CC, CB_SU, CB_GU, CB_GV = range(10)
SAMPLE_KEYS = PAST + LANES
NEG_INF = float("-inf")
INT_MIN = -2 ** 31
VMEM_LIMIT = 56 * 1024 * 1024


def _params(n_axes, vmem=VMEM_LIMIT):
    return pltpu.CompilerParams(dimension_semantics=("arbitrary",) * n_axes, vmem_limit_bytes=vmem)


def _rms(x, g):
    return x * lax.rsqrt(jnp.mean(x * x, axis=-1, keepdims=True) + 1e-6) * g


def _dot(a, b):
    return jnp.dot(a, b, preferred_element_type=F32)


def _dot_nt(a, b):
    return lax.dot_general(a, b, (((1,), (1,)), ((), ())), preferred_element_type=F32)


def _sigmoid(x):
    return 1.0 / (1.0 + jnp.exp(-x))


def _norm_mm_kernel(x_ref, g_ref, w_ref, o_ref, h_ref):
    @pl.when(pl.program_id(1) == 0)
    def _():
        h_ref[...] = _rms(x_ref[...], g_ref[...]).astype(BF16)

    o_ref[...] = _dot(h_ref[...], w_ref[...])


def _norm_mm(x, g, w, tm, tn):
    m, k = x.shape
    n = w.shape[1]
    return pl.pallas_call(
        _norm_mm_kernel,
        grid=(m // tm, n // tn),
        in_specs=[pl.BlockSpec((tm, k), lambda i, j: (i, 0)),
                  pl.BlockSpec((1, k), lambda i, j: (0, 0)),
                  pl.BlockSpec((k, tn), lambda i, j: (0, j))],
        out_specs=pl.BlockSpec((tm, tn), lambda i, j: (i, j)),
        out_shape=jax.ShapeDtypeStruct((m, n), F32),
        scratch_shapes=[pltpu.VMEM((tm, k), BF16)],
        compiler_params=_params(2),
        name="norm_mm",
    )(x, g, w)


def _gate_merge_kernel(x_ref, g_ref, wg0, wg1, wg2, wg3, y0, y1, y2, y3, wb_ref, o_ref, h_ref):
    @pl.when(pl.program_id(1) == 0)
    def _():
        h_ref[...] = _rms(x_ref[...], g_ref[...]).astype(BF16)

    h = h_ref[...]
    acc = None
    for kk, (wg, y) in enumerate(((wg0, y0), (wg1, y1), (wg2, y2), (wg3, y3))):
        gate = _sigmoid(_dot(h, wg[...]))
        term = gate * _dot(y[...], wb_ref[kk])
        acc = term if acc is None else acc + term
    o_ref[...] = acc.astype(BF16)


def _gate_merge(x, g, wg, ys, wb, tm, tn):
    nj = D_MODEL // tn
    wg_specs = [pl.BlockSpec((D_MODEL, tn), functools.partial(lambda i, j, kk: (0, kk * nj + j), kk=kk))
                for kk in range(N_BRANCH)]
    y_specs = [pl.BlockSpec((tm, MIX_W), lambda i, j: (i, 0)) for _ in range(N_BRANCH)]
    return pl.pallas_call(
        _gate_merge_kernel,
        grid=(M // tm, nj),
        in_specs=[pl.BlockSpec((tm, D_MODEL), lambda i, j: (i, 0)),
                  pl.BlockSpec((1, D_MODEL), lambda i, j: (0, 0))] + wg_specs + y_specs
                 + [pl.BlockSpec((N_BRANCH, MIX_W, tn), lambda i, j: (0, 0, j))],
        out_specs=pl.BlockSpec((tm, tn), lambda i, j: (i, j)),
        out_shape=jax.ShapeDtypeStruct((M, D_MODEL), BF16),
        scratch_shapes=[pltpu.VMEM((tm, D_MODEL), BF16)],
        compiler_params=_params(2),
        name="gate_merge",
    )(x, g, wg, wg, wg, wg, *ys, wb)


def _mm_res_kernel(a_ref, w_ref, r_ref, o_ref):
    o_ref[...] = r_ref[...] + _dot(a_ref[...], w_ref[...])


def _mm_res(a, w, res, tm, tn):
    m, k = a.shape
    n = w.shape[1]
    return pl.pallas_call(
        _mm_res_kernel,
        grid=(m // tm, n // tn),
        in_specs=[pl.BlockSpec((tm, k), lambda i, j: (i, 0)),
                  pl.BlockSpec((k, tn), lambda i, j: (0, j)),
                  pl.BlockSpec((tm, tn), lambda i, j: (i, j))],
        out_specs=pl.BlockSpec((tm, tn), lambda i, j: (i, j)),
        out_shape=jax.ShapeDtypeStruct((m, n), F32),
        compiler_params=_params(2),
        name="mm_res",
    )(a, w, res)


def _swiglu_kernel(x_ref, g_ref, wa_ref, wb_ref, o_ref, h_ref):
    @pl.when(pl.program_id(1) == 0)
    def _():
        h_ref[...] = _rms(x_ref[...], g_ref[...]).astype(BF16)

    h = h_ref[...]
    a = _dot(h, wa_ref[...])
    b = _dot(h, wb_ref[...])
    o_ref[...] = (a * _sigmoid(a) * b).astype(BF16)


def _swiglu(x, g, w, tm, tn):
    nj = D_FF // tn
    return pl.pallas_call(
        _swiglu_kernel,
        grid=(M // tm, nj),
        in_specs=[pl.BlockSpec((tm, D_MODEL), lambda i, j: (i, 0)),
                  pl.BlockSpec((1, D_MODEL), lambda i, j: (0, 0)),
                  pl.BlockSpec((D_MODEL, tn), lambda i, j: (0, j)),
                  pl.BlockSpec((D_MODEL, tn), lambda i, j: (0, nj + j))],
        out_specs=pl.BlockSpec((tm, tn), lambda i, j: (i, j)),
        out_shape=jax.ShapeDtypeStruct((M, D_FF), BF16),
        scratch_shapes=[pltpu.VMEM((tm, D_MODEL), BF16)],
        compiler_params=_params(2),
        name="swiglu",
    )(x, g, w, w)


def _final_norm_kernel(x_ref, g_ref, o_ref):
    o_ref[...] = _rms(x_ref[...], g_ref[...])


def _final_norm(x, g, row0, rows, tm):
    return pl.pallas_call(
        _final_norm_kernel,
        grid=(rows // tm,),
        in_specs=[pl.BlockSpec((tm, D_MODEL), lambda i: (row0 // tm + i, 0)),
                  pl.BlockSpec((1, D_MODEL), lambda i: (0, 0))],
        out_specs=pl.BlockSpec((tm, D_MODEL), lambda i: (i, 0)),
        out_shape=jax.ShapeDtypeStruct((rows, D_MODEL), F32),
        compiler_params=_params(1),
        name="final_norm",
    )(x, g)


PREP_TM = 256


def _prep_kernel(q_ref, k_ref, v_ref, qi_ref, t_ref, inv_ref, qo, ko, kbo, vbo, qio, to, ki2o):
    i = pl.program_id(0)
    row = i * PREP_TM + lax.broadcasted_iota(I32, (PREP_TM, LANES), 0)
    lane = lax.broadcasted_iota(I32, (PREP_TM, LANES), 1)
    pos = jnp.where(row < MP, row & (SEQ - 1), PAST + (row & (DEC_SEQ - 1))).astype(F32)

    def rope(x, inv_row, half):
        ang = pos * inv_row
        c, s = jnp.cos(ang), jnp.sin(ang)
        upper = (lane & half) != 0
        partner = jnp.where(upper, pltpu.roll(x, half, 1), -pltpu.roll(x, LANES - half, 1))
        return x * c + partner * s

    inv_qk, inv_i, inv_t = inv_ref[0:1, :], inv_ref[1:2, :], inv_ref[2:3, :]
    for h in range(N_HEADS):
        sl = slice(h * LANES, (h + 1) * LANES)
        qo[:, sl] = rope(q_ref[:, sl], inv_qk, ROT_DIM // 2).astype(BF16)
        kr = rope(k_ref[:, sl], inv_qk, ROT_DIM // 2)
        ko[:, sl] = kr
        kbo[:, sl] = kr.astype(BF16)
        qio[:, sl] = rope(qi_ref[:, sl], inv_i, IDX_ROT // 2).astype(BF16)
    vbo[...] = v_ref[...].astype(BF16)
    tr = rope(t_ref[...], inv_t, IDX_ROT // 2)
    to[...] = tr
    ki = jnp.where(lane < IDX_DIM, tr, 0.0)
    ki2o[:, :LANES] = ki.astype(BF16)
    ki2o[:, LANES:] = pltpu.roll(ki, IDX_DIM, 1).astype(BF16)


def _prep(proj, inv):
    def cb(c):
        return pl.BlockSpec((PREP_TM, MIX_W), lambda i: (i, c))

    row512 = pl.BlockSpec((PREP_TM, MIX_W), lambda i: (i, 0))
    return pl.pallas_call(
        _prep_kernel,
        grid=(M // PREP_TM,),
        in_specs=[cb(CB_Q), cb(CB_K), cb(CB_V), cb(CB_QI),
                  pl.BlockSpec((PREP_TM, LANES), lambda i: (i, TAIL_COL // LANES)),
                  pl.BlockSpec((8, LANES), lambda i: (0, 0))],
        out_specs=[row512, row512, row512, row512, row512,
                   pl.BlockSpec((PREP_TM, LANES), lambda i: (i, 0)),
                   pl.BlockSpec((PREP_TM, 2 * LANES), lambda i: (i, 0))],
        out_shape=[jax.ShapeDtypeStruct((M, MIX_W), BF16),
                   jax.ShapeDtypeStruct((M, MIX_W), F32),
                   jax.ShapeDtypeStruct((M, MIX_W), BF16),
                   jax.ShapeDtypeStruct((M, MIX_W), BF16),
                   jax.ShapeDtypeStruct((M, MIX_W), BF16),
                   jax.ShapeDtypeStruct((M, LANES), F32),
                   jax.ShapeDtypeStruct((M, 2 * LANES), BF16)],
        compiler_params=_params(1),
        name="prep",
    )(proj, proj, proj, proj, proj, inv)


KEY_NEG_INF = INT_MIN + 0x7FFFFF


def _key_to_float(key):
    return lax.bitcast_convert_type(jnp.where(key < 0, key ^ 0x7FFFFFFF, key), F32)


def _select_chunks(sc_ref, bias_ref, nc, k, allowed_fn):
    _, rows, cw = sc_ref.shape
    kf = float(k)

    def count_ge(c):
        def chunk(j, acc):
            return acc + jnp.where(sc_ref[j] >= c, 1.0, 0.0)

        return jnp.sum(lax.fori_loop(0, nc, chunk, jnp.zeros((rows, cw), F32)), axis=-1, keepdims=True)

    t0 = jnp.where(count_ge(jnp.zeros((rows, 1), F32)) >= kf, 0, INT_MIN).astype(I32)

    def body(i, t):
        cand = t + lax.shift_left(jnp.int32(1), 30 - i)
        ok = jnp.logical_or(count_ge(_key_to_float(cand)) >= kf, cand <= KEY_NEG_INF)
        return jnp.where(ok, cand, t)

    t = lax.fori_loop(0, 31, body, t0)
    lo = _key_to_float(t)
    hi = _key_to_float(t + 1)
    need = kf - count_ge(hi)
    r_io = lax.broadcasted_iota(I32, (LANES, LANES), 0)
    c_io = lax.broadcasted_iota(I32, (LANES, LANES), 1)
    tri = jnp.where(r_io <= c_io, 1.0, 0.0).astype(BF16)

    def tie_chunk(j, off):
        s = sc_ref[j]
        for u in range(cw // LANES):
            sl = slice(u * LANES, (u + 1) * LANES)
            su = s[:, sl]
            above = su >= hi
            e = jnp.where(above, 0.0, jnp.where(su >= lo, 1.0, 0.0))
            rank = _dot(e.astype(BF16), tri) + off
            off = off + jnp.sum(e, axis=-1, keepdims=True)
            take = jnp.where(above, 1.0, jnp.where(rank <= need, e, 0.0))
            take = jnp.where(allowed_fn(j, u, su), take, 0.0)
            bias_ref[j, :, sl] = jnp.where(take > 0.5, 0.0, NEG_INF)
        return off

    lax.fori_loop(0, nc, tie_chunk, jnp.zeros((rows, 1), F32))


def _index_scores(qi, w_scaled, ki_a, ki_b, n_keys):
    acc = None
    for p in range(IDX_HEADS // 2):
        qp = qi[:, p * LANES:(p + 1) * LANES]
        for half, ki in enumerate((ki_a, ki_b)):
            hh = 2 * p + half
            d = _dot_nt(qp, ki) * (IDX_DIM ** -0.5)
            term = jnp.maximum(d, 0.0) * w_scaled[:, IDX_DIM + hh:IDX_DIM + hh + 1]
            acc = term if acc is None else acc + term
    return acc


KEY_CHUNK = 256
N_KEY_CHUNKS = SEQ // KEY_CHUNK


def _dsa_prompt_kernel(q_ref, qi_ref, t_ref, k_ref, v_ref, ki2_ref, o_ref, sc_ref, bias_ref, s_ref):
    i = pl.program_id(1)
    nc = lax.shift_right_logical(i, 1) + 1
    row = i * QBLOCK + lax.broadcasted_iota(I32, (QBLOCK, KEY_CHUNK), 0)
    lane = lax.broadcasted_iota(I32, (QBLOCK, KEY_CHUNK), 1)
    w_scaled = t_ref[...] * (IDX_HEADS ** -0.5)

    def key_rows(j):
        return pl.ds(pl.multiple_of(j * KEY_CHUNK, KEY_CHUNK), KEY_CHUNK)

    def score_chunk(j, carry):
        rows = key_rows(j)
        sc = _index_scores(qi_ref[...], w_scaled, ki2_ref[rows, :LANES], ki2_ref[rows, LANES:], KEY_CHUNK)
        sc_ref[j] = jnp.where(j * KEY_CHUNK + lane <= row, sc, NEG_INF)
        return carry

    lax.fori_loop(0, nc, score_chunk, 0)

    row_g = i * QBLOCK + lax.broadcasted_iota(I32, (QBLOCK, LANES), 0)
    lane_g = lax.broadcasted_iota(I32, (QBLOCK, LANES), 1)

    def causal(j, u, su):
        return j * KEY_CHUNK + u * LANES + lane_g <= row_g

    _select_chunks(sc_ref, bias_ref, nc, TOPK, causal)

    for h in range(N_HEADS):
        sl = slice(h * HEAD_DIM, (h + 1) * HEAD_DIM)
        qh = q_ref[:, sl]

        def logits(j, m):
            s = _dot_nt(qh, k_ref[key_rows(j), sl]) * (HEAD_DIM ** -0.5) + bias_ref[j]
            s_ref[j] = s
            return jnp.maximum(m, jnp.max(s, axis=-1, keepdims=True))

        m = lax.fori_loop(0, nc, logits, jnp.full((QBLOCK, 1), NEG_INF, F32))

        def weighted(j, carry):
            l, acc = carry
            p = jnp.exp(s_ref[j] - m)
            return (l + jnp.sum(p, axis=-1, keepdims=True),
                    acc + _dot(p.astype(BF16), v_ref[key_rows(j), sl]))

        l, acc = lax.fori_loop(0, nc, weighted,
                               (jnp.zeros((QBLOCK, 1), F32), jnp.zeros((QBLOCK, HEAD_DIM), F32)))
        o_ref[:, sl] = (acc / l).astype(BF16)


def _dsa_prompt(q, qi, tail, kb, vb, ki2):
    nb = SEQ // QBLOCK

    def qspec(w):
        return pl.BlockSpec((QBLOCK, w), lambda b, i: (b * nb + i, 0))

    def kspec(w):
        return pl.BlockSpec((SEQ, w), lambda b, i: (b, 0))

    return pl.pallas_call(
        _dsa_prompt_kernel,
        grid=(BATCH, nb),
        in_specs=[qspec(MIX_W), qspec(MIX_W), qspec(LANES), kspec(MIX_W), kspec(MIX_W), kspec(2 * LANES)],
        out_specs=qspec(MIX_W),
        out_shape=jax.ShapeDtypeStruct((MP, MIX_W), BF16),
        scratch_shapes=[pltpu.VMEM((N_KEY_CHUNKS, QBLOCK, KEY_CHUNK), F32)] * 3,
        compiler_params=_params(2),
        name="dsa_prompt",
    )(q, qi, tail, kb, vb, ki2)


def _dsa_sample_scores_kernel(pt_ref, qi_ref, w_ref, *rest):
    page_refs, knew_ref, o_ref = rest[:N_PAGES], rest[N_PAGES], rest[N_PAGES + 1]
    qi = qi_ref[0]
    w = w_ref[0] * (IDX_HEADS ** -0.5)

    def chunk_scores(d):
        r = jnp.maximum(d * (IDX_DIM ** -0.5), 0.0) * w
        return jnp.sum(r.reshape(DEC_SEQ, IDX_HEADS, LANES), axis=1)

    for p in range(N_PAGES):
        o_ref[0, :, p * LANES:(p + 1) * LANES] = chunk_scores(_dot(qi, page_refs[p][0, 0].astype(BF16)))
    new = chunk_scores(_dot_nt(qi, knew_ref[0]))
    tq = lax.broadcasted_iota(I32, (DEC_SEQ, LANES), 0)
    jk = lax.broadcasted_iota(I32, (DEC_SEQ, LANES), 1)
    o_ref[0, :, PAST:] = jnp.where(jk <= tq, new, NEG_INF)


def _dsa_sample_scores(page_table, qi32, w32, cache_kidx_t, layer, knew_pad):
    page_specs = [pl.BlockSpec((1, 1, IDX_DIM, PAGE_SIZE),
                               functools.partial(lambda b, pt, p: (layer, pt[b, p], 0, 0), p=p))
                  for p in range(N_PAGES)]
    grid_spec = pltpu.PrefetchScalarGridSpec(
        num_scalar_prefetch=1,
        grid=(DEC_BATCH,),
        in_specs=[pl.BlockSpec((1, DEC_SEQ * IDX_HEADS, IDX_DIM), lambda b, pt: (b, 0, 0)),
                  pl.BlockSpec((1, DEC_SEQ * IDX_HEADS, 1), lambda b, pt: (b, 0, 0))] + page_specs
                 + [pl.BlockSpec((1, LANES, IDX_DIM), lambda b, pt: (b, 0, 0))],
        out_specs=pl.BlockSpec((1, DEC_SEQ, SAMPLE_KEYS), lambda b, pt: (b, 0, 0)),
    )
    return pl.pallas_call(
        _dsa_sample_scores_kernel,
        grid_spec=grid_spec,
        out_shape=jax.ShapeDtypeStruct((DEC_BATCH, DEC_SEQ, SAMPLE_KEYS), F32),
        compiler_params=_params(1),
        name="dsa_sample_scores",
    )(page_table, qi32, w32, *([cache_kidx_t] * N_PAGES), knew_pad)


SEL_ROWS = 128
N_SAMPLE_CHUNKS = SAMPLE_KEYS // LANES


def _sample_select_kernel(in_ref, o_ref, sc_ref, bias_ref):
    for c in range(N_SAMPLE_CHUNKS):
        sc_ref[c] = in_ref[:, c * LANES:(c + 1) * LANES]
    _select_chunks(sc_ref, bias_ref, N_SAMPLE_CHUNKS, TOPK, lambda j, u, su: su > NEG_INF)
    for c in range(N_SAMPLE_CHUNKS):
        o_ref[:, c * LANES:(c + 1) * LANES] = bias_ref[c]


def _sample_select(sc):
    return pl.pallas_call(
        _sample_select_kernel,
        grid=(MS // SEL_ROWS,),
        in_specs=[pl.BlockSpec((SEL_ROWS, SAMPLE_KEYS), lambda i: (i, 0))],
        out_specs=pl.BlockSpec((SEL_ROWS, SAMPLE_KEYS), lambda i: (i, 0)),
        out_shape=jax.ShapeDtypeStruct((MS, SAMPLE_KEYS), F32),
        scratch_shapes=[pltpu.VMEM((N_SAMPLE_CHUNKS, SEL_ROWS, LANES), F32)] * 2,
        compiler_params=_params(1),
        name="sample_select",
    )(sc)


def _dsa_sample_attn_kernel(pt_ref, q_ref, bias_ref, *rest):
    k_refs = rest[:N_PAGES]
    v_refs = rest[N_PAGES:2 * N_PAGES]
    knew_ref, vnew_ref, o_ref, s_ref = rest[2 * N_PAGES:]
    q = q_ref[0]
    for h in range(N_HEADS):
        rs = slice(h * DEC_SEQ, (h + 1) * DEC_SEQ)
        hs = slice(h * HEAD_DIM, (h + 1) * HEAD_DIM)
        qh = q[:, hs]
        for p in range(N_PAGES):
            s_ref[rs, p * LANES:(p + 1) * LANES] = _dot_nt(qh, k_refs[p][0, 0, :, h, :].astype(BF16))
        s_ref[rs, PAST:] = _dot_nt(qh, knew_ref[0, :, hs])
    bias = bias_ref[0]
    s = s_ref[...] * (HEAD_DIM ** -0.5) + jnp.concatenate([bias] * N_HEADS, axis=0)
    m = jnp.max(s, axis=-1, keepdims=True)
    pr = jnp.exp(s - m)
    l = jnp.sum(pr, axis=-1, keepdims=True)
    pb = pr.astype(BF16)
    for h in range(N_HEADS):
        rs = slice(h * DEC_SEQ, (h + 1) * DEC_SEQ)
        hs = slice(h * HEAD_DIM, (h + 1) * HEAD_DIM)
        acc = _dot(pb[rs, PAST:], vnew_ref[0, :, hs])
        for p in range(N_PAGES):
            acc = acc + _dot(pb[rs, p * LANES:(p + 1) * LANES], v_refs[p][0, 0, :, h, :].astype(BF16))
        o_ref[0, :, hs] = (acc / l[rs]).astype(BF16)


def _dsa_sample_attn(page_table, q, bias, cache_k, cache_v, layer, knew_pad, vnew_pad):
    def page_spec(p):
        return pl.BlockSpec((1, 1, PAGE_SIZE, N_HEADS, HEAD_DIM),
                            functools.partial(lambda b, pt, p: (layer, pt[b, p], 0, 0, 0), p=p))

    grid_spec = pltpu.PrefetchScalarGridSpec(
        num_scalar_prefetch=1,
        grid=(DEC_BATCH,),
        in_specs=[pl.BlockSpec((1, DEC_SEQ, MIX_W), lambda b, pt: (b, 0, 0)),
                  pl.BlockSpec((1, DEC_SEQ, SAMPLE_KEYS), lambda b, pt: (b, 0, 0))]
                 + [page_spec(p) for p in range(N_PAGES)] + [page_spec(p) for p in range(N_PAGES)]
                 + [pl.BlockSpec((1, LANES, MIX_W), lambda b, pt: (b, 0, 0)),
                    pl.BlockSpec((1, LANES, MIX_W), lambda b, pt: (b, 0, 0))],
        out_specs=pl.BlockSpec((1, DEC_SEQ, MIX_W), lambda b, pt: (b, 0, 0)),
        scratch_shapes=[pltpu.VMEM((N_HEADS * DEC_SEQ, SAMPLE_KEYS), F32)],
    )
    return pl.pallas_call(
        _dsa_sample_attn_kernel,
        grid_spec=grid_spec,
        out_shape=jax.ShapeDtypeStruct((DEC_BATCH, DEC_SEQ, MIX_W), BF16),
        compiler_params=_params(1),
        name="dsa_sample_attn",
    )(page_table, q, bias, *([cache_k] * N_PAGES), *([cache_v] * N_PAGES), knew_pad, vnew_pad)


CONV_TT = 512


def _conv_prompt_kernel(cx_ref, cb_ref, cc_ref, w_ref, yb_ref, buf_ref, carry_ref):
    j = pl.program_id(1)

    @pl.when(j == 0)
    def _():
        carry_ref[...] = jnp.zeros_like(carry_ref)

    z = cc_ref[...] * cx_ref[...]
    row = lax.broadcasted_iota(I32, z.shape, 0)
    c0 = carry_ref[0:1, :]
    c1 = carry_ref[1:2, :]
    zm1 = jnp.where(row == 0, c1, pltpu.roll(z, 1, 0))
    zm2 = jnp.where(row == 0, c0, jnp.where(row == 1, c1, pltpu.roll(z, 2, 0)))
    zc = w_ref[0:1, :] * zm2 + w_ref[1:2, :] * zm1 + w_ref[2:3, :] * z
    yb_ref[...] = (cb_ref[...] * zc).astype(BF16)
    last = z[CONV_TT - 2:CONV_TT, :]
    carry_ref[0:2, :] = last
    buf_ref[0] = last


def _conv_prompt(proj, conv_w):
    nt = SEQ // CONV_TT

    def cb(c):
        return pl.BlockSpec((CONV_TT, MIX_W), lambda b, j: (b * nt + j, c))

    return pl.pallas_call(
        _conv_prompt_kernel,
        grid=(BATCH, nt),
        in_specs=[cb(CB_CX), cb(CB_CB), cb(CB_CC), pl.BlockSpec((3, MIX_W), lambda b, j: (0, 0))],
        out_specs=[pl.BlockSpec((CONV_TT, MIX_W), lambda b, j: (b * nt + j, 0)),
                   pl.BlockSpec((1, 2, MIX_W), lambda b, j: (b, 0, 0))],
        out_shape=[jax.ShapeDtypeStruct((MP, MIX_W), BF16),
                   jax.ShapeDtypeStruct((BATCH, 2, MIX_W), F32)],
        scratch_shapes=[pltpu.VMEM((8, MIX_W), F32)],
        compiler_params=_params(2),
        name="conv_prompt",
    )(proj, proj, proj, conv_w)


def _gmlp_prompt_kernel(gu_ref, gv_ref, w_ref, bt_ref, o_ref):
    r_io = lax.broadcasted_iota(I32, (GMLP_CHUNK, GMLP_CHUNK), 0)
    c_io = lax.broadcasted_iota(I32, (GMLP_CHUNK, GMLP_CHUNK), 1)
    tril = c_io <= r_io
    for g in range(GMLP_GROUPS):
        sl = slice(g * LANES, (g + 1) * LANES)
        wm = jnp.where(tril, w_ref[g], 0.0).astype(BF16)
        z = _dot(wm, gv_ref[:, sl].astype(BF16)) + bt_ref[:, g:g + 1]
        o_ref[:, sl] = (gu_ref[:, sl] * z).astype(BF16)


def _gmlp_prompt(proj, gmlp_w, gmlp_bt):
    def cb(c):
        return pl.BlockSpec((GMLP_CHUNK, MIX_W), lambda i: (i, c))

    return pl.pallas_call(
        _gmlp_prompt_kernel,
        grid=(MP // GMLP_CHUNK,),
        in_specs=[cb(CB_GU), cb(CB_GV),
                  pl.BlockSpec((GMLP_GROUPS, GMLP_CHUNK, GMLP_CHUNK), lambda i: (0, 0, 0)),
                  pl.BlockSpec((GMLP_CHUNK, GMLP_GROUPS), lambda i: (0, 0))],
        out_specs=pl.BlockSpec((GMLP_CHUNK, MIX_W), lambda i: (i, 0)),
        out_shape=jax.ShapeDtypeStruct((MP, MIX_W), BF16),
        compiler_params=_params(1),
        name="gmlp_prompt",
    )(proj, proj, gmlp_w, gmlp_bt)


def _mix_sample_kernel(cx_ref, cb_ref, cc_ref, gu_ref, gv_ref, buf_ref, cw_ref, gw_ref, gb_ref,
                       yb_ref, nbuf_ref, yd_ref):
    zf = [buf_ref[:, 0, :], buf_ref[:, 1, :]]
    for t in range(DEC_SEQ):
        zf.append(cc_ref[:, t, :] * cx_ref[:, t, :])
    for t in range(DEC_SEQ):
        zc = cw_ref[0:1, :] * zf[t] + cw_ref[1:2, :] * zf[t + 1] + cw_ref[2:3, :] * zf[t + 2]
        yb_ref[:, t, :] = (cb_ref[:, t, :] * zc).astype(BF16)
    nbuf_ref[:, 0, :] = zf[DEC_SEQ]
    nbuf_ref[:, 1, :] = zf[DEC_SEQ + 1]
    v = [gv_ref[:, s, :] for s in range(DEC_SEQ)]
    for t in range(DEC_SEQ):
        z = gb_ref[t:t + 1, :]
        for s in range(t + 1):
            z = z + gw_ref[t * DEC_SEQ + s:t * DEC_SEQ + s + 1, :] * v[s]
        yd_ref[:, t, :] = (gu_ref[:, t, :] * z).astype(BF16)


def _mix_sample(proj_s, state_conv_l, conv_w, gw_lane, gb_lane):
    def cb(c):
        return pl.BlockSpec((DEC_BATCH, DEC_SEQ, MIX_W), lambda i: (0, 0, c))

    full3 = pl.BlockSpec((DEC_BATCH, DEC_SEQ, MIX_W), lambda i: (0, 0, 0))
    buf3 = pl.BlockSpec((DEC_BATCH, 2, MIX_W), lambda i: (0, 0, 0))
    return pl.pallas_call(
        _mix_sample_kernel,
        grid=(1,),
        in_specs=[cb(CB_CX), cb(CB_CB), cb(CB_CC), cb(CB_GU), cb(CB_GV), buf3,
                  pl.BlockSpec((3, MIX_W), lambda i: (0, 0)),
                  pl.BlockSpec((DEC_SEQ * DEC_SEQ, MIX_W), lambda i: (0, 0)),
                  pl.BlockSpec((DEC_SEQ, MIX_W), lambda i: (0, 0))],
        out_specs=[full3, buf3, full3],
        out_shape=[jax.ShapeDtypeStruct((DEC_BATCH, DEC_SEQ, MIX_W), BF16),
                   jax.ShapeDtypeStruct((DEC_BATCH, 2, MIX_W), F32),
                   jax.ShapeDtypeStruct((DEC_BATCH, DEC_SEQ, MIX_W), BF16)],
        compiler_params=_params(1),
        name="mix_sample",
    )(proj_s, proj_s, proj_s, proj_s, proj_s, state_conv_l, conv_w, gw_lane, gb_lane)


def _ssm_params_kernel(ar_ref, ai_ref, ldt_ref, br_ref, bi_ref, ab_ref, bbr_ref, bbi_ref):
    ar, ai = ar_ref[...], ai_ref[...]
    dt = jnp.exp(ldt_ref[...])
    mag = jnp.exp(dt * ar)
    abar_re, abar_im = mag * jnp.cos(dt * ai), mag * jnp.sin(dt * ai)
    den = ar * ar + ai * ai
    nr, ni = abar_re - 1.0, abar_im
    coef_re = (nr * ar + ni * ai) / den
    coef_im = (ni * ar - nr * ai) / den
    ab_ref[...] = jnp.zeros_like(ab_ref)
    ab_ref[0:1, :] = abar_re
    ab_ref[1:2, :] = abar_im
    br, bi = br_ref[...], bi_ref[...]
    bbr_ref[...] = (coef_re * br - coef_im * bi).astype(BF16)
    bbi_ref[...] = (coef_re * bi + coef_im * br).astype(BF16)


def _ssm_params(a_re, a_im, ldt, bd_re, bd_im):
    row = pl.BlockSpec((1, SSM_STATE), lambda i: (0, 0))
    mat = pl.BlockSpec((MIX_W, SSM_STATE), lambda i: (0, 0))
    return pl.pallas_call(
        _ssm_params_kernel,
        grid=(1,),
        in_specs=[row, row, row, mat, mat],
        out_specs=[pl.BlockSpec((8, SSM_STATE), lambda i: (0, 0)), mat, mat],
        out_shape=[jax.ShapeDtypeStruct((8, SSM_STATE), F32),
                   jax.ShapeDtypeStruct((MIX_W, SSM_STATE), BF16),
                   jax.ShapeDtypeStruct((MIX_W, SSM_STATE), BF16)],
        compiler_params=_params(1),
        name="ssm_params",
    )(a_re, a_im, ldt, bd_re, bd_im)


def _gelu_tanh(x):
    return 0.5 * x * (1.0 + jnp.tanh(0.7978845608028654 * (x + 0.044715 * (x * x * x))))


def _ssm_readout(u, hr, hi, cr_ref, ci_ref, d_ref, wg_ref, bg_ref):
    y = _dot(hr.astype(BF16), cr_ref[...]) - _dot(hi.astype(BF16), ci_ref[...]) + d_ref[...] * u
    g = _gelu_tanh(y)
    return g * _sigmoid(_dot(g.astype(BF16), wg_ref[...]) + bg_ref[...])


SSM_TC = 256


def _ssm_prompt_kernel(u_ref, ab_ref, bbr_ref, bbi_ref, cr_ref, ci_ref, d_ref, wg_ref, bg_ref,
                       y_ref, st_ref, hr_ref, hi_ref, h_ref):
    j = pl.program_id(1)

    @pl.when(j == 0)
    def _():
        h_ref[...] = jnp.zeros_like(h_ref)

    u = u_ref[...]
    ub = u.astype(BF16)
    hr_ref[...] = _dot(ub, bbr_ref[...])
    hi_ref[...] = _dot(ub, bbi_ref[...])
    ar, ai = ab_ref[0:1, :], ab_ref[1:2, :]

    def step(t, carry):
        pr, pi = carry
        nr = ar * pr - ai * pi + hr_ref[pl.ds(t, 1), :]
        ni = ar * pi + ai * pr + hi_ref[pl.ds(t, 1), :]
        hr_ref[pl.ds(t, 1), :] = nr
        hi_ref[pl.ds(t, 1), :] = ni
        return nr, ni

    fr, fi = lax.fori_loop(0, SSM_TC, step, (h_ref[0:1, :], h_ref[1:2, :]), unroll=8)
    h_ref[0:1, :] = fr
    h_ref[1:2, :] = fi
    st_ref[0, 0:1, :] = fr
    st_ref[0, 1:2, :] = fi
    y_ref[...] = _ssm_readout(u, hr_ref[...], hi_ref[...], cr_ref, ci_ref, d_ref, wg_ref, bg_ref).astype(BF16)


def _ssm_prompt(proj, ab, bbr, bbi, cdr, cdi, d, wglu, bglu):
    nt = SEQ // SSM_TC

    def const(shape):
        return pl.BlockSpec(shape, lambda b, j: (0,) * len(shape))

    return pl.pallas_call(
        _ssm_prompt_kernel,
        grid=(BATCH, nt),
        in_specs=[pl.BlockSpec((SSM_TC, MIX_W), lambda b, j: (b * nt + j, CB_SU)),
                  const((8, SSM_STATE)), const((MIX_W, SSM_STATE)), const((MIX_W, SSM_STATE)),
                  const((SSM_STATE, MIX_W)), const((SSM_STATE, MIX_W)), const((1, MIX_W)),
                  const((MIX_W, MIX_W)), const((1, MIX_W))],
        out_specs=[pl.BlockSpec((SSM_TC, MIX_W), lambda b, j: (b * nt + j, 0)),
                   pl.BlockSpec((1, 2, SSM_STATE), lambda b, j: (b, 0, 0))],
        out_shape=[jax.ShapeDtypeStruct((MP, MIX_W), BF16),
                   jax.ShapeDtypeStruct((BATCH, 2, SSM_STATE), F32)],
        scratch_shapes=[pltpu.VMEM((SSM_TC, SSM_STATE), F32), pltpu.VMEM((SSM_TC, SSM_STATE), F32),
                        pltpu.VMEM((8, SSM_STATE), F32)],
        compiler_params=_params(2),
        name="ssm_prompt",
    )(proj, ab, bbr, bbi, cdr, cdi, d, wglu, bglu)


def _ssm_sample_kernel(u_ref, h0r_ref, h0i_ref, ab_ref, bbr_ref, bbi_ref, cr_ref, ci_ref, d_ref, wg_ref, bg_ref,
                       y_ref, nr_ref, ni_ref):
    ar, ai = ab_ref[0:1, :], ab_ref[1:2, :]
    hr, hi = h0r_ref[...], h0i_ref[...]
    for t in range(DEC_SEQ):
        u = u_ref[:, t, :]
        ub = u.astype(BF16)
        hr, hi = (ar * hr - ai * hi + _dot(ub, bbr_ref[...]),
                  ar * hi + ai * hr + _dot(ub, bbi_ref[...]))
        y_ref[:, t, :] = _ssm_readout(u, hr, hi, cr_ref, ci_ref, d_ref, wg_ref, bg_ref).astype(BF16)
    nr_ref[...] = hr
    ni_ref[...] = hi


def _ssm_sample(proj_s, h0r, h0i, ab, bbr, bbi, cdr, cdi, d, wglu, bglu):
    def const(shape):
        return pl.BlockSpec(shape, lambda i: (0,) * len(shape))

    st = const((DEC_BATCH, SSM_STATE))
    return pl.pallas_call(
        _ssm_sample_kernel,
        grid=(1,),
        in_specs=[pl.BlockSpec((DEC_BATCH, DEC_SEQ, MIX_W), lambda i: (0, 0, CB_SU)), st, st,
                  const((8, SSM_STATE)), const((MIX_W, SSM_STATE)), const((MIX_W, SSM_STATE)),
                  const((SSM_STATE, MIX_W)), const((SSM_STATE, MIX_W)), const((1, MIX_W)),
                  const((MIX_W, MIX_W)), const((1, MIX_W))],
        out_specs=[const((DEC_BATCH, DEC_SEQ, MIX_W)), st, st],
        out_shape=[jax.ShapeDtypeStruct((DEC_BATCH, DEC_SEQ, MIX_W), BF16),
                   jax.ShapeDtypeStruct((DEC_BATCH, SSM_STATE), F32),
                   jax.ShapeDtypeStruct((DEC_BATCH, SSM_STATE), F32)],
        compiler_params=_params(1),
        name="ssm_sample",
    )(proj_s, h0r, h0i, ab, bbr, bbi, cdr, cdi, d, wglu, bglu)


def _rope_inv_rows():
    lane = jnp.arange(LANES)

    def inv(half):
        return ROPE_THETA ** (-jnp.arange(half, dtype=F32) / half)

    inv_qk = jnp.where(lane < ROT_DIM, inv(ROT_DIM // 2)[lane % (ROT_DIM // 2)], 0.0)
    idx = inv(IDX_ROT // 2)[lane % (IDX_ROT // 2)]
    inv_i = jnp.where((lane % IDX_DIM) < IDX_ROT, idx, 0.0)
    inv_t = jnp.where(lane < IDX_ROT, idx, 0.0)
    rows = jnp.stack([inv_qk, inv_i, inv_t]).astype(F32)
    return jnp.concatenate([rows, jnp.zeros((5, LANES), F32)], axis=0)


def _pad_rows(a, rows):
    return jnp.pad(a, ((0, 0), (0, rows - a.shape[1]), (0, 0)))


def kernel(x_prompt, x_sample, cache_k, cache_v, cache_kidx, state_conv, state_ssm_re, state_ssm_im, page_table,
           norm_mix, w_in, conv_w, ssm_a_re, ssm_a_im, ssm_log_dt, ssm_b_re, ssm_b_im, ssm_c_re, ssm_c_im, ssm_d,
           w_glu, b_glu, gmlp_w, gmlp_b, w_branch, w_out, norm_ffn, w_ffn_in, w_ffn_out, norm_final):
    x = jnp.concatenate([x_prompt.reshape(MP, D_MODEL), x_sample.reshape(MS, D_MODEL)], axis=0)
    inv = _rope_inv_rows()
    eye_g = jnp.eye(SSM_GROUPS, dtype=F32)
    cache_kidx_t = jnp.swapaxes(cache_kidx, 2, 3)
    n_gate0 = w_in.shape[2] - N_BRANCH * D_MODEL
    c_ki = 3 * MIX_W + IDX_HEADS * IDX_DIM
    c_cx = c_ki + IDX_DIM + IDX_HEADS

    st_p, st_s = [], []
    for l in range(DEPTH):
        wl = w_in[l]
        w_main = jnp.concatenate(
            [wl[:, :c_ki], wl[:, c_cx:n_gate0], wl[:, c_ki:c_cx],
             jnp.zeros((D_MODEL, N_MAIN - n_gate0), F32)], axis=1).astype(BF16)
        w_gate = wl[:, n_gate0:].astype(BF16)

        proj = _norm_mm(x, norm_mix[l][None, :], w_main, 1088, 896)
        q_r, k_r, k_b, v_b, qi_r, tail_r, ki2 = _prep(proj, inv)

        ya_p = _dsa_prompt(q_r, qi_r, tail_r, k_b, v_b, ki2)
        qi_s = qi_r[MP:].reshape(DEC_BATCH, DEC_SEQ * IDX_HEADS, IDX_DIM)
        tail_s = tail_r[MP:].reshape(DEC_BATCH, DEC_SEQ, LANES)
        w_s = tail_s[:, :, IDX_DIM:IDX_DIM + IDX_HEADS].reshape(DEC_BATCH, DEC_SEQ * IDX_HEADS, 1)
        kinew = _pad_rows(tail_s[:, :, :IDX_DIM].astype(BF16), LANES)
        sc_s = _dsa_sample_scores(page_table, qi_s, w_s, cache_kidx_t, l, kinew)
        bias_s = _sample_select(sc_s.reshape(MS, SAMPLE_KEYS)).reshape(DEC_BATCH, DEC_SEQ, SAMPLE_KEYS)
        q_s = q_r[MP:].reshape(DEC_BATCH, DEC_SEQ, MIX_W)
        knew = _pad_rows(k_b[MP:].reshape(DEC_BATCH, DEC_SEQ, MIX_W), LANES)
        vnew = _pad_rows(v_b[MP:].reshape(DEC_BATCH, DEC_SEQ, MIX_W), LANES)
        ya_s = _dsa_sample_attn(page_table, q_s, bias_s, cache_k, cache_v, l, knew, vnew)
        ya = jnp.concatenate([ya_p, ya_s.reshape(MS, MIX_W)], axis=0)

        proj_s = proj[MP:].reshape(DEC_BATCH, DEC_SEQ, N_MAIN)
        yb_p, buf_p = _conv_prompt(proj, conv_w[l])
        yd_p = _gmlp_prompt(proj, gmlp_w[l], gmlp_b[l].T)
        gw_lane = jnp.repeat(gmlp_w[l][:, :DEC_SEQ, :DEC_SEQ].transpose(1, 2, 0).reshape(DEC_SEQ * DEC_SEQ, GMLP_GROUPS),
                             LANES, axis=1)
        gb_lane = jnp.repeat(gmlp_b[l][:, :DEC_SEQ].T, LANES, axis=1)
        yb_s, buf_s, yd_s = _mix_sample(proj_s, state_conv[l], conv_w[l], gw_lane, gb_lane)
        yb = jnp.concatenate([yb_p, yb_s.reshape(MS, MIX_W)], axis=0)
        yd = jnp.concatenate([yd_p, yd_s.reshape(MS, MIX_W)], axis=0)

        bd_re = jnp.einsum('gnp,gh->gphn', ssm_b_re[l], eye_g).reshape(MIX_W, SSM_STATE)
        bd_im = jnp.einsum('gnp,gh->gphn', ssm_b_im[l], eye_g).reshape(MIX_W, SSM_STATE)
        cd_re = jnp.einsum('gpn,gh->gnhp', ssm_c_re[l], eye_g).reshape(SSM_STATE, MIX_W).astype(BF16)
        cd_im = jnp.einsum('gpn,gh->gnhp', ssm_c_im[l], eye_g).reshape(SSM_STATE, MIX_W).astype(BF16)
        ldt = jnp.repeat(ssm_log_dt[l], SSM_N)[None, :]
        ab, bbr, bbi = _ssm_params(ssm_a_re[l].reshape(1, SSM_STATE), ssm_a_im[l].reshape(1, SSM_STATE), ldt,
                                   bd_re, bd_im)
        ssm_consts = (ab, bbr, bbi, cd_re, cd_im, ssm_d[l][None, :], w_glu[l].astype(BF16), b_glu[l][None, :])
        yc_p, hst_p = _ssm_prompt(proj, *ssm_consts)
        yc_s, nre_s, nim_s = _ssm_sample(proj_s, state_ssm_re[l].reshape(DEC_BATCH, SSM_STATE),
                                         state_ssm_im[l].reshape(DEC_BATCH, SSM_STATE), *ssm_consts)
        yc = jnp.concatenate([yc_p, yc_s.reshape(MS, MIX_W)], axis=0)

        merged = _gate_merge(x, norm_mix[l][None, :], w_gate, (ya, yb, yc, yd), w_branch[l].astype(BF16), 544, 512)
        x = _mm_res(merged, w_out[l].astype(BF16), x, 1088, 512)
        act = _swiglu(x, norm_ffn[l][None, :], w_ffn_in[l].astype(BF16), 1088, 512)
        x = _mm_res(act, w_ffn_out[l].astype(BF16), x, 544, 512)

        v_f = proj[:, CB_V * MIX_W:(CB_V + 1) * MIX_W]
        gv_s = proj[MP:, CB_GV * MIX_W:(CB_GV + 1) * MIX_W]
        st_p.append((k_r[:MP].reshape(BATCH, SEQ, N_HEADS, HEAD_DIM), v_f[:MP].reshape(BATCH, SEQ, N_HEADS, HEAD_DIM),
                     tail_r[:MP, :IDX_DIM].reshape(BATCH, SEQ, IDX_DIM), buf_p,
                     hst_p[:, 0].reshape(BATCH, SSM_GROUPS, SSM_N), hst_p[:, 1].reshape(BATCH, SSM_GROUPS, SSM_N)))
        st_s.append((k_r[MP:].reshape(DEC_BATCH, DEC_SEQ, N_HEADS, HEAD_DIM),
                     v_f[MP:].reshape(DEC_BATCH, DEC_SEQ, N_HEADS, HEAD_DIM),
                     tail_r[MP:, :IDX_DIM].reshape(DEC_BATCH, DEC_SEQ, IDX_DIM), buf_s,
                     nre_s.reshape(DEC_BATCH, SSM_GROUPS, SSM_N), nim_s.reshape(DEC_BATCH, SSM_GROUPS, SSM_N),
                     gv_s.reshape(DEC_BATCH, DEC_SEQ, MIX_W)))

    y_p = _final_norm(x, norm_final[None, :], 0, MP, MS)
    y_s = _final_norm(x, norm_final[None, :], MP, MS, MS)
    outs = [y_p.reshape(BATCH, SEQ, D_MODEL), y_s.reshape(DEC_BATCH, DEC_SEQ, D_MODEL)]
    outs += [jnp.stack([s[i] for s in st_p]) for i in range(6)]
    outs += [jnp.stack([s[i] for s in st_s]) for i in range(7)]
    return tuple(outs)
```

```python
import functools

import jax
import jax.numpy as jnp
from jax import lax
from jax.experimental import pallas as pl
from jax.experimental.pallas import tpu as pltpu

F32 = jnp.float32
BF16 = jnp.bfloat16
I32 = jnp.int32

D_MODEL = 2048
BATCH = 4
SEQ = 2048
DEPTH = 2
DEC_BATCH = 128
DEC_SEQ = 4
PAGE_SIZE = 128
N_PAGES = 16
PAST = N_PAGES * PAGE_SIZE
MIX_W = 512
N_HEADS = 4
HEAD_DIM = 128
ROT_DIM = 32
IDX_HEADS = 8
IDX_DIM = 64
IDX_ROT = 16
TOPK = 256
QBLOCK = 128
ROPE_THETA = 500000.0
SSM_P = 16
SSM_GROUPS = 32
SSM_N = 64
SSM_STATE = SSM_GROUPS * SSM_N
GMLP_CHUNK = 128
GMLP_GROUPS = 4
D_FF = 5632
N_BRANCH = 4

MP = BATCH * SEQ
MS = DEC_BATCH * DEC_SEQ
M = MP + MS
LANES = 128
TAIL_COL = 10 * MIX_W
N_MAIN = TAIL_COL + 2 * LANES
CB_Q, CB_K, CB_V, CB_QI, CB_CX, CB_CB, CB_CC, CB_SU, CB_GU, CB_GV = range(10)
SAMPLE_KEYS = PAST + LANES
NEG_INF = float("-inf")
INT_MIN = -2 ** 31
VMEM_LIMIT = 56 * 1024 * 1024


def _params(n_axes, vmem=VMEM_LIMIT):
    return pltpu.CompilerParams(dimension_semantics=("arbitrary",) * n_axes, vmem_limit_bytes=vmem)


def _rms(x, g):
    return x * lax.rsqrt(jnp.mean(x * x, axis=-1, keepdims=True) + 1e-6) * g


def _dot(a, b):
    return jnp.dot(a, b, preferred_element_type=F32)


def _dot_nt(a, b):
    return lax.dot_general(a, b, (((1,), (1,)), ((), ())), preferred_element_type=F32)


def _sigmoid(x):
    return 1.0 / (1.0 + jnp.exp(-x))


def _norm_mm_kernel(x_ref, g_ref, w_ref, o_ref, h_ref):
    @pl.when(pl.program_id(1) == 0)
    def _():
        h_ref[...] = _rms(x_ref[...], g_ref[...]).astype(BF16)

    o_ref[...] = _dot(h_ref[...], w_ref[...])


def _norm_mm(x, g, w, tm, tn):
    m, k = x.shape
    n = w.shape[1]
    return pl.pallas_call(
        _norm_mm_kernel,
        grid=(m // tm, n // tn),
        in_specs=[pl.BlockSpec((tm, k), lambda i, j: (i, 0)),
                  pl.BlockSpec((1, k), lambda i, j: (0, 0)),
                  pl.BlockSpec((k, tn), lambda i, j: (0, j))],
        out_specs=pl.BlockSpec((tm, tn), lambda i, j: (i, j)),
        out_shape=jax.ShapeDtypeStruct((m, n), F32),
        scratch_shapes=[pltpu.VMEM((tm, k), BF16)],
        compiler_params=_params(2),
        name="norm_mm",
    )(x, g, w)


def _gate_merge_kernel(x_ref, g_ref, wg0, wg1, wg2, wg3, y0, y1, y2, y3, wb_ref, o_ref, h_ref):
    @pl.when(pl.program_id(1) == 0)
    def _():
        h_ref[...] = _rms(x_ref[...], g_ref[...]).astype(BF16)

    h = h_ref[...]
    acc = None
    for kk, (wg, y) in enumerate(((wg0, y0), (wg1, y1), (wg2, y2), (wg3, y3))):
        gate = _sigmoid(_dot(h, wg[...]))
        term = gate * _dot(y[...], wb_ref[kk])
        acc = term if acc is None else acc + term
    o_ref[...] = acc.astype(BF16)


def _gate_merge(x, g, wg, ys, wb, tm, tn):
    nj = D_MODEL // tn
    wg_specs = [pl.BlockSpec((D_MODEL, tn), functools.partial(lambda i, j, kk: (0, kk * nj + j), kk=kk))
                for kk in range(N_BRANCH)]
    y_specs = [pl.BlockSpec((tm, MIX_W), lambda i, j: (i, 0)) for _ in range(N_BRANCH)]
    return pl.pallas_call(
        _gate_merge_kernel,
        grid=(M // tm, nj),
        in_specs=[pl.BlockSpec((tm, D_MODEL), lambda i, j: (i, 0)),
                  pl.BlockSpec((1, D_MODEL), lambda i, j: (0, 0))] + wg_specs + y_specs
                 + [pl.BlockSpec((N_BRANCH, MIX_W, tn), lambda i, j: (0, 0, j))],
        out_specs=pl.BlockSpec((tm, tn), lambda i, j: (i, j)),
        out_shape=jax.ShapeDtypeStruct((M, D_MODEL), BF16),
        scratch_shapes=[pltpu.VMEM((tm, D_MODEL), BF16)],
        compiler_params=_params(2),
        name="gate_merge",
    )(x, g, wg, wg, wg, wg, *ys, wb)


def _mm_res_kernel(a_ref, w_ref, r_ref, o_ref):
    o_ref[...] = r_ref[...] + _dot(a_ref[...], w_ref[...])


def _mm_res(a, w, res, tm, tn):
    m, k = a.shape
    n = w.shape[1]
    return pl.pallas_call(
        _mm_res_kernel,
        grid=(m // tm, n // tn),
        in_specs=[pl.BlockSpec((tm, k), lambda i, j: (i, 0)),
                  pl.BlockSpec((k, tn), lambda i, j: (0, j)),
                  pl.BlockSpec((tm, tn), lambda i, j: (i, j))],
        out_specs=pl.BlockSpec((tm, tn), lambda i, j: (i, j)),
        out_shape=jax.ShapeDtypeStruct((m, n), F32),
        compiler_params=_params(2),
        name="mm_res",
    )(a, w, res)


def _swiglu_kernel(x_ref, g_ref, wa_ref, wb_ref, o_ref, h_ref):
    @pl.when(pl.program_id(1) == 0)
    def _():
        h_ref[...] = _rms(x_ref[...], g_ref[...]).astype(BF16)

    h = h_ref[...]
    a = _dot(h, wa_ref[...])
    b = _dot(h, wb_ref[...])
    o_ref[...] = (a * _sigmoid(a) * b).astype(BF16)


def _swiglu(x, g, w, tm, tn):
    nj = D_FF // tn
    return pl.pallas_call(
        _swiglu_kernel,
        grid=(M // tm, nj),
        in_specs=[pl.BlockSpec((tm, D_MODEL), lambda i, j: (i, 0)),
                  pl.BlockSpec((1, D_MODEL), lambda i, j: (0, 0)),
                  pl.BlockSpec((D_MODEL, tn), lambda i, j: (0, j)),
                  pl.BlockSpec((D_MODEL, tn), lambda i, j: (0, nj + j))],
        out_specs=pl.BlockSpec((tm, tn), lambda i, j: (i, j)),
        out_shape=jax.ShapeDtypeStruct((M, D_FF), BF16),
        scratch_shapes=[pltpu.VMEM((tm, D_MODEL), BF16)],
        compiler_params=_params(2),
        name="swiglu",
    )(x, g, w, w)


def _final_norm_kernel(x_ref, g_ref, o_ref):
    o_ref[...] = _rms(x_ref[...], g_ref[...])


def _final_norm(x, g, row0, rows, tm):
    return pl.pallas_call(
        _final_norm_kernel,
        grid=(rows // tm,),
        in_specs=[pl.BlockSpec((tm, D_MODEL), lambda i: (row0 // tm + i, 0)),
                  pl.BlockSpec((1, D_MODEL), lambda i: (0, 0))],
        out_specs=pl.BlockSpec((tm, D_MODEL), lambda i: (i, 0)),
        out_shape=jax.ShapeDtypeStruct((rows, D_MODEL), F32),
        compiler_params=_params(1),
        name="final_norm",
    )(x, g)


PREP_TM = 256


def _prep_kernel(q_ref, k_ref, v_ref, qi_ref, t_ref, inv_ref, qo, ko, kbo, vbo, qio, to, ki2o):
    i = pl.program_id(0)
    row = i * PREP_TM + lax.broadcasted_iota(I32, (PREP_TM, LANES), 0)
    lane = lax.broadcasted_iota(I32, (PREP_TM, LANES), 1)
    pos = jnp.where(row < MP, row & (SEQ - 1), PAST + (row & (DEC_SEQ - 1))).astype(F32)

    def rope(x, inv_row, half):
        ang = pos * inv_row
        c, s = jnp.cos(ang), jnp.sin(ang)
        upper = (lane & half) != 0
        partner = jnp.where(upper, pltpu.roll(x, half, 1), -pltpu.roll(x, LANES - half, 1))
        return x * c + partner * s

    inv_qk, inv_i, inv_t = inv_ref[0:1, :], inv_ref[1:2, :], inv_ref[2:3, :]
    for h in range(N_HEADS):
        sl = slice(h * LANES, (h + 1) * LANES)
        qo[:, sl] = rope(q_ref[:, sl], inv_qk, ROT_DIM // 2).astype(BF16)
        kr = rope(k_ref[:, sl], inv_qk, ROT_DIM // 2)
        ko[:, sl] = kr
        kbo[:, sl] = kr.astype(BF16)
        qio[:, sl] = rope(qi_ref[:, sl], inv_i, IDX_ROT // 2).astype(BF16)
    vbo[...] = v_ref[...].astype(BF16)
    tr = rope(t_ref[...], inv_t, IDX_ROT // 2)
    to[...] = tr
    ki = jnp.where(lane < IDX_DIM, tr, 0.0)
    ki2o[:, :LANES] = ki.astype(BF16)
    ki2o[:, LANES:] = pltpu.roll(ki, IDX_DIM, 1).astype(BF16)


def _prep(proj, inv):
    def cb(c):
        return pl.BlockSpec((PREP_TM, MIX_W), lambda i: (i, c))

    row512 = pl.BlockSpec((PREP_TM, MIX_W), lambda i: (i, 0))
    return pl.pallas_call(
        _prep_kernel,
        grid=(M // PREP_TM,),
        in_specs=[cb(CB_Q), cb(CB_K), cb(CB_V), cb(CB_QI),
                  pl.BlockSpec((PREP_TM, LANES), lambda i: (i, TAIL_COL // LANES)),
                  pl.BlockSpec((8, LANES), lambda i: (0, 0))],
        out_specs=[row512, row512, row512, row512, row512,
                   pl.BlockSpec((PREP_TM, LANES), lambda i: (i, 0)),
                   pl.BlockSpec((PREP_TM, 2 * LANES), lambda i: (i, 0))],
        out_shape=[jax.ShapeDtypeStruct((M, MIX_W), BF16),
                   jax.ShapeDtypeStruct((M, MIX_W), F32),
                   jax.ShapeDtypeStruct((M, MIX_W), BF16),
                   jax.ShapeDtypeStruct((M, MIX_W), BF16),
                   jax.ShapeDtypeStruct((M, MIX_W), BF16),
                   jax.ShapeDtypeStruct((M, LANES), F32),
                   jax.ShapeDtypeStruct((M, 2 * LANES), BF16)],
        compiler_params=_params(1),
        name="prep",
    )(proj, proj, proj, proj, proj, inv)


KEY_NEG_INF = INT_MIN + 0x7FFFFF


def _key_to_float(key):
    return lax.bitcast_convert_type(jnp.where(key < 0, key ^ 0x7FFFFFFF, key), F32)


def _select_chunks(sc_ref, bias_ref, nc, k, allowed_fn):
    _, rows, cw = sc_ref.shape
    kf = float(k)

    def count_ge(c):
        acc = jnp.where(sc_ref[0] >= c, 1.0, 0.0)
        for j in range(1, nc):
            acc = acc + jnp.where(sc_ref[j] >= c, 1.0, 0.0)
        return jnp.sum(acc, axis=-1, keepdims=True)

    t = jnp.where(count_ge(jnp.zeros((rows, 1), F32)) >= kf, 0, INT_MIN).astype(I32)
    for bit in range(30, -1, -1):
        cand = t + (1 << bit)
        ok = jnp.logical_or(count_ge(_key_to_float(cand)) >= kf, cand <= KEY_NEG_INF)
        t = jnp.where(ok, cand, t)
    lo = _key_to_float(t)
    hi = _key_to_float(t + 1)
    need = kf - count_ge(hi)
    r_io = lax.broadcasted_iota(I32, (LANES, LANES), 0)
    c_io = lax.broadcasted_iota(I32, (LANES, LANES), 1)
    tri = jnp.where(r_io <= c_io, 1.0, 0.0).astype(BF16)

    off = jnp.zeros((rows, 1), F32)
    for j in range(nc):
        for u in range(cw // LANES):
            sl = slice(u * LANES, (u + 1) * LANES)
            su = sc_ref[j, :, sl]
            above = su >= hi
            e = jnp.where(above, 0.0, jnp.where(su >= lo, 1.0, 0.0))
            rank = _dot(e.astype(BF16), tri) + off
            off = off + jnp.sum(e, axis=-1, keepdims=True)
            take = jnp.where(above, 1.0, jnp.where(rank <= need, e, 0.0))
            take = jnp.where(allowed_fn(j, u, su), take, 0.0)
            bias_ref[j, :, sl] = jnp.where(take > 0.5, 0.0, NEG_INF)


def _index_scores(qi, w_scaled, ki_a, ki_b, n_keys):
    acc = None
    for p in range(IDX_HEADS // 2):
        qp = qi[:, p * LANES:(p + 1) * LANES]
        for half, ki in enumerate((ki_a, ki_b)):
            hh = 2 * p + half
            d = _dot_nt(qp, ki) * (IDX_DIM ** -0.5)
            term = jnp.maximum(d, 0.0) * w_scaled[:, IDX_DIM + hh:IDX_DIM + hh + 1]
            acc = term if acc is None else acc + term
    return acc


KEY_CHUNK = 256
N_KEY_CHUNKS = SEQ // KEY_CHUNK


def _dsa_prompt_kernel(q_ref, qi_ref, t_ref, k_ref, v_ref, ki2_ref, o_ref, sc_ref, bias_ref):
    i = pl.program_id(1)
    row = i * QBLOCK + lax.broadcasted_iota(I32, (QBLOCK, KEY_CHUNK), 0)
    lane = lax.broadcasted_iota(I32, (QBLOCK, KEY_CHUNK), 1)
    w_scaled = t_ref[...] * (IDX_HEADS ** -0.5)
    sc = _index_scores(qi_ref[...], w_scaled, ki2_ref[:, :LANES], ki2_ref[:, LANES:], SEQ)
    for j in range(N_KEY_CHUNKS):
        sc_ref[j] = jnp.where(j * KEY_CHUNK + lane <= row, sc[:, j * KEY_CHUNK:(j + 1) * KEY_CHUNK], NEG_INF)

    row_g = i * QBLOCK + lax.broadcasted_iota(I32, (QBLOCK, LANES), 0)
    lane_g = lax.broadcasted_iota(I32, (QBLOCK, LANES), 1)

    def causal(j, u, su):
        return j * KEY_CHUNK + u * LANES + lane_g <= row_g

    _select_chunks(sc_ref, bias_ref, N_KEY_CHUNKS, TOPK, causal)
    bias = jnp.concatenate([bias_ref[j] for j in range(N_KEY_CHUNKS)], axis=-1)
    for h in range(N_HEADS):
        sl = slice(h * HEAD_DIM, (h + 1) * HEAD_DIM)
        s = _dot_nt(q_ref[:, sl], k_ref[:, sl]) * (HEAD_DIM ** -0.5) + bias
        m = jnp.max(s, axis=-1, keepdims=True)
        p = jnp.exp(s - m)
        l = jnp.sum(p, axis=-1, keepdims=True)
        o_ref[:, sl] = (_dot(p.astype(BF16), v_ref[:, sl]) / l).astype(BF16)


def _dsa_prompt(q, qi, tail, kb, vb, ki2):
    nb = SEQ // QBLOCK

    def qspec(w):
        return pl.BlockSpec((QBLOCK, w), lambda b, i: (b * nb + i, 0))

    def kspec(w):
        return pl.BlockSpec((SEQ, w), lambda b, i: (b, 0))

    return pl.pallas_call(
        _dsa_prompt_kernel,
        grid=(BATCH, nb),
        in_specs=[qspec(MIX_W), qspec(MIX_W), qspec(LANES), kspec(MIX_W), kspec(MIX_W), kspec(2 * LANES)],
        out_specs=qspec(MIX_W),
        out_shape=jax.ShapeDtypeStruct((MP, MIX_W), BF16),
        scratch_shapes=[pltpu.VMEM((N_KEY_CHUNKS, QBLOCK, KEY_CHUNK), F32)] * 2,
        compiler_params=_params(2),
        name="dsa_prompt",
    )(q, qi, tail, kb, vb, ki2)


def _dsa_sample_scores_kernel(pt_ref, qi_ref, w_ref, *rest):
    page_refs, knew_ref, o_ref = rest[:N_PAGES], rest[N_PAGES], rest[N_PAGES + 1]
    qi = qi_ref[0]
    w = w_ref[0] * (IDX_HEADS ** -0.5)

    def chunk_scores(d):
        r = jnp.maximum(d * (IDX_DIM ** -0.5), 0.0) * w
        return jnp.sum(r.reshape(DEC_SEQ, IDX_HEADS, LANES), axis=1)

    for p in range(N_PAGES):
        o_ref[0, :, p * LANES:(p + 1) * LANES] = chunk_scores(_dot(qi, page_refs[p][0, 0].astype(BF16)))
    new = chunk_scores(_dot_nt(qi, knew_ref[0]))
    tq = lax.broadcasted_iota(I32, (DEC_SEQ, LANES), 0)
    jk = lax.broadcasted_iota(I32, (DEC_SEQ, LANES), 1)
    o_ref[0, :, PAST:] = jnp.where(jk <= tq, new, NEG_INF)


def _dsa_sample_scores(page_table, qi32, w32, cache_kidx_t, layer, knew_pad):
    page_specs = [pl.BlockSpec((1, 1, IDX_DIM, PAGE_SIZE),
                               functools.partial(lambda b, pt, p: (layer, pt[b, p], 0, 0), p=p))
                  for p in range(N_PAGES)]
    grid_spec = pltpu.PrefetchScalarGridSpec(
        num_scalar_prefetch=1,
        grid=(DEC_BATCH,),
        in_specs=[pl.BlockSpec((1, DEC_SEQ * IDX_HEADS, IDX_DIM), lambda b, pt: (b, 0, 0)),
                  pl.BlockSpec((1, DEC_SEQ * IDX_HEADS, 1), lambda b, pt: (b, 0, 0))] + page_specs
                 + [pl.BlockSpec((1, LANES, IDX_DIM), lambda b, pt: (b, 0, 0))],
        out_specs=pl.BlockSpec((1, DEC_SEQ, SAMPLE_KEYS), lambda b, pt: (b, 0, 0)),
    )
    return pl.pallas_call(
        _dsa_sample_scores_kernel,
        grid_spec=grid_spec,
        out_shape=jax.ShapeDtypeStruct((DEC_BATCH, DEC_SEQ, SAMPLE_KEYS), F32),
        compiler_params=_params(1),
        name="dsa_sample_scores",
    )(page_table, qi32, w32, *([cache_kidx_t] * N_PAGES), knew_pad)


SEL_ROWS = 128
N_SAMPLE_CHUNKS = SAMPLE_KEYS // LANES


def _sample_select_kernel(in_ref, o_ref, sc_ref, bias_ref):
    for c in range(N_SAMPLE_CHUNKS):
        sc_ref[c] = in_ref[:, c * LANES:(c + 1) * LANES]
    _select_chunks(sc_ref, bias_ref, N_SAMPLE_CHUNKS, TOPK, lambda j, u, su: su > NEG_INF)
    for c in range(N_SAMPLE_CHUNKS):
        o_ref[:, c * LANES:(c + 1) * LANES] = bias_ref[c]


def _sample_select(sc):
    return pl.pallas_call(
        _sample_select_kernel,
        grid=(MS // SEL_ROWS,),
        in_specs=[pl.BlockSpec((SEL_ROWS, SAMPLE_KEYS), lambda i: (i, 0))],
        out_specs=pl.BlockSpec((SEL_ROWS, SAMPLE_KEYS), lambda i: (i, 0)),
        out_shape=jax.ShapeDtypeStruct((MS, SAMPLE_KEYS), F32),
        scratch_shapes=[pltpu.VMEM((N_SAMPLE_CHUNKS, SEL_ROWS, LANES), F32)] * 2,
        compiler_params=_params(1),
        name="sample_select",
    )(sc)


def _dsa_sample_attn_kernel(pt_ref, q_ref, bias_ref, *rest):
    k_refs = rest[:N_PAGES]
    v_refs = rest[N_PAGES:2 * N_PAGES]
    knew_ref, vnew_ref, o_ref, s_ref = rest[2 * N_PAGES:]
    q = q_ref[0]
    def head_rows(h):
        return pl.ds(h, PAGE_SIZE, stride=N_HEADS)

    for h in range(N_HEADS):
        rs = slice(h * DEC_SEQ, (h + 1) * DEC_SEQ)
        hs = slice(h * HEAD_DIM, (h + 1) * HEAD_DIM)
        qh = q[:, hs]
        for p in range(N_PAGES):
            s_ref[rs, p * LANES:(p + 1) * LANES] = _dot_nt(qh, k_refs[p][0, 0, head_rows(h), :].astype(BF16))
        s_ref[rs, PAST:] = _dot_nt(qh, knew_ref[0, :, hs])
    bias = bias_ref[0]
    s = s_ref[...] * (HEAD_DIM ** -0.5) + jnp.concatenate([bias] * N_HEADS, axis=0)
    m = jnp.max(s, axis=-1, keepdims=True)
    pr = jnp.exp(s - m)
    l = jnp.sum(pr, axis=-1, keepdims=True)
    pb = pr.astype(BF16)
    for h in range(N_HEADS):
        rs = slice(h * DEC_SEQ, (h + 1) * DEC_SEQ)
        hs = slice(h * HEAD_DIM, (h + 1) * HEAD_DIM)
        acc = _dot(pb[rs, PAST:], vnew_ref[0, :, hs])
        for p in range(N_PAGES):
            acc = acc + _dot(pb[rs, p * LANES:(p + 1) * LANES], v_refs[p][0, 0, head_rows(h), :].astype(BF16))
        o_ref[0, :, hs] = (acc / l[rs]).astype(BF16)


def _dsa_sample_attn(page_table, q, bias, cache_k, cache_v, layer, knew_pad, vnew_pad):
    def page_spec(p):
        return pl.BlockSpec((1, 1, PAGE_SIZE * N_HEADS, HEAD_DIM),
                            functools.partial(lambda b, pt, p: (layer, pt[b, p], 0, 0), p=p))

    grid_spec = pltpu.PrefetchScalarGridSpec(
        num_scalar_prefetch=1,
        grid=(DEC_BATCH,),
        in_specs=[pl.BlockSpec((1, DEC_SEQ, MIX_W), lambda b, pt: (b, 0, 0)),
                  pl.BlockSpec((1, DEC_SEQ, SAMPLE_KEYS), lambda b, pt: (b, 0, 0))]
                 + [page_spec(p) for p in range(N_PAGES)] + [page_spec(p) for p in range(N_PAGES)]
                 + [pl.BlockSpec((1, LANES, MIX_W), lambda b, pt: (b, 0, 0)),
                    pl.BlockSpec((1, LANES, MIX_W), lambda b, pt: (b, 0, 0))],
        out_specs=pl.BlockSpec((1, DEC_SEQ, MIX_W), lambda b, pt: (b, 0, 0)),
        scratch_shapes=[pltpu.VMEM((N_HEADS * DEC_SEQ, SAMPLE_KEYS), F32)],
    )
    return pl.pallas_call(
        _dsa_sample_attn_kernel,
        grid_spec=grid_spec,
        out_shape=jax.ShapeDtypeStruct((DEC_BATCH, DEC_SEQ, MIX_W), BF16),
        compiler_params=_params(1),
        name="dsa_sample_attn",
    )(page_table, q, bias, *([cache_k] * N_PAGES), *([cache_v] * N_PAGES), knew_pad, vnew_pad)


CONV_TT = 512


def _conv_prompt_kernel(cx_ref, cb_ref, cc_ref, w_ref, yb_ref, buf_ref, carry_ref):
    j = pl.program_id(1)

    @pl.when(j == 0)
    def _():
        carry_ref[...] = jnp.zeros_like(carry_ref)

    z = cc_ref[...] * cx_ref[...]
    row = lax.broadcasted_iota(I32, z.shape, 0)
    c0 = carry_ref[0:1, :]
    c1 = carry_ref[1:2, :]
    zm1 = jnp.where(row == 0, c1, pltpu.roll(z, 1, 0))
    zm2 = jnp.where(row == 0, c0, jnp.where(row == 1, c1, pltpu.roll(z, 2, 0)))
    zc = w_ref[0:1, :] * zm2 + w_ref[1:2, :] * zm1 + w_ref[2:3, :] * z
    yb_ref[...] = (cb_ref[...] * zc).astype(BF16)
    last = z[CONV_TT - 2:CONV_TT, :]
    carry_ref[0:2, :] = last
    buf_ref[0] = last


def _conv_prompt(proj, conv_w):
    nt = SEQ // CONV_TT

    def cb(c):
        return pl.BlockSpec((CONV_TT, MIX_W), lambda b, j: (b * nt + j, c))

    return pl.pallas_call(
        _conv_prompt_kernel,
        grid=(BATCH, nt),
        in_specs=[cb(CB_CX), cb(CB_CB), cb(CB_CC), pl.BlockSpec((3, MIX_W), lambda b, j: (0, 0))],
        out_specs=[pl.BlockSpec((CONV_TT, MIX_W), lambda b, j: (b * nt + j, 0)),
                   pl.BlockSpec((1, 2, MIX_W), lambda b, j: (b, 0, 0))],
        out_shape=[jax.ShapeDtypeStruct((MP, MIX_W), BF16),
                   jax.ShapeDtypeStruct((BATCH, 2, MIX_W), F32)],
        scratch_shapes=[pltpu.VMEM((8, MIX_W), F32)],
        compiler_params=_params(2),
        name="conv_prompt",
    )(proj, proj, proj, conv_w)


def _gmlp_prompt_kernel(gu_ref, gv_ref, w_ref, bt_ref, o_ref):
    r_io = lax.broadcasted_iota(I32, (GMLP_CHUNK, GMLP_CHUNK), 0)
    c_io = lax.broadcasted_iota(I32, (GMLP_CHUNK, GMLP_CHUNK), 1)
    tril = c_io <= r_io
    for g in range(GMLP_GROUPS):
        sl = slice(g * LANES, (g + 1) * LANES)
        wm = jnp.where(tril, w_ref[g], 0.0).astype(BF16)
        z = _dot(wm, gv_ref[:, sl].astype(BF16)) + bt_ref[:, g:g + 1]
        o_ref[:, sl] = (gu_ref[:, sl] * z).astype(BF16)


def _gmlp_prompt(proj, gmlp_w, gmlp_bt):
    def cb(c):
        return pl.BlockSpec((GMLP_CHUNK, MIX_W), lambda i: (i, c))

    return pl.pallas_call(
        _gmlp_prompt_kernel,
        grid=(MP // GMLP_CHUNK,),
        in_specs=[cb(CB_GU), cb(CB_GV),
                  pl.BlockSpec((GMLP_GROUPS, GMLP_CHUNK, GMLP_CHUNK), lambda i: (0, 0, 0)),
                  pl.BlockSpec((GMLP_CHUNK, GMLP_GROUPS), lambda i: (0, 0))],
        out_specs=pl.BlockSpec((GMLP_CHUNK, MIX_W), lambda i: (i, 0)),
        out_shape=jax.ShapeDtypeStruct((MP, MIX_W), BF16),
        compiler_params=_params(1),
        name="gmlp_prompt",
    )(proj, proj, gmlp_w, gmlp_bt)


def _mix_sample_kernel(cx_ref, cb_ref, cc_ref, gu_ref, gv_ref, buf_ref, cw_ref, gw_ref, gb_ref,
                       yb_ref, nbuf_ref, yd_ref):
    zf = [buf_ref[:, 0, :], buf_ref[:, 1, :]]
    for t in range(DEC_SEQ):
        zf.append(cc_ref[:, t, :] * cx_ref[:, t, :])
    for t in range(DEC_SEQ):
        zc = cw_ref[0:1, :] * zf[t] + cw_ref[1:2, :] * zf[t + 1] + cw_ref[2:3, :] * zf[t + 2]
        yb_ref[:, t, :] = (cb_ref[:, t, :] * zc).astype(BF16)
    nbuf_ref[:, 0, :] = zf[DEC_SEQ]
    nbuf_ref[:, 1, :] = zf[DEC_SEQ + 1]
    v = [gv_ref[:, s, :] for s in range(DEC_SEQ)]
    for t in range(DEC_SEQ):
        z = gb_ref[t:t + 1, :]
        for s in range(t + 1):
            z = z + gw_ref[t * DEC_SEQ + s:t * DEC_SEQ + s + 1, :] * v[s]
        yd_ref[:, t, :] = (gu_ref[:, t, :] * z).astype(BF16)


def _mix_sample(proj_s, state_conv_l, conv_w, gw_lane, gb_lane):
    def cb(c):
        return pl.BlockSpec((DEC_BATCH, DEC_SEQ, MIX_W), lambda i: (0, 0, c))

    full3 = pl.BlockSpec((DEC_BATCH, DEC_SEQ, MIX_W), lambda i: (0, 0, 0))
    buf3 = pl.BlockSpec((DEC_BATCH, 2, MIX_W), lambda i: (0, 0, 0))
    return pl.pallas_call(
        _mix_sample_kernel,
        grid=(1,),
        in_specs=[cb(CB_CX), cb(CB_CB), cb(CB_CC), cb(CB_GU), cb(CB_GV), buf3,
                  pl.BlockSpec((3, MIX_W), lambda i: (0, 0)),
                  pl.BlockSpec((DEC_SEQ * DEC_SEQ, MIX_W), lambda i: (0, 0)),
                  pl.BlockSpec((DEC_SEQ, MIX_W), lambda i: (0, 0))],
        out_specs=[full3, buf3, full3],
        out_shape=[jax.ShapeDtypeStruct((DEC_BATCH, DEC_SEQ, MIX_W), BF16),
                   jax.ShapeDtypeStruct((DEC_BATCH, 2, MIX_W), F32),
                   jax.ShapeDtypeStruct((DEC_BATCH, DEC_SEQ, MIX_W), BF16)],
        compiler_params=_params(1),
        name="mix_sample",
    )(proj_s, proj_s, proj_s, proj_s, proj_s, state_conv_l, conv_w, gw_lane, gb_lane)


def _ssm_params_kernel(ar_ref, ai_ref, ldt_ref, br_ref, bi_ref, ab_ref, bbr_ref, bbi_ref):
    ar, ai = ar_ref[...], ai_ref[...]
    dt = jnp.exp(ldt_ref[...])
    mag = jnp.exp(dt * ar)
    abar_re, abar_im = mag * jnp.cos(dt * ai), mag * jnp.sin(dt * ai)
    den = ar * ar + ai * ai
    nr, ni = abar_re - 1.0, abar_im
    coef_re = (nr * ar + ni * ai) / den
    coef_im = (ni * ar - nr * ai) / den
    ab_ref[...] = jnp.zeros_like(ab_ref)
    ab_ref[0:1, :] = abar_re
    ab_ref[1:2, :] = abar_im
    br, bi = br_ref[...], bi_ref[...]
    bbr_ref[...] = (coef_re * br - coef_im * bi).astype(BF16)
    bbi_ref[...] = (coef_re * bi + coef_im * br).astype(BF16)


def _ssm_params(a_re, a_im, ldt, bd_re, bd_im):
    row = pl.BlockSpec((1, SSM_STATE), lambda i: (0, 0))
    mat = pl.BlockSpec((MIX_W, SSM_STATE), lambda i: (0, 0))
    return pl.pallas_call(
        _ssm_params_kernel,
        grid=(1,),
        in_specs=[row, row, row, mat, mat],
        out_specs=[pl.BlockSpec((8, SSM_STATE), lambda i: (0, 0)), mat, mat],
        out_shape=[jax.ShapeDtypeStruct((8, SSM_STATE), F32),
                   jax.ShapeDtypeStruct((MIX_W, SSM_STATE), BF16),
                   jax.ShapeDtypeStruct((MIX_W, SSM_STATE), BF16)],
        compiler_params=_params(1),
        name="ssm_params",
    )(a_re, a_im, ldt, bd_re, bd_im)


def _gelu_tanh(x):
    return 0.5 * x * (1.0 + jnp.tanh(0.7978845608028654 * (x + 0.044715 * (x * x * x))))


def _ssm_readout(u, hr, hi, cr_ref, ci_ref, d_ref, wg_ref, bg_ref):
    y = _dot(hr.astype(BF16), cr_ref[...]) - _dot(hi.astype(BF16), ci_ref[...]) + d_ref[...] * u
    g = _gelu_tanh(y)
    return g * _sigmoid(_dot(g.astype(BF16), wg_ref[...]) + bg_ref[...])


SSM_TC = 256


def _ssm_prompt_kernel(u_ref, ab_ref, bbr_ref, bbi_ref, cr_ref, ci_ref, d_ref, wg_ref, bg_ref,
                       y_ref, st_ref, hr_ref, hi_ref, h_ref):
    j = pl.program_id(1)

    @pl.when(j == 0)
    def _():
        h_ref[...] = jnp.zeros_like(h_ref)

    u = u_ref[...]
    ub = u.astype(BF16)
    hr_ref[...] = _dot(ub, bbr_ref[...])
    hi_ref[...] = _dot(ub, bbi_ref[...])
    ar, ai = ab_ref[0:1, :], ab_ref[1:2, :]

    def step(t, carry):
        pr, pi = carry
        nr = ar * pr - ai * pi + hr_ref[pl.ds(t, 1), :]
        ni = ar * pi + ai * pr + hi_ref[pl.ds(t, 1), :]
        hr_ref[pl.ds(t, 1), :] = nr
        hi_ref[pl.ds(t, 1), :] = ni
        return nr, ni

    fr, fi = lax.fori_loop(0, SSM_TC, step, (h_ref[0:1, :], h_ref[1:2, :]), unroll=8)
    h_ref[0:1, :] = fr
    h_ref[1:2, :] = fi
    st_ref[0, 0:1, :] = fr
    st_ref[0, 1:2, :] = fi
    y_ref[...] = _ssm_readout(u, hr_ref[...], hi_ref[...], cr_ref, ci_ref, d_ref, wg_ref, bg_ref).astype(BF16)


def _ssm_prompt(proj, ab, bbr, bbi, cdr, cdi, d, wglu, bglu):
    nt = SEQ // SSM_TC

    def const(shape):
        return pl.BlockSpec(shape, lambda b, j: (0,) * len(shape))

    return pl.pallas_call(
        _ssm_prompt_kernel,
        grid=(BATCH, nt),
        in_specs=[pl.BlockSpec((SSM_TC, MIX_W), lambda b, j: (b * nt + j, CB_SU)),
                  const((8, SSM_STATE)), const((MIX_W, SSM_STATE)), const((MIX_W, SSM_STATE)),
                  const((SSM_STATE, MIX_W)), const((SSM_STATE, MIX_W)), const((1, MIX_W)),
                  const((MIX_W, MIX_W)), const((1, MIX_W))],
        out_specs=[pl.BlockSpec((SSM_TC, MIX_W), lambda b, j: (b * nt + j, 0)),
                   pl.BlockSpec((1, 2, SSM_STATE), lambda b, j: (b, 0, 0))],
        out_shape=[jax.ShapeDtypeStruct((MP, MIX_W), BF16),
                   jax.ShapeDtypeStruct((BATCH, 2, SSM_STATE), F32)],
        scratch_shapes=[pltpu.VMEM((SSM_TC, SSM_STATE), F32), pltpu.VMEM((SSM_TC, SSM_STATE), F32),
                        pltpu.VMEM((8, SSM_STATE), F32)],
        compiler_params=_params(2),
        name="ssm_prompt",
    )(proj, ab, bbr, bbi, cdr, cdi, d, wglu, bglu)


def _ssm_sample_kernel(u_ref, h0r_ref, h0i_ref, ab_ref, bbr_ref, bbi_ref, cr_ref, ci_ref, d_ref, wg_ref, bg_ref,
                       y_ref, nr_ref, ni_ref):
    ar, ai = ab_ref[0:1, :], ab_ref[1:2, :]
    hr, hi = h0r_ref[...], h0i_ref[...]
    for t in range(DEC_SEQ):
        u = u_ref[:, t, :]
        ub = u.astype(BF16)
        hr, hi = (ar * hr - ai * hi + _dot(ub, bbr_ref[...]),
                  ar * hi + ai * hr + _dot(ub, bbi_ref[...]))
        y_ref[:, t, :] = _ssm_readout(u, hr, hi, cr_ref, ci_ref, d_ref, wg_ref, bg_ref).astype(BF16)
    nr_ref[...] = hr
    ni_ref[...] = hi


def _ssm_sample(proj_s, h0r, h0i, ab, bbr, bbi, cdr, cdi, d, wglu, bglu):
    def const(shape):
        return pl.BlockSpec(shape, lambda i: (0,) * len(shape))

    st = const((DEC_BATCH, SSM_STATE))
    return pl.pallas_call(
        _ssm_sample_kernel,
        grid=(1,),
        in_specs=[pl.BlockSpec((DEC_BATCH, DEC_SEQ, MIX_W), lambda i: (0, 0, CB_SU)), st, st,
                  const((8, SSM_STATE)), const((MIX_W, SSM_STATE)), const((MIX_W, SSM_STATE)),
                  const((SSM_STATE, MIX_W)), const((SSM_STATE, MIX_W)), const((1, MIX_W)),
                  const((MIX_W, MIX_W)), const((1, MIX_W))],
        out_specs=[const((DEC_BATCH, DEC_SEQ, MIX_W)), st, st],
        out_shape=[jax.ShapeDtypeStruct((DEC_BATCH, DEC_SEQ, MIX_W), BF16),
                   jax.ShapeDtypeStruct((DEC_BATCH, SSM_STATE), F32),
                   jax.ShapeDtypeStruct((DEC_BATCH, SSM_STATE), F32)],
        compiler_params=_params(1),
        name="ssm_sample",
    )(proj_s, h0r, h0i, ab, bbr, bbi, cdr, cdi, d, wglu, bglu)


def _rope_inv_rows():
    lane = jnp.arange(LANES)

    def inv(half):
        return ROPE_THETA ** (-jnp.arange(half, dtype=F32) / half)

    inv_qk = jnp.where(lane < ROT_DIM, inv(ROT_DIM // 2)[lane % (ROT_DIM // 2)], 0.0)
    idx = inv(IDX_ROT // 2)[lane % (IDX_ROT // 2)]
    inv_i = jnp.where((lane % IDX_DIM) < IDX_ROT, idx, 0.0)
    inv_t = jnp.where(lane < IDX_ROT, idx, 0.0)
    rows = jnp.stack([inv_qk, inv_i, inv_t]).astype(F32)
    return jnp.concatenate([rows, jnp.zeros((5, LANES), F32)], axis=0)


def _pad_rows(a, rows):
    return jnp.pad(a, ((0, 0), (0, rows - a.shape[1]), (0, 0)))


def kernel(x_prompt, x_sample, cache_k, cache_v, cache_kidx, state_conv, state_ssm_re, state_ssm_im, page_table,
           norm_mix, w_in, conv_w, ssm_a_re, ssm_a_im, ssm_log_dt, ssm_b_re, ssm_b_im, ssm_c_re, ssm_c_im, ssm_d,
           w_glu, b_glu, gmlp_w, gmlp_b, w_branch, w_out, norm_ffn, w_ffn_in, w_ffn_out, norm_final):
    x = jnp.concatenate([x_prompt.reshape(MP, D_MODEL), x_sample.reshape(MS, D_MODEL)], axis=0)
    inv = _rope_inv_rows()
    eye_g = jnp.eye(SSM_GROUPS, dtype=F32)
    cache_kidx_t = jnp.swapaxes(cache_kidx, 2, 3)
    n_pool = cache_k.shape[1]
    cache_k2 = cache_k.reshape(DEPTH, n_pool, PAGE_SIZE * N_HEADS, HEAD_DIM)
    cache_v2 = cache_v.reshape(DEPTH, n_pool, PAGE_SIZE * N_HEADS, HEAD_DIM)
    n_gate0 = w_in.shape[2] - N_BRANCH * D_MODEL
    c_ki = 3 * MIX_W + IDX_HEADS * IDX_DIM
    c_cx = c_ki + IDX_DIM + IDX_HEADS

    st_p, st_s = [], []
    for l in range(DEPTH):
        wl = w_in[l]
        w_main = jnp.concatenate(
            [wl[:, :c_ki], wl[:, c_cx:n_gate0], wl[:, c_ki:c_cx],
             jnp.zeros((D_MODEL, N_MAIN - n_gate0), F32)], axis=1).astype(BF16)
        w_gate = wl[:, n_gate0:].astype(BF16)

        proj = _norm_mm(x, norm_mix[l][None, :], w_main, 1088, 896)
        q_r, k_r, k_b, v_b, qi_r, tail_r, ki2 = _prep(proj, inv)

        ya_p = _dsa_prompt(q_r, qi_r, tail_r, k_b, v_b, ki2)
        qi_s = qi_r[MP:].reshape(DEC_BATCH, DEC_SEQ * IDX_HEADS, IDX_DIM)
        tail_s = tail_r[MP:].reshape(DEC_BATCH, DEC_SEQ, LANES)
        w_s = tail_s[:, :, IDX_DIM:IDX_DIM + IDX_HEADS].reshape(DEC_BATCH, DEC_SEQ * IDX_HEADS, 1)
        kinew = _pad_rows(tail_s[:, :, :IDX_DIM].astype(BF16), LANES)
        sc_s = _dsa_sample_scores(page_table, qi_s, w_s, cache_kidx_t, l, kinew)
        bias_s = _sample_select(sc_s.reshape(MS, SAMPLE_KEYS)).reshape(DEC_BATCH, DEC_SEQ, SAMPLE_KEYS)
        q_s = q_r[MP:].reshape(DEC_BATCH, DEC_SEQ, MIX_W)
        knew = _pad_rows(k_b[MP:].reshape(DEC_BATCH, DEC_SEQ, MIX_W), LANES)
        vnew = _pad_rows(v_b[MP:].reshape(DEC_BATCH, DEC_SEQ, MIX_W), LANES)
        ya_s = _dsa_sample_attn(page_table, q_s, bias_s, cache_k2, cache_v2, l, knew, vnew)
        ya = jnp.concatenate([ya_p, ya_s.reshape(MS, MIX_W)], axis=0)

        proj_s = proj[MP:].reshape(DEC_BATCH, DEC_SEQ, N_MAIN)
        yb_p, buf_p = _conv_prompt(proj, conv_w[l])
        yd_p = _gmlp_prompt(proj, gmlp_w[l], gmlp_b[l].T)
        gw_lane = jnp.repeat(gmlp_w[l][:, :DEC_SEQ, :DEC_SEQ].transpose(1, 2, 0).reshape(DEC_SEQ * DEC_SEQ, GMLP_GROUPS),
                             LANES, axis=1)
        gb_lane = jnp.repeat(gmlp_b[l][:, :DEC_SEQ].T, LANES, axis=1)
        yb_s, buf_s, yd_s = _mix_sample(proj_s, state_conv[l], conv_w[l], gw_lane, gb_lane)
        yb = jnp.concatenate([yb_p, yb_s.reshape(MS, MIX_W)], axis=0)
        yd = jnp.concatenate([yd_p, yd_s.reshape(MS, MIX_W)], axis=0)

        bd_re = jnp.einsum('gnp,gh->gphn', ssm_b_re[l], eye_g).reshape(MIX_W, SSM_STATE)
        bd_im = jnp.einsum('gnp,gh->gphn', ssm_b_im[l], eye_g).reshape(MIX_W, SSM_STATE)
        cd_re = jnp.einsum('gpn,gh->gnhp', ssm_c_re[l], eye_g).reshape(SSM_STATE, MIX_W).astype(BF16)
        cd_im = jnp.einsum('gpn,gh->gnhp', ssm_c_im[l], eye_g).reshape(SSM_STATE, MIX_W).astype(BF16)
        ldt = jnp.repeat(ssm_log_dt[l], SSM_N)[None, :]
        ab, bbr, bbi = _ssm_params(ssm_a_re[l].reshape(1, SSM_STATE), ssm_a_im[l].reshape(1, SSM_STATE), ldt,
                                   bd_re, bd_im)
        ssm_consts = (ab, bbr, bbi, cd_re, cd_im, ssm_d[l][None, :], w_glu[l].astype(BF16), b_glu[l][None, :])
        yc_p, hst_p = _ssm_prompt(proj, *ssm_consts)
        yc_s, nre_s, nim_s = _ssm_sample(proj_s, state_ssm_re[l].reshape(DEC_BATCH, SSM_STATE),
                                         state_ssm_im[l].reshape(DEC_BATCH, SSM_STATE), *ssm_consts)
        yc = jnp.concatenate([yc_p, yc_s.reshape(MS, MIX_W)], axis=0)

        merged = _gate_merge(x, norm_mix[l][None, :], w_gate, (ya, yb, yc, yd), w_branch[l].astype(BF16), 544, 512)
        x = _mm_res(merged, w_out[l].astype(BF16), x, 1088, 512)
        act = _swiglu(x, norm_ffn[l][None, :], w_ffn_in[l].astype(BF16), 1088, 512)
        x = _mm_res(act, w_ffn_out[l].astype(BF16), x, 544, 512)

        v_f = proj[:, CB_V * MIX_W:(CB_V + 1) * MIX_W]
        gv_s = proj[MP:, CB_GV * MIX_W:(CB_GV + 1) * MIX_W]
        st_p.append((k_r[:MP].reshape(BATCH, SEQ, N_HEADS, HEAD_DIM), v_f[:MP].reshape(BATCH, SEQ, N_HEADS, HEAD_DIM),
                     tail_r[:MP, :IDX_DIM].reshape(BATCH, SEQ, IDX_DIM), buf_p,
                     hst_p[:, 0].reshape(BATCH, SSM_GROUPS, SSM_N), hst_p[:, 1].reshape(BATCH, SSM_GROUPS, SSM_N)))
        st_s.append((k_r[MP:].reshape(DEC_BATCH, DEC_SEQ, N_HEADS, HEAD_DIM),
                     v_f[MP:].reshape(DEC_BATCH, DEC_SEQ, N_HEADS, HEAD_DIM),
                     tail_r[MP:, :IDX_DIM].reshape(DEC_BATCH, DEC_SEQ, IDX_DIM), buf_s,
                     nre_s.reshape(DEC_BATCH, SSM_GROUPS, SSM_N), nim_s.reshape(DEC_BATCH, SSM_GROUPS, SSM_N),
                     gv_s.reshape(DEC_BATCH, DEC_SEQ, MIX_W)))

    y_p = _final_norm(x, norm_final[None, :], 0, MP, MS)
    y_s = _final_norm(x, norm_final[None, :], MP, MS, MS)
    outs = [y_p.reshape(BATCH, SEQ, D_MODEL), y_s.reshape(DEC_BATCH, DEC_SEQ, D_MODEL)]
    outs += [jnp.stack([s[i] for s in st_p]) for i in range(6)]
    outs += [jnp.stack([s[i] for s in st_s]) for i in range(7)]
    return tuple(outs)
```

```python
import functools

import jax
import jax.numpy as jnp
from jax import lax
from jax.experimental import pallas as pl
from jax.experimental.pallas import tpu as pltpu

F32 = jnp.float32
BF16 = jnp.bfloat16
I32 = jnp.int32

D_MODEL = 2048
BATCH = 4
SEQ = 2048
DEPTH = 2
DEC_BATCH = 128
DEC_SEQ = 4
PAGE_SIZE = 128
N_PAGES = 16
PAST = N_PAGES * PAGE_SIZE
MIX_W = 512
N_HEADS = 4
HEAD_DIM = 128
ROT_DIM = 32
IDX_HEADS = 8
IDX_DIM = 64
IDX_ROT = 16
TOPK = 256
QBLOCK = 128
ROPE_THETA = 500000.0
SSM_P = 16
SSM_GROUPS = 32
SSM_N = 64
SSM_STATE = SSM_GROUPS * SSM_N
GMLP_CHUNK = 128
GMLP_GROUPS = 4
D_FF = 5632
N_BRANCH = 4

MP = BATCH * SEQ
MS = DEC_BATCH * DEC_SEQ
M = MP + MS
LANES = 128
TAIL_COL = 10 * MIX_W
N_MAIN = TAIL_COL + 2 * LANES
CB_Q, CB_K, CB_V, CB_QI, CB_CX, CB_CB, CB_CC, CB_SU, CB_GU, CB_GV = range(10)
SAMPLE_KEYS = PAST + LANES
NEG_INF = float("-inf")
INT_MIN = -2 ** 31
VMEM_LIMIT = 56 * 1024 * 1024


def _params(n_axes, vmem=VMEM_LIMIT):
    return pltpu.CompilerParams(dimension_semantics=("arbitrary",) * n_axes, vmem_limit_bytes=vmem)


def _rms(x, g):
    return x * lax.rsqrt(jnp.mean(x * x, axis=-1, keepdims=True) + 1e-6) * g


def _dot(a, b):
    return jnp.dot(a, b, preferred_element_type=F32)


def _dot_nt(a, b):
    return lax.dot_general(a, b, (((1,), (1,)), ((), ())), preferred_element_type=F32)


def _sigmoid(x):
    return 1.0 / (1.0 + jnp.exp(-x))


def _norm_mm_kernel(x_ref, g_ref, w_ref, o_ref, h_ref):
    @pl.when(pl.program_id(1) == 0)
    def _():
        h_ref[...] = _rms(x_ref[...], g_ref[...]).astype(BF16)

    o_ref[...] = _dot(h_ref[...], w_ref[...])


def _norm_mm(x, g, w, tm, tn):
    m, k = x.shape
    n = w.shape[1]
    return pl.pallas_call(
        _norm_mm_kernel,
        grid=(m // tm, n // tn),
        in_specs=[pl.BlockSpec((tm, k), lambda i, j: (i, 0)),
                  pl.BlockSpec((1, k), lambda i, j: (0, 0)),
                  pl.BlockSpec((k, tn), lambda i, j: (0, j))],
        out_specs=pl.BlockSpec((tm, tn), lambda i, j: (i, j)),
        out_shape=jax.ShapeDtypeStruct((m, n), F32),
        scratch_shapes=[pltpu.VMEM((tm, k), BF16)],
        compiler_params=_params(2),
        name="norm_mm",
    )(x, g, w)


def _gate_merge_kernel(x_ref, g_ref, wg0, wg1, wg2, wg3, y0, y1, y2, y3, wb_ref, o_ref, h_ref):
    @pl.when(pl.program_id(1) == 0)
    def _():
        h_ref[...] = _rms(x_ref[...], g_ref[...]).astype(BF16)

    h = h_ref[...]
    acc = None
    for kk, (wg, y) in enumerate(((wg0, y0), (wg1, y1), (wg2, y2), (wg3, y3))):
        gate = _sigmoid(_dot(h, wg[...]))
        term = gate * _dot(y[...], wb_ref[0, kk])
        acc = term if acc is None else acc + term
    o_ref[...] = acc.astype(BF16)


def _gate_merge(x, g, wg, ys, wb, layer, tm, tn):
    nj = D_MODEL // tn
    wg_specs = [pl.BlockSpec((D_MODEL, tn), functools.partial(lambda i, j, kk: (0, kk * nj + j), kk=kk))
                for kk in range(N_BRANCH)]
    y_specs = [pl.BlockSpec((tm, MIX_W), lambda i, j: (i, 0)) for _ in range(N_BRANCH)]
    return pl.pallas_call(
        _gate_merge_kernel,
        grid=(M // tm, nj),
        in_specs=[pl.BlockSpec((tm, D_MODEL), lambda i, j: (i, 0)),
                  pl.BlockSpec((1, D_MODEL), lambda i, j: (0, 0))] + wg_specs + y_specs
                 + [pl.BlockSpec((1, N_BRANCH, MIX_W, tn), lambda i, j: (layer, 0, 0, j))],
        out_specs=pl.BlockSpec((tm, tn), lambda i, j: (i, j)),
        out_shape=jax.ShapeDtypeStruct((M, D_MODEL), BF16),
        scratch_shapes=[pltpu.VMEM((tm, D_MODEL), BF16)],
        compiler_params=_params(2),
        name="gate_merge",
    )(x, g, wg, wg, wg, wg, *ys, wb)


def _mm_res_kernel(a_ref, w_ref, r_ref, o_ref):
    o_ref[...] = r_ref[...] + _dot(a_ref[...], w_ref[0])


def _mm_res(a, w, layer, res, tm, tn):
    m, k = a.shape
    n = w.shape[2]
    return pl.pallas_call(
        _mm_res_kernel,
        grid=(m // tm, n // tn),
        in_specs=[pl.BlockSpec((tm, k), lambda i, j: (i, 0)),
                  pl.BlockSpec((1, k, tn), lambda i, j: (layer, 0, j)),
                  pl.BlockSpec((tm, tn), lambda i, j: (i, j))],
        out_specs=pl.BlockSpec((tm, tn), lambda i, j: (i, j)),
        out_shape=jax.ShapeDtypeStruct((m, n), F32),
        compiler_params=_params(2),
        name="mm_res",
    )(a, w, res)


def _swiglu_kernel(x_ref, g_ref, wa_ref, wb_ref, o_ref, h_ref):
    @pl.when(pl.program_id(1) == 0)
    def _():
        h_ref[...] = _rms(x_ref[...], g_ref[...]).astype(BF16)

    h = h_ref[...]
    a = _dot(h, wa_ref[0])
    b = _dot(h, wb_ref[0])
    o_ref[...] = (a * _sigmoid(a) * b).astype(BF16)


def _swiglu(x, g, w, layer, tm, tn):
    nj = D_FF // tn
    return pl.pallas_call(
        _swiglu_kernel,
        grid=(M // tm, nj),
        in_specs=[pl.BlockSpec((tm, D_MODEL), lambda i, j: (i, 0)),
                  pl.BlockSpec((1, D_MODEL), lambda i, j: (0, 0)),
                  pl.BlockSpec((1, D_MODEL, tn), lambda i, j: (layer, 0, j)),
                  pl.BlockSpec((1, D_MODEL, tn), lambda i, j: (layer, 0, nj + j))],
        out_specs=pl.BlockSpec((tm, tn), lambda i, j: (i, j)),
        out_shape=jax.ShapeDtypeStruct((M, D_FF), BF16),
        scratch_shapes=[pltpu.VMEM((tm, D_MODEL), BF16)],
        compiler_params=_params(2),
        name="swiglu",
    )(x, g, w, w)


def _final_norm_kernel(x_ref, g_ref, o_ref):
    o_ref[...] = _rms(x_ref[...], g_ref[...])


def _final_norm(x, g, row0, rows, tm):
    return pl.pallas_call(
        _final_norm_kernel,
        grid=(rows // tm,),
        in_specs=[pl.BlockSpec((tm, D_MODEL), lambda i: (row0 // tm + i, 0)),
                  pl.BlockSpec((1, D_MODEL), lambda i: (0, 0))],
        out_specs=pl.BlockSpec((tm, D_MODEL), lambda i: (i, 0)),
        out_shape=jax.ShapeDtypeStruct((rows, D_MODEL), F32),
        compiler_params=_params(1),
        name="final_norm",
    )(x, g)


PREP_TM = 256


def _prep_kernel(q_ref, k_ref, v_ref, qi_ref, t_ref, inv_ref, qo, ko, kbo, vbo, qio, to, ki2o):
    i = pl.program_id(0)
    row = i * PREP_TM + lax.broadcasted_iota(I32, (PREP_TM, LANES), 0)
    lane = lax.broadcasted_iota(I32, (PREP_TM, LANES), 1)
    pos = jnp.where(row < MP, row & (SEQ - 1), PAST + (row & (DEC_SEQ - 1))).astype(F32)

    def rope(x, inv_row, half):
        ang = pos * inv_row
        c, s = jnp.cos(ang), jnp.sin(ang)
        upper = (lane & half) != 0
        partner = jnp.where(upper, pltpu.roll(x, half, 1), -pltpu.roll(x, LANES - half, 1))
        return x * c + partner * s

    inv_qk, inv_i, inv_t = inv_ref[0:1, :], inv_ref[1:2, :], inv_ref[2:3, :]
    for h in range(N_HEADS):
        sl = slice(h * LANES, (h + 1) * LANES)
        qo[:, sl] = rope(q_ref[:, sl], inv_qk, ROT_DIM // 2).astype(BF16)
        kr = rope(k_ref[:, sl], inv_qk, ROT_DIM // 2)
        ko[:, sl] = kr
        kbo[:, sl] = kr.astype(BF16)
        qio[:, sl] = rope(qi_ref[:, sl], inv_i, IDX_ROT // 2).astype(BF16)
    vbo[...] = v_ref[...].astype(BF16)
    tr = rope(t_ref[...], inv_t, IDX_ROT // 2)
    to[...] = tr
    ki = jnp.where(lane < IDX_DIM, tr, 0.0)
    ki2o[:, :LANES] = ki.astype(BF16)
    ki2o[:, LANES:] = pltpu.roll(ki, IDX_DIM, 1).astype(BF16)


def _prep(proj, inv):
    def cb(c):
        return pl.BlockSpec((PREP_TM, MIX_W), lambda i: (i, c))

    row512 = pl.BlockSpec((PREP_TM, MIX_W), lambda i: (i, 0))
    return pl.pallas_call(
        _prep_kernel,
        grid=(M // PREP_TM,),
        in_specs=[cb(CB_Q), cb(CB_K), cb(CB_V), cb(CB_QI),
                  pl.BlockSpec((PREP_TM, LANES), lambda i: (i, TAIL_COL // LANES)),
                  pl.BlockSpec((8, LANES), lambda i: (0, 0))],
        out_specs=[row512, row512, row512, row512, row512,
                   pl.BlockSpec((PREP_TM, LANES), lambda i: (i, 0)),
                   pl.BlockSpec((PREP_TM, 2 * LANES), lambda i: (i, 0))],
        out_shape=[jax.ShapeDtypeStruct((M, MIX_W), BF16),
                   jax.ShapeDtypeStruct((M, MIX_W), F32),
                   jax.ShapeDtypeStruct((M, MIX_W), BF16),
                   jax.ShapeDtypeStruct((M, MIX_W), BF16),
                   jax.ShapeDtypeStruct((M, MIX_W), BF16),
                   jax.ShapeDtypeStruct((M, LANES), F32),
                   jax.ShapeDtypeStruct((M, 2 * LANES), BF16)],
        compiler_params=_params(1),
        name="prep",
    )(proj, proj, proj, proj, proj, inv)


KEY_NEG_INF = INT_MIN + 0x7FFFFF


def _key_to_float(key):
    return lax.bitcast_convert_type(jnp.where(key < 0, key ^ 0x7FFFFFFF, key), F32)


def _select_chunks(sc_ref, bias_ref, nc, k, allowed_fn):
    _, rows, cw = sc_ref.shape
    kf = float(k)

    def count_ge(c):
        acc = jnp.where(sc_ref[0] >= c, 1.0, 0.0)
        for j in range(1, nc):
            acc = acc + jnp.where(sc_ref[j] >= c, 1.0, 0.0)
        return jnp.sum(acc, axis=-1, keepdims=True)

    t = jnp.where(count_ge(jnp.zeros((rows, 1), F32)) >= kf, 0, INT_MIN).astype(I32)
    for bit in range(30, -1, -1):
        cand = t + (1 << bit)
        ok = jnp.logical_or(count_ge(_key_to_float(cand)) >= kf, cand <= KEY_NEG_INF)
        t = jnp.where(ok, cand, t)
    lo = _key_to_float(t)
    hi = _key_to_float(t + 1)
    need = kf - count_ge(hi)
    r_io = lax.broadcasted_iota(I32, (LANES, LANES), 0)
    c_io = lax.broadcasted_iota(I32, (LANES, LANES), 1)
    tri = jnp.where(r_io <= c_io, 1.0, 0.0).astype(BF16)

    off = jnp.zeros((rows, 1), F32)
    for j in range(nc):
        for u in range(cw // LANES):
            sl = slice(u * LANES, (u + 1) * LANES)
            su = sc_ref[j, :, sl]
            above = su >= hi
            e = jnp.where(above, 0.0, jnp.where(su >= lo, 1.0, 0.0))
            rank = _dot(e.astype(BF16), tri) + off
            off = off + jnp.sum(e, axis=-1, keepdims=True)
            take = jnp.where(above, 1.0, jnp.where(rank <= need, e, 0.0))
            take = jnp.where(allowed_fn(j, u, su), take, 0.0)
            bias_ref[j, :, sl] = jnp.where(take > 0.5, 0.0, NEG_INF)


def _index_scores(qi, w_scaled, ki_a, ki_b):
    acc = None
    for p in range(IDX_HEADS // 2):
        qp = qi[:, p * LANES:(p + 1) * LANES]
        for half, ki in enumerate((ki_a, ki_b)):
            hh = 2 * p + half
            d = _dot_nt(qp, ki) * (IDX_DIM ** -0.5)
            term = jnp.maximum(d, 0.0) * w_scaled[:, IDX_DIM + hh:IDX_DIM + hh + 1]
            acc = term if acc is None else acc + term
    return acc


KEY_CHUNK = 256
N_KEY_CHUNKS = SEQ // KEY_CHUNK
CAUSAL_VARIANTS = 4


def _dsa_prompt_kernel(q_ref, qi_ref, t_ref, k_ref, v_ref, ki2_ref, o_ref, sc_ref, bias_ref):
    i = pl.program_id(1)
    row = i * QBLOCK + lax.broadcasted_iota(I32, (QBLOCK, KEY_CHUNK), 0)
    lane = lax.broadcasted_iota(I32, (QBLOCK, KEY_CHUNK), 1)
    row_g = i * QBLOCK + lax.broadcasted_iota(I32, (QBLOCK, LANES), 0)
    lane_g = lax.broadcasted_iota(I32, (QBLOCK, LANES), 1)

    def causal(j, u, su):
        return j * KEY_CHUNK + u * LANES + lane_g <= row_g

    def attend(n_keys):
        nc = n_keys // KEY_CHUNK
        w_scaled = t_ref[...] * (IDX_HEADS ** -0.5)
        sc = _index_scores(qi_ref[...], w_scaled, ki2_ref[:n_keys, :LANES], ki2_ref[:n_keys, LANES:])
        for j in range(nc):
            sc_ref[j] = jnp.where(j * KEY_CHUNK + lane <= row, sc[:, j * KEY_CHUNK:(j + 1) * KEY_CHUNK], NEG_INF)
        _select_chunks(sc_ref.at[:nc], bias_ref.at[:nc], nc, TOPK, causal)
        bias = jnp.concatenate([bias_ref[j] for j in range(nc)], axis=-1)
        for h in range(N_HEADS):
            sl = slice(h * HEAD_DIM, (h + 1) * HEAD_DIM)
            s = _dot_nt(q_ref[:, sl], k_ref[:n_keys, sl]) * (HEAD_DIM ** -0.5) + bias
            m = jnp.max(s, axis=-1, keepdims=True)
            p = jnp.exp(s - m)
            l = jnp.sum(p, axis=-1, keepdims=True)
            o_ref[:, sl] = (_dot(p.astype(BF16), v_ref[:n_keys, sl]) / l).astype(BF16)

    blocks_per_variant = (SEQ // QBLOCK) // CAUSAL_VARIANTS
    for v in range(CAUSAL_VARIANTS):
        @pl.when(i // blocks_per_variant == v)
        def _(v=v):
            attend((v + 1) * (SEQ // CAUSAL_VARIANTS))


def _dsa_prompt(q, qi, tail, kb, vb, ki2):
    nb = SEQ // QBLOCK

    def qspec(w):
        return pl.BlockSpec((QBLOCK, w), lambda b, i: (b * nb + i, 0))

    def kspec(w):
        return pl.BlockSpec((SEQ, w), lambda b, i: (b, 0))

    return pl.pallas_call(
        _dsa_prompt_kernel,
        grid=(BATCH, nb),
        in_specs=[qspec(MIX_W), qspec(MIX_W), qspec(LANES), kspec(MIX_W), kspec(MIX_W), kspec(2 * LANES)],
        out_specs=qspec(MIX_W),
        out_shape=jax.ShapeDtypeStruct((MP, MIX_W), BF16),
        scratch_shapes=[pltpu.VMEM((N_KEY_CHUNKS, QBLOCK, KEY_CHUNK), F32)] * 2,
        compiler_params=_params(2),
        name="dsa_prompt",
    )(q, qi, tail, kb, vb, ki2)


def _dsa_sample_scores_kernel(pt_ref, qi_ref, w_ref, *rest):
    page_refs, knew_ref, o_ref = rest[:N_PAGES], rest[N_PAGES], rest[N_PAGES + 1]
    qi = qi_ref[0]
    w = w_ref[0] * (IDX_HEADS ** -0.5)

    def chunk_scores(d):
        r = jnp.maximum(d * (IDX_DIM ** -0.5), 0.0) * w
        return jnp.sum(r.reshape(DEC_SEQ, IDX_HEADS, LANES), axis=1)

    for p in range(N_PAGES):
        o_ref[0, :, p * LANES:(p + 1) * LANES] = chunk_scores(_dot(qi, page_refs[p][0, 0].astype(BF16)))
    new = chunk_scores(_dot_nt(qi, knew_ref[0]))
    tq = lax.broadcasted_iota(I32, (DEC_SEQ, LANES), 0)
    jk = lax.broadcasted_iota(I32, (DEC_SEQ, LANES), 1)
    o_ref[0, :, PAST:] = jnp.where(jk <= tq, new, NEG_INF)


def _dsa_sample_scores(page_table, qi32, w32, cache_kidx_t, layer, knew_pad):
    page_specs = [pl.BlockSpec((1, 1, IDX_DIM, PAGE_SIZE),
                               functools.partial(lambda b, pt, p: (layer, pt[b, p], 0, 0), p=p))
                  for p in range(N_PAGES)]
    grid_spec = pltpu.PrefetchScalarGridSpec(
        num_scalar_prefetch=1,
        grid=(DEC_BATCH,),
        in_specs=[pl.BlockSpec((1, DEC_SEQ * IDX_HEADS, IDX_DIM), lambda b, pt: (b, 0, 0)),
                  pl.BlockSpec((1, DEC_SEQ * IDX_HEADS, 1), lambda b, pt: (b, 0, 0))] + page_specs
                 + [pl.BlockSpec((1, LANES, IDX_DIM), lambda b, pt: (b, 0, 0))],
        out_specs=pl.BlockSpec((1, DEC_SEQ, SAMPLE_KEYS), lambda b, pt: (b, 0, 0)),
    )
    return pl.pallas_call(
        _dsa_sample_scores_kernel,
        grid_spec=grid_spec,
        out_shape=jax.ShapeDtypeStruct((DEC_BATCH, DEC_SEQ, SAMPLE_KEYS), F32),
        compiler_params=_params(1),
        name="dsa_sample_scores",
    )(page_table, qi32, w32, *([cache_kidx_t] * N_PAGES), knew_pad)


SEL_ROWS = 128
N_SAMPLE_CHUNKS = SAMPLE_KEYS // LANES


def _sample_select_kernel(in_ref, o_ref, sc_ref, bias_ref):
    for c in range(N_SAMPLE_CHUNKS):
        sc_ref[c] = in_ref[:, c * LANES:(c + 1) * LANES]
    _select_chunks(sc_ref, bias_ref, N_SAMPLE_CHUNKS, TOPK, lambda j, u, su: su > NEG_INF)
    for c in range(N_SAMPLE_CHUNKS):
        o_ref[:, c * LANES:(c + 1) * LANES] = bias_ref[c]


def _sample_select(sc):
    return pl.pallas_call(
        _sample_select_kernel,
        grid=(MS // SEL_ROWS,),
        in_specs=[pl.BlockSpec((SEL_ROWS, SAMPLE_KEYS), lambda i: (i, 0))],
        out_specs=pl.BlockSpec((SEL_ROWS, SAMPLE_KEYS), lambda i: (i, 0)),
        out_shape=jax.ShapeDtypeStruct((MS, SAMPLE_KEYS), F32),
        scratch_shapes=[pltpu.VMEM((N_SAMPLE_CHUNKS, SEL_ROWS, LANES), F32)] * 2,
        compiler_params=_params(1),
        name="sample_select",
    )(sc)


def _dsa_sample_attn_kernel(pt_ref, q_ref, bias_ref, *rest):
    k_refs = rest[:N_PAGES]
    v_refs = rest[N_PAGES:2 * N_PAGES]
    knew_ref, vnew_ref, o_ref, s_ref = rest[2 * N_PAGES:]
    q = q_ref[0]
    def head_rows(h):
        return pl.ds(h, PAGE_SIZE, stride=N_HEADS)

    for h in range(N_HEADS):
        rs = slice(h * DEC_SEQ, (h + 1) * DEC_SEQ)
        hs = slice(h * HEAD_DIM, (h + 1) * HEAD_DIM)
        qh = q[:, hs]
        for p in range(N_PAGES):
            s_ref[rs, p * LANES:(p + 1) * LANES] = _dot_nt(qh, k_refs[p][0, 0, head_rows(h), :].astype(BF16))
        s_ref[rs, PAST:] = _dot_nt(qh, knew_ref[0, :, hs])
    bias = bias_ref[0]
    s = s_ref[...] * (HEAD_DIM ** -0.5) + jnp.concatenate([bias] * N_HEADS, axis=0)
    m = jnp.max(s, axis=-1, keepdims=True)
    pr = jnp.exp(s - m)
    l = jnp.sum(pr, axis=-1, keepdims=True)
    pb = pr.astype(BF16)
    for h in range(N_HEADS):
        rs = slice(h * DEC_SEQ, (h + 1) * DEC_SEQ)
        hs = slice(h * HEAD_DIM, (h + 1) * HEAD_DIM)
        acc = _dot(pb[rs, PAST:], vnew_ref[0, :, hs])
        for p in range(N_PAGES):
            acc = acc + _dot(pb[rs, p * LANES:(p + 1) * LANES], v_refs[p][0, 0, head_rows(h), :].astype(BF16))
        o_ref[0, :, hs] = (acc / l[rs]).astype(BF16)


def _dsa_sample_attn(page_table, q, bias, cache_k, cache_v, layer, knew_pad, vnew_pad):
    def page_spec(p):
        return pl.BlockSpec((1, 1, PAGE_SIZE * N_HEADS, HEAD_DIM),
                            functools.partial(lambda b, pt, p: (layer, pt[b, p], 0, 0), p=p))

    grid_spec = pltpu.PrefetchScalarGridSpec(
        num_scalar_prefetch=1,
        grid=(DEC_BATCH,),
        in_specs=[pl.BlockSpec((1, DEC_SEQ, MIX_W), lambda b, pt: (b, 0, 0)),
                  pl.BlockSpec((1, DEC_SEQ, SAMPLE_KEYS), lambda b, pt: (b, 0, 0))]
                 + [page_spec(p) for p in range(N_PAGES)] + [page_spec(p) for p in range(N_PAGES)]
                 + [pl.BlockSpec((1, LANES, MIX_W), lambda b, pt: (b, 0, 0)),
                    pl.BlockSpec((1, LANES, MIX_W), lambda b, pt: (b, 0, 0))],
        out_specs=pl.BlockSpec((1, DEC_SEQ, MIX_W), lambda b, pt: (b, 0, 0)),
        scratch_shapes=[pltpu.VMEM((N_HEADS * DEC_SEQ, SAMPLE_KEYS), F32)],
    )
    return pl.pallas_call(
        _dsa_sample_attn_kernel,
        grid_spec=grid_spec,
        out_shape=jax.ShapeDtypeStruct((DEC_BATCH, DEC_SEQ, MIX_W), BF16),
        compiler_params=_params(1),
        name="dsa_sample_attn",
    )(page_table, q, bias, *([cache_k] * N_PAGES), *([cache_v] * N_PAGES), knew_pad, vnew_pad)


CONV_TT = 512


def _conv_prompt_kernel(cx_ref, cb_ref, cc_ref, w_ref, yb_ref, buf_ref, carry_ref):
    j = pl.program_id(1)

    @pl.when(j == 0)
    def _():
        carry_ref[...] = jnp.zeros_like(carry_ref)

    z = cc_ref[...] * cx_ref[...]
    row = lax.broadcasted_iota(I32, z.shape, 0)
    c0 = carry_ref[0:1, :]
    c1 = carry_ref[1:2, :]
    zm1 = jnp.where(row == 0, c1, pltpu.roll(z, 1, 0))
    zm2 = jnp.where(row == 0, c0, jnp.where(row == 1, c1, pltpu.roll(z, 2, 0)))
    zc = w_ref[0:1, :] * zm2 + w_ref[1:2, :] * zm1 + w_ref[2:3, :] * z
    yb_ref[...] = (cb_ref[...] * zc).astype(BF16)
    last = z[CONV_TT - 2:CONV_TT, :]
    carry_ref[0:2, :] = last
    buf_ref[0] = last


def _conv_prompt(proj, conv_w):
    nt = SEQ // CONV_TT

    def cb(c):
        return pl.BlockSpec((CONV_TT, MIX_W), lambda b, j: (b * nt + j, c))

    return pl.pallas_call(
        _conv_prompt_kernel,
        grid=(BATCH, nt),
        in_specs=[cb(CB_CX), cb(CB_CB), cb(CB_CC), pl.BlockSpec((3, MIX_W), lambda b, j: (0, 0))],
        out_specs=[pl.BlockSpec((CONV_TT, MIX_W), lambda b, j: (b * nt + j, 0)),
                   pl.BlockSpec((1, 2, MIX_W), lambda b, j: (b, 0, 0))],
        out_shape=[jax.ShapeDtypeStruct((MP, MIX_W), BF16),
                   jax.ShapeDtypeStruct((BATCH, 2, MIX_W), F32)],
        scratch_shapes=[pltpu.VMEM((8, MIX_W), F32)],
        compiler_params=_params(2),
        name="conv_prompt",
    )(proj, proj, proj, conv_w)


def _gmlp_prompt_kernel(gu_ref, gv_ref, w_ref, bt_ref, o_ref):
    r_io = lax.broadcasted_iota(I32, (GMLP_CHUNK, GMLP_CHUNK), 0)
    c_io = lax.broadcasted_iota(I32, (GMLP_CHUNK, GMLP_CHUNK), 1)
    tril = c_io <= r_io
    for g in range(GMLP_GROUPS):
        sl = slice(g * LANES, (g + 1) * LANES)
        wm = jnp.where(tril, w_ref[g], 0.0).astype(BF16)
        z = _dot(wm, gv_ref[:, sl].astype(BF16)) + bt_ref[:, g:g + 1]
        o_ref[:, sl] = (gu_ref[:, sl] * z).astype(BF16)


def _gmlp_prompt(proj, gmlp_w, gmlp_bt):
    def cb(c):
        return pl.BlockSpec((GMLP_CHUNK, MIX_W), lambda i: (i, c))

    return pl.pallas_call(
        _gmlp_prompt_kernel,
        grid=(MP // GMLP_CHUNK,),
        in_specs=[cb(CB_GU), cb(CB_GV),
                  pl.BlockSpec((GMLP_GROUPS, GMLP_CHUNK, GMLP_CHUNK), lambda i: (0, 0, 0)),
                  pl.BlockSpec((GMLP_CHUNK, GMLP_GROUPS), lambda i: (0, 0))],
        out_specs=pl.BlockSpec((GMLP_CHUNK, MIX_W), lambda i: (i, 0)),
        out_shape=jax.ShapeDtypeStruct((MP, MIX_W), BF16),
        compiler_params=_params(1),
        name="gmlp_prompt",
    )(proj, proj, gmlp_w, gmlp_bt)


def _mix_sample_kernel(cx_ref, cb_ref, cc_ref, gu_ref, gv_ref, buf_ref, cw_ref, gw_ref, gb_ref,
                       yb_ref, nbuf_ref, yd_ref):
    zf = [buf_ref[:, 0, :], buf_ref[:, 1, :]]
    for t in range(DEC_SEQ):
        zf.append(cc_ref[:, t, :] * cx_ref[:, t, :])
    for t in range(DEC_SEQ):
        zc = cw_ref[0:1, :] * zf[t] + cw_ref[1:2, :] * zf[t + 1] + cw_ref[2:3, :] * zf[t + 2]
        yb_ref[:, t, :] = (cb_ref[:, t, :] * zc).astype(BF16)
    nbuf_ref[:, 0, :] = zf[DEC_SEQ]
    nbuf_ref[:, 1, :] = zf[DEC_SEQ + 1]
    v = [gv_ref[:, s, :] for s in range(DEC_SEQ)]
    for t in range(DEC_SEQ):
        z = gb_ref[t:t + 1, :]
        for s in range(t + 1):
            z = z + gw_ref[t * DEC_SEQ + s:t * DEC_SEQ + s + 1, :] * v[s]
        yd_ref[:, t, :] = (gu_ref[:, t, :] * z).astype(BF16)


def _mix_sample(proj_s, state_conv_l, conv_w, gw_lane, gb_lane):
    def cb(c):
        return pl.BlockSpec((DEC_BATCH, DEC_SEQ, MIX_W), lambda i: (0, 0, c))

    full3 = pl.BlockSpec((DEC_BATCH, DEC_SEQ, MIX_W), lambda i: (0, 0, 0))
    buf3 = pl.BlockSpec((DEC_BATCH, 2, MIX_W), lambda i: (0, 0, 0))
    return pl.pallas_call(
        _mix_sample_kernel,
        grid=(1,),
        in_specs=[cb(CB_CX), cb(CB_CB), cb(CB_CC), cb(CB_GU), cb(CB_GV), buf3,
                  pl.BlockSpec((3, MIX_W), lambda i: (0, 0)),
                  pl.BlockSpec((DEC_SEQ * DEC_SEQ, MIX_W), lambda i: (0, 0)),
                  pl.BlockSpec((DEC_SEQ, MIX_W), lambda i: (0, 0))],
        out_specs=[full3, buf3, full3],
        out_shape=[jax.ShapeDtypeStruct((DEC_BATCH, DEC_SEQ, MIX_W), BF16),
                   jax.ShapeDtypeStruct((DEC_BATCH, 2, MIX_W), F32),
                   jax.ShapeDtypeStruct((DEC_BATCH, DEC_SEQ, MIX_W), BF16)],
        compiler_params=_params(1),
        name="mix_sample",
    )(proj_s, proj_s, proj_s, proj_s, proj_s, state_conv_l, conv_w, gw_lane, gb_lane)


def _ssm_params_kernel(ar_ref, ai_ref, ldt_ref, br_ref, bi_ref, ab_ref, bbr_ref, bbi_ref):
    ar, ai = ar_ref[...], ai_ref[...]
    dt = jnp.exp(ldt_ref[...])
    mag = jnp.exp(dt * ar)
    abar_re, abar_im = mag * jnp.cos(dt * ai), mag * jnp.sin(dt * ai)
    den = ar * ar + ai * ai
    nr, ni = abar_re - 1.0, abar_im
    coef_re = (nr * ar + ni * ai) / den
    coef_im = (ni * ar - nr * ai) / den
    ab_ref[...] = jnp.zeros_like(ab_ref)
    ab_ref[0:1, :] = abar_re
    ab_ref[1:2, :] = abar_im
    br, bi = br_ref[...], bi_ref[...]
    bbr_ref[...] = (coef_re * br - coef_im * bi).astype(BF16)
    bbi_ref[...] = (coef_re * bi + coef_im * br).astype(BF16)


def _ssm_params(a_re, a_im, ldt, bd_re, bd_im):
    row = pl.BlockSpec((1, SSM_STATE), lambda i: (0, 0))
    mat = pl.BlockSpec((MIX_W, SSM_STATE), lambda i: (0, 0))
    return pl.pallas_call(
        _ssm_params_kernel,
        grid=(1,),
        in_specs=[row, row, row, mat, mat],
        out_specs=[pl.BlockSpec((8, SSM_STATE), lambda i: (0, 0)), mat, mat],
        out_shape=[jax.ShapeDtypeStruct((8, SSM_STATE), F32),
                   jax.ShapeDtypeStruct((MIX_W, SSM_STATE), BF16),
                   jax.ShapeDtypeStruct((MIX_W, SSM_STATE), BF16)],
        compiler_params=_params(1),
        name="ssm_params",
    )(a_re, a_im, ldt, bd_re, bd_im)


def _gelu_tanh(x):
    return 0.5 * x * (1.0 + jnp.tanh(0.7978845608028654 * (x + 0.044715 * (x * x * x))))


SSM_BLOCK_STATES = SSM_STATE // (MIX_W // LANES)


def _ssm_drive(ub, bb_ref):
    return jnp.concatenate(
        [_dot(ub[:, c * LANES:(c + 1) * LANES],
              bb_ref[c * LANES:(c + 1) * LANES, c * SSM_BLOCK_STATES:(c + 1) * SSM_BLOCK_STATES])
         for c in range(MIX_W // LANES)], axis=-1)


def _ssm_readout(u, hr, hi, cr_ref, ci_ref, d_ref, wg_ref, bg_ref):
    hrb, hib = hr.astype(BF16), hi.astype(BF16)
    parts = []
    for c in range(MIX_W // LANES):
        ss = slice(c * SSM_BLOCK_STATES, (c + 1) * SSM_BLOCK_STATES)
        cs = slice(c * LANES, (c + 1) * LANES)
        parts.append(_dot(hrb[:, ss], cr_ref[ss, cs]) - _dot(hib[:, ss], ci_ref[ss, cs]))
    y = jnp.concatenate(parts, axis=-1) + d_ref[...] * u
    g = _gelu_tanh(y)
    return g * _sigmoid(_dot(g.astype(BF16), wg_ref[...]) + bg_ref[...])


SSM_TC = 128
SSM_SLABS = MIX_W // LANES


def _ssm_prompt_kernel(u0, u1, u2, u3, ab_ref, bbr_ref, bbi_ref, cr_ref, ci_ref, d_ref, wg_ref, bg_ref,
                       y_ref, str_ref, sti_ref, hr_ref, hi_ref, carry_ref, il_ref):
    j = pl.program_id(0)

    @pl.when(j == 0)
    def _():
        carry_ref[...] = jnp.zeros_like(carry_ref)

    def batch_rows(b):
        return pl.ds(b, SSM_TC, stride=BATCH)

    for b, u_ref in enumerate((u0, u1, u2, u3)):
        for c in range(SSM_SLABS):
            il_ref[c, batch_rows(b), :] = u_ref[:, c * LANES:(c + 1) * LANES]
    u = jnp.concatenate([il_ref[c] for c in range(SSM_SLABS)], axis=-1)
    ub = u.astype(BF16)
    hr_ref[...] = _ssm_drive(ub, bbr_ref)
    hi_ref[...] = _ssm_drive(ub, bbi_ref)
    ar, ai = ab_ref[0:1, :], ab_ref[1:2, :]

    def step2(t2, carry):
        pr, pi = carry
        rows = pl.ds(pl.multiple_of(t2 * 2 * BATCH, 2 * BATCH), 2 * BATCH)
        br, bi = hr_ref[rows, :], hi_ref[rows, :]
        er = ar * pr - ai * pi + br[:BATCH]
        ei = ar * pi + ai * pr + bi[:BATCH]
        nr = ar * er - ai * ei + br[BATCH:]
        ni = ar * ei + ai * er + bi[BATCH:]
        hr_ref[rows, :] = jnp.concatenate([er, nr], axis=0)
        hi_ref[rows, :] = jnp.concatenate([ei, ni], axis=0)
        return nr, ni

    fr, fi = lax.fori_loop(0, SSM_TC // 2, step2, (carry_ref[0:BATCH, :], carry_ref[BATCH:2 * BATCH, :]), unroll=4)
    carry_ref[0:BATCH, :] = fr
    carry_ref[BATCH:2 * BATCH, :] = fi
    str_ref[...] = fr
    sti_ref[...] = fi
    y = _ssm_readout(u, hr_ref[...], hi_ref[...], cr_ref, ci_ref, d_ref, wg_ref, bg_ref)
    for c in range(SSM_SLABS):
        il_ref[c] = y[:, c * LANES:(c + 1) * LANES]
    for b in range(BATCH):
        for c in range(SSM_SLABS):
            y_ref[b, :, c * LANES:(c + 1) * LANES] = il_ref[c, batch_rows(b), :].astype(BF16)


def _ssm_prompt(proj, ab, bbr, bbi, cdr, cdi, d, wglu, bglu):
    nt = SEQ // SSM_TC

    def const(shape):
        return pl.BlockSpec(shape, lambda j: (0,) * len(shape))

    u_specs = [pl.BlockSpec((SSM_TC, MIX_W), functools.partial(lambda j, b: (b * nt + j, CB_SU), b=b))
               for b in range(BATCH)]
    state = pl.BlockSpec((BATCH, SSM_STATE), lambda j: (0, 0))
    return pl.pallas_call(
        _ssm_prompt_kernel,
        grid=(nt,),
        in_specs=u_specs + [const((8, SSM_STATE)), const((MIX_W, SSM_STATE)), const((MIX_W, SSM_STATE)),
                            const((SSM_STATE, MIX_W)), const((SSM_STATE, MIX_W)), const((1, MIX_W)),
                            const((MIX_W, MIX_W)), const((1, MIX_W))],
        out_specs=[pl.BlockSpec((BATCH, SSM_TC, MIX_W), lambda j: (0, j, 0)), state, state],
        out_shape=[jax.ShapeDtypeStruct((BATCH, SEQ, MIX_W), BF16),
                   jax.ShapeDtypeStruct((BATCH, SSM_STATE), F32),
                   jax.ShapeDtypeStruct((BATCH, SSM_STATE), F32)],
        scratch_shapes=[pltpu.VMEM((BATCH * SSM_TC, SSM_STATE), F32), pltpu.VMEM((BATCH * SSM_TC, SSM_STATE), F32),
                        pltpu.VMEM((2 * BATCH, SSM_STATE), F32),
                        pltpu.VMEM((SSM_SLABS, BATCH * SSM_TC, LANES), F32)],
        compiler_params=_params(1),
        name="ssm_prompt",
    )(proj, proj, proj, proj, ab, bbr, bbi, cdr, cdi, d, wglu, bglu)


def _ssm_sample_kernel(u_ref, h0r_ref, h0i_ref, ab_ref, bbr_ref, bbi_ref, cr_ref, ci_ref, d_ref, wg_ref, bg_ref,
                       y_ref, nr_ref, ni_ref):
    ar, ai = ab_ref[0:1, :], ab_ref[1:2, :]
    hr, hi = h0r_ref[...], h0i_ref[...]
    for t in range(DEC_SEQ):
        u = u_ref[:, t, :]
        ub = u.astype(BF16)
        hr, hi = (ar * hr - ai * hi + _ssm_drive(ub, bbr_ref),
                  ar * hi + ai * hr + _ssm_drive(ub, bbi_ref))
        y_ref[:, t, :] = _ssm_readout(u, hr, hi, cr_ref, ci_ref, d_ref, wg_ref, bg_ref).astype(BF16)
    nr_ref[...] = hr
    ni_ref[...] = hi


def _ssm_sample(proj_s, h0r, h0i, ab, bbr, bbi, cdr, cdi, d, wglu, bglu):
    def const(shape):
        return pl.BlockSpec(shape, lambda i: (0,) * len(shape))

    st = const((DEC_BATCH, SSM_STATE))
    return pl.pallas_call(
        _ssm_sample_kernel,
        grid=(1,),
        in_specs=[pl.BlockSpec((DEC_BATCH, DEC_SEQ, MIX_W), lambda i: (0, 0, CB_SU)), st, st,
                  const((8, SSM_STATE)), const((MIX_W, SSM_STATE)), const((MIX_W, SSM_STATE)),
                  const((SSM_STATE, MIX_W)), const((SSM_STATE, MIX_W)), const((1, MIX_W)),
                  const((MIX_W, MIX_W)), const((1, MIX_W))],
        out_specs=[const((DEC_BATCH, DEC_SEQ, MIX_W)), st, st],
        out_shape=[jax.ShapeDtypeStruct((DEC_BATCH, DEC_SEQ, MIX_W), BF16),
                   jax.ShapeDtypeStruct((DEC_BATCH, SSM_STATE), F32),
                   jax.ShapeDtypeStruct((DEC_BATCH, SSM_STATE), F32)],
        compiler_params=_params(1),
        name="ssm_sample",
    )(proj_s, h0r, h0i, ab, bbr, bbi, cdr, cdi, d, wglu, bglu)


def _rope_inv_rows():
    lane = jnp.arange(LANES)

    def inv(half):
        return ROPE_THETA ** (-jnp.arange(half, dtype=F32) / half)

    inv_qk = jnp.where(lane < ROT_DIM, inv(ROT_DIM // 2)[lane % (ROT_DIM // 2)], 0.0)
    idx = inv(IDX_ROT // 2)[lane % (IDX_ROT // 2)]
    inv_i = jnp.where((lane % IDX_DIM) < IDX_ROT, idx, 0.0)
    inv_t = jnp.where(lane < IDX_ROT, idx, 0.0)
    rows = jnp.stack([inv_qk, inv_i, inv_t]).astype(F32)
    return jnp.concatenate([rows, jnp.zeros((5, LANES), F32)], axis=0)


def _pad_rows(a, rows):
    return jnp.pad(a, ((0, 0), (0, rows - a.shape[1]), (0, 0)))


def kernel(x_prompt, x_sample, cache_k, cache_v, cache_kidx, state_conv, state_ssm_re, state_ssm_im, page_table,
           norm_mix, w_in, conv_w, ssm_a_re, ssm_a_im, ssm_log_dt, ssm_b_re, ssm_b_im, ssm_c_re, ssm_c_im, ssm_d,
           w_glu, b_glu, gmlp_w, gmlp_b, w_branch, w_out, norm_ffn, w_ffn_in, w_ffn_out, norm_final):
    x = jnp.concatenate([x_prompt.reshape(MP, D_MODEL), x_sample.reshape(MS, D_MODEL)], axis=0)
    inv = _rope_inv_rows()
    eye_g = jnp.eye(SSM_GROUPS, dtype=F32)
    cache_kidx_t = jnp.swapaxes(cache_kidx, 2, 3)
    n_pool = cache_k.shape[1]
    cache_k2 = cache_k.reshape(DEPTH, n_pool, PAGE_SIZE * N_HEADS, HEAD_DIM)
    cache_v2 = cache_v.reshape(DEPTH, n_pool, PAGE_SIZE * N_HEADS, HEAD_DIM)
    n_gate0 = w_in.shape[2] - N_BRANCH * D_MODEL
    c_ki = 3 * MIX_W + IDX_HEADS * IDX_DIM
    c_cx = c_ki + IDX_DIM + IDX_HEADS
    w_branch_b, w_out_b = w_branch.astype(BF16), w_out.astype(BF16)
    w_ffn_in_b, w_ffn_out_b = w_ffn_in.astype(BF16), w_ffn_out.astype(BF16)

    st_p, st_s = [], []
    for l in range(DEPTH):
        wl = w_in[l]
        w_main = jnp.concatenate(
            [wl[:, :c_ki], wl[:, c_cx:n_gate0], wl[:, c_ki:c_cx],
             jnp.zeros((D_MODEL, N_MAIN - n_gate0), F32)], axis=1).astype(BF16)
        w_gate = wl[:, n_gate0:].astype(BF16)

        proj = _norm_mm(x, norm_mix[l][None, :], w_main, 1088, 768)
        q_r, k_r, k_b, v_b, qi_r, tail_r, ki2 = _prep(proj, inv)

        ya_p = _dsa_prompt(q_r, qi_r, tail_r, k_b, v_b, ki2)
        qi_s = qi_r[MP:].reshape(DEC_BATCH, DEC_SEQ * IDX_HEADS, IDX_DIM)
        tail_s = tail_r[MP:].reshape(DEC_BATCH, DEC_SEQ, LANES)
        w_s = tail_s[:, :, IDX_DIM:IDX_DIM + IDX_HEADS].reshape(DEC_BATCH, DEC_SEQ * IDX_HEADS, 1)
        kinew = _pad_rows(tail_s[:, :, :IDX_DIM].astype(BF16), LANES)
        sc_s = _dsa_sample_scores(page_table, qi_s, w_s, cache_kidx_t, l, kinew)
        bias_s = _sample_select(sc_s.reshape(MS, SAMPLE_KEYS)).reshape(DEC_BATCH, DEC_SEQ, SAMPLE_KEYS)
        q_s = q_r[MP:].reshape(DEC_BATCH, DEC_SEQ, MIX_W)
        knew = _pad_rows(k_b[MP:].reshape(DEC_BATCH, DEC_SEQ, MIX_W), LANES)
        vnew = _pad_rows(v_b[MP:].reshape(DEC_BATCH, DEC_SEQ, MIX_W), LANES)
        ya_s = _dsa_sample_attn(page_table, q_s, bias_s, cache_k2, cache_v2, l, knew, vnew)
        ya = jnp.concatenate([ya_p, ya_s.reshape(MS, MIX_W)], axis=0)

        proj_s = proj[MP:].reshape(DEC_BATCH, DEC_SEQ, N_MAIN)
        yb_p, buf_p = _conv_prompt(proj, conv_w[l])
        yd_p = _gmlp_prompt(proj, gmlp_w[l], gmlp_b[l].T)
        gw_lane = jnp.repeat(gmlp_w[l][:, :DEC_SEQ, :DEC_SEQ].transpose(1, 2, 0).reshape(DEC_SEQ * DEC_SEQ, GMLP_GROUPS),
                             LANES, axis=1)
        gb_lane = jnp.repeat(gmlp_b[l][:, :DEC_SEQ].T, LANES, axis=1)
        yb_s, buf_s, yd_s = _mix_sample(proj_s, state_conv[l], conv_w[l], gw_lane, gb_lane)
        yb = jnp.concatenate([yb_p, yb_s.reshape(MS, MIX_W)], axis=0)
        yd = jnp.concatenate([yd_p, yd_s.reshape(MS, MIX_W)], axis=0)

        bd_re = jnp.einsum('gnp,gh->gphn', ssm_b_re[l], eye_g).reshape(MIX_W, SSM_STATE)
        bd_im = jnp.einsum('gnp,gh->gphn', ssm_b_im[l], eye_g).reshape(MIX_W, SSM_STATE)
        cd_re = jnp.einsum('gpn,gh->gnhp', ssm_c_re[l], eye_g).reshape(SSM_STATE, MIX_W).astype(BF16)
        cd_im = jnp.einsum('gpn,gh->gnhp', ssm_c_im[l], eye_g).reshape(SSM_STATE, MIX_W).astype(BF16)
        ldt = jnp.repeat(ssm_log_dt[l], SSM_N)[None, :]
        ab, bbr, bbi = _ssm_params(ssm_a_re[l].reshape(1, SSM_STATE), ssm_a_im[l].reshape(1, SSM_STATE), ldt,
                                   bd_re, bd_im)
        ssm_consts = (ab, bbr, bbi, cd_re, cd_im, ssm_d[l][None, :], w_glu[l].astype(BF16), b_glu[l][None, :])
        yc_p, hre_p, him_p = _ssm_prompt(proj, *ssm_consts)
        yc_p = yc_p.reshape(MP, MIX_W)
        yc_s, nre_s, nim_s = _ssm_sample(proj_s, state_ssm_re[l].reshape(DEC_BATCH, SSM_STATE),
                                         state_ssm_im[l].reshape(DEC_BATCH, SSM_STATE), *ssm_consts)
        yc = jnp.concatenate([yc_p, yc_s.reshape(MS, MIX_W)], axis=0)

        merged = _gate_merge(x, norm_mix[l][None, :], w_gate, (ya, yb, yc, yd), w_branch_b, l, 544, 512)
        x = _mm_res(merged, w_out_b, l, x, 1088, 512)
        act = _swiglu(x, norm_ffn[l][None, :], w_ffn_in_b, l, 1088, 512)
        x = _mm_res(act, w_ffn_out_b, l, x, 544, 512)

        v_f = proj[:, CB_V * MIX_W:(CB_V + 1) * MIX_W]
        gv_s = proj[MP:, CB_GV * MIX_W:(CB_GV + 1) * MIX_W]
        st_p.append((k_r[:MP].reshape(BATCH, SEQ, N_HEADS, HEAD_DIM), v_f[:MP].reshape(BATCH, SEQ, N_HEADS, HEAD_DIM),
                     tail_r[:MP, :IDX_DIM].reshape(BATCH, SEQ, IDX_DIM), buf_p,
                     hre_p.reshape(BATCH, SSM_GROUPS, SSM_N), him_p.reshape(BATCH, SSM_GROUPS, SSM_N)))
        st_s.append((k_r[MP:].reshape(DEC_BATCH, DEC_SEQ, N_HEADS, HEAD_DIM),
                     v_f[MP:].reshape(DEC_BATCH, DEC_SEQ, N_HEADS, HEAD_DIM),
                     tail_r[MP:, :IDX_DIM].reshape(DEC_BATCH, DEC_SEQ, IDX_DIM), buf_s,
                     nre_s.reshape(DEC_BATCH, SSM_GROUPS, SSM_N), nim_s.reshape(DEC_BATCH, SSM_GROUPS, SSM_N),
                     gv_s.reshape(DEC_BATCH, DEC_SEQ, MIX_W)))

    y_p = _final_norm(x, norm_final[None, :], 0, MP, MS)
    y_s = _final_norm(x, norm_final[None, :], MP, MS, MS)
    outs = [y_p.reshape(BATCH, SEQ, D_MODEL), y_s.reshape(DEC_BATCH, DEC_SEQ, D_MODEL)]
    outs += [jnp.stack([s[i] for s in st_p]) for i in range(6)]
    outs += [jnp.stack([s[i] for s in st_s]) for i in range(7)]
    return tuple(outs)
```

```python
import functools

import jax
import jax.numpy as jnp
from jax import lax
from jax.experimental import pallas as pl
from jax.experimental.pallas import tpu as pltpu

F32 = jnp.float32
BF16 = jnp.bfloat16
I32 = jnp.int32

D_MODEL = 2048
BATCH = 4
SEQ = 2048
DEPTH = 2
DEC_BATCH = 128
DEC_SEQ = 4
PAGE_SIZE = 128
N_PAGES = 16
PAST = N_PAGES * PAGE_SIZE
MIX_W = 512
N_HEADS = 4
HEAD_DIM = 128
ROT_DIM = 32
IDX_HEADS = 8
IDX_DIM = 64
IDX_ROT = 16
TOPK = 256
QBLOCK = 128
ROPE_THETA = 500000.0
SSM_P = 16
SSM_GROUPS = 32
SSM_N = 64
SSM_STATE = SSM_GROUPS * SSM_N
GMLP_CHUNK = 128
GMLP_GROUPS = 4
D_FF = 5632
N_BRANCH = 4

MP = BATCH * SEQ
MS = DEC_BATCH * DEC_SEQ
M = MP + MS
LANES = 128
TAIL_COL = 10 * MIX_W
N_MAIN = TAIL_COL + 2 * LANES
CB_Q, CB_K, CB_V, CB_QI, CB_CX, CB_CB, CB_CC, CB_SU, CB_GU, CB_GV = range(10)
SAMPLE_KEYS = PAST + LANES
NEG_INF = float("-inf")
INT_MIN = -2 ** 31
VMEM_LIMIT = 56 * 1024 * 1024


def _params(n_axes, vmem=VMEM_LIMIT):
    return pltpu.CompilerParams(dimension_semantics=("arbitrary",) * n_axes, vmem_limit_bytes=vmem)


def _rms(x, g):
    return x * lax.rsqrt(jnp.mean(x * x, axis=-1, keepdims=True) + 1e-6) * g


def _dot(a, b):
    return jnp.dot(a, b, preferred_element_type=F32)


def _dot_nt(a, b):
    return lax.dot_general(a, b, (((1,), (1,)), ((), ())), preferred_element_type=F32)


def _sigmoid(x):
    return 1.0 / (1.0 + jnp.exp(-x))


def _norm_mm_kernel(x_ref, g_ref, w_ref, o_ref, h_ref):
    @pl.when(pl.program_id(1) == 0)
    def _():
        h_ref[...] = _rms(x_ref[...], g_ref[...]).astype(BF16)

    o_ref[...] = _dot(h_ref[...], w_ref[...])


def _norm_mm(x, g, w, tm, tn):
    m, k = x.shape
    n = w.shape[1]
    return pl.pallas_call(
        _norm_mm_kernel,
        grid=(m // tm, n // tn),
        in_specs=[pl.BlockSpec((tm, k), lambda i, j: (i, 0)),
                  pl.BlockSpec((1, k), lambda i, j: (0, 0)),
                  pl.BlockSpec((k, tn), lambda i, j: (0, j))],
        out_specs=pl.BlockSpec((tm, tn), lambda i, j: (i, j)),
        out_shape=jax.ShapeDtypeStruct((m, n), F32),
        scratch_shapes=[pltpu.VMEM((tm, k), BF16)],
        compiler_params=_params(2),
        name="norm_mm",
    )(x, g, w)


def _gate_merge_kernel(x_ref, g_ref, wg0, wg1, wg2, wg3, yp0, yp1, yp2, yp3, ys0, ys1, ys2, ys3, wb_ref,
                       o_ref, h_ref):
    @pl.when(pl.program_id(1) == 0)
    def _():
        h_ref[...] = _rms(x_ref[...], g_ref[...]).astype(BF16)

    h = h_ref[...]
    is_prompt = pl.program_id(0) < MP // MS
    acc = None
    for kk, (wg, yp, ys) in enumerate(((wg0, yp0, ys0), (wg1, yp1, ys1), (wg2, yp2, ys2), (wg3, yp3, ys3))):
        gate = _sigmoid(_dot(h, wg[...]))
        y = jnp.where(is_prompt, yp[...], ys[...])
        term = gate * _dot(y, wb_ref[0, kk])
        acc = term if acc is None else acc + term
    o_ref[...] = acc.astype(BF16)


def _gate_merge(x, g, wg, ys_prompt, ys_sample, wb, layer, tn):
    tm = MS
    nj = D_MODEL // tn
    wg_specs = [pl.BlockSpec((D_MODEL, tn), functools.partial(lambda i, j, kk: (0, kk * nj + j), kk=kk))
                for kk in range(N_BRANCH)]
    yp_specs = [pl.BlockSpec((tm, MIX_W), lambda i, j: (jnp.minimum(i, MP // MS - 1), 0)) for _ in range(N_BRANCH)]
    ys_specs = [pl.BlockSpec((tm, MIX_W), lambda i, j: (0, 0)) for _ in range(N_BRANCH)]
    return pl.pallas_call(
        _gate_merge_kernel,
        grid=(M // tm, nj),
        in_specs=[pl.BlockSpec((tm, D_MODEL), lambda i, j: (i, 0)),
                  pl.BlockSpec((1, D_MODEL), lambda i, j: (0, 0))] + wg_specs + yp_specs + ys_specs
                 + [pl.BlockSpec((1, N_BRANCH, MIX_W, tn), lambda i, j: (layer, 0, 0, j))],
        out_specs=pl.BlockSpec((tm, tn), lambda i, j: (i, j)),
        out_shape=jax.ShapeDtypeStruct((M, D_MODEL), BF16),
        scratch_shapes=[pltpu.VMEM((tm, D_MODEL), BF16)],
        compiler_params=_params(2),
        name="gate_merge",
    )(x, g, wg, wg, wg, wg, *ys_prompt, *ys_sample, wb)


def _mm_res_kernel(a_ref, w_ref, r_ref, o_ref):
    o_ref[...] = r_ref[...] + _dot(a_ref[...], w_ref[0])


def _mm_res(a, w, layer, res, tm, tn):
    m, k = a.shape
    n = w.shape[2]
    return pl.pallas_call(
        _mm_res_kernel,
        grid=(m // tm, n // tn),
        in_specs=[pl.BlockSpec((tm, k), lambda i, j: (i, 0)),
                  pl.BlockSpec((1, k, tn), lambda i, j: (layer, 0, j)),
                  pl.BlockSpec((tm, tn), lambda i, j: (i, j))],
        out_specs=pl.BlockSpec((tm, tn), lambda i, j: (i, j)),
        out_shape=jax.ShapeDtypeStruct((m, n), F32),
        compiler_params=_params(2),
        name="mm_res",
    )(a, w, res)


def _swiglu_kernel(x_ref, g_ref, wa_ref, wb_ref, o_ref, h_ref):
    @pl.when(pl.program_id(1) == 0)
    def _():
        h_ref[...] = _rms(x_ref[...], g_ref[...]).astype(BF16)

    h = h_ref[...]
    a = _dot(h, wa_ref[0])
    b = _dot(h, wb_ref[0])
    o_ref[...] = (a * _sigmoid(a) * b).astype(BF16)


def _swiglu(x, g, w, layer, tm, tn):
    nj = D_FF // tn
    return pl.pallas_call(
        _swiglu_kernel,
        grid=(M // tm, nj),
        in_specs=[pl.BlockSpec((tm, D_MODEL), lambda i, j: (i, 0)),
                  pl.BlockSpec((1, D_MODEL), lambda i, j: (0, 0)),
                  pl.BlockSpec((1, D_MODEL, tn), lambda i, j: (layer, 0, j)),
                  pl.BlockSpec((1, D_MODEL, tn), lambda i, j: (layer, 0, nj + j))],
        out_specs=pl.BlockSpec((tm, tn), lambda i, j: (i, j)),
        out_shape=jax.ShapeDtypeStruct((M, D_FF), BF16),
        scratch_shapes=[pltpu.VMEM((tm, D_MODEL), BF16)],
        compiler_params=_params(2),
        name="swiglu",
    )(x, g, w, w)


def _final_norm_kernel(x_ref, g_ref, o_ref):
    o_ref[...] = _rms(x_ref[...], g_ref[...])


def _final_norm(x, g, row0, rows, tm):
    return pl.pallas_call(
        _final_norm_kernel,
        grid=(rows // tm,),
        in_specs=[pl.BlockSpec((tm, D_MODEL), lambda i: (row0 // tm + i, 0)),
                  pl.BlockSpec((1, D_MODEL), lambda i: (0, 0))],
        out_specs=pl.BlockSpec((tm, D_MODEL), lambda i: (i, 0)),
        out_shape=jax.ShapeDtypeStruct((rows, D_MODEL), F32),
        compiler_params=_params(1),
        name="final_norm",
    )(x, g)


PREP_TM = 256


def _prep_kernel(q_ref, k_ref, v_ref, qi_ref, t_ref, tab_ref, qo, ko, kbo, vo, vbo, qio, to, ki2o):
    lane = lax.broadcasted_iota(I32, (PREP_TM, LANES), 1)

    def rope(x, table, half):
        c, s = tab_ref[2 * table], tab_ref[2 * table + 1]
        upper = (lane & half) != 0
        partner = jnp.where(upper, pltpu.roll(x, half, 1), -pltpu.roll(x, LANES - half, 1))
        return x * c + partner * s

    for h in range(N_HEADS):
        sl = slice(h * LANES, (h + 1) * LANES)
        head_rows = pl.ds(h, PREP_TM, stride=N_HEADS)
        qo[:, sl] = rope(q_ref[:, sl], 0, ROT_DIM // 2).astype(BF16)
        kr = rope(k_ref[:, sl], 0, ROT_DIM // 2)
        ko[head_rows, :] = kr
        kbo[:, sl] = kr.astype(BF16)
        v = v_ref[:, sl]
        vo[head_rows, :] = v
        vbo[:, sl] = v.astype(BF16)
        qio[:, sl] = rope(qi_ref[:, sl], 1, IDX_ROT // 2).astype(BF16)
    tr = rope(t_ref[...], 2, IDX_ROT // 2)
    to[...] = tr
    ki = jnp.where(lane < IDX_DIM, tr, 0.0)
    ki2o[:, :LANES] = ki.astype(BF16)
    ki2o[:, LANES:] = pltpu.roll(ki, IDX_DIM, 1).astype(BF16)


def _prep(proj, tables, row0, rows, table_block):
    b0 = row0 // PREP_TM

    def cb(c):
        return pl.BlockSpec((PREP_TM, MIX_W), lambda i: (b0 + i, c))

    row512 = pl.BlockSpec((PREP_TM, MIX_W), lambda i: (i, 0))
    head_rows = pl.BlockSpec((PREP_TM * N_HEADS, HEAD_DIM), lambda i: (i, 0))
    return pl.pallas_call(
        _prep_kernel,
        grid=(rows // PREP_TM,),
        in_specs=[cb(CB_Q), cb(CB_K), cb(CB_V), cb(CB_QI),
                  pl.BlockSpec((PREP_TM, LANES), lambda i: (b0 + i, TAIL_COL // LANES)),
                  pl.BlockSpec((6, PREP_TM, LANES), lambda i: (0, table_block(i), 0))],
        out_specs=[row512, head_rows, row512, head_rows, row512, row512,
                   pl.BlockSpec((PREP_TM, LANES), lambda i: (i, 0)),
                   pl.BlockSpec((PREP_TM, 2 * LANES), lambda i: (i, 0))],
        out_shape=[jax.ShapeDtypeStruct((rows, MIX_W), BF16),
                   jax.ShapeDtypeStruct((rows * N_HEADS, HEAD_DIM), F32),
                   jax.ShapeDtypeStruct((rows, MIX_W), BF16),
                   jax.ShapeDtypeStruct((rows * N_HEADS, HEAD_DIM), F32),
                   jax.ShapeDtypeStruct((rows, MIX_W), BF16),
                   jax.ShapeDtypeStruct((rows, MIX_W), BF16),
                   jax.ShapeDtypeStruct((rows, LANES), F32),
                   jax.ShapeDtypeStruct((rows, 2 * LANES), BF16)],
        compiler_params=_params(1),
        name="prep",
    )(proj, proj, proj, proj, proj, tables)


KEY_NEG_INF = INT_MIN + 0x7FFFFF


def _key_to_float(key):
    return lax.bitcast_convert_type(jnp.where(key < 0, key ^ 0x7FFFFFFF, key), F32)


def _select_chunks(sc_ref, bias_ref, nc, k, allowed_fn):
    _, rows, cw = sc_ref.shape
    kf = float(k)

    def count_ge(c):
        acc = jnp.where(sc_ref[0] >= c, 1.0, 0.0)
        for j in range(1, nc):
            acc = acc + jnp.where(sc_ref[j] >= c, 1.0, 0.0)
        return jnp.sum(acc, axis=-1, keepdims=True)

    t = jnp.where(count_ge(jnp.zeros((rows, 1), F32)) >= kf, 0, INT_MIN).astype(I32)
    for bit in range(30, -1, -1):
        cand = t + (1 << bit)
        ok = jnp.logical_or(count_ge(_key_to_float(cand)) >= kf, cand <= KEY_NEG_INF)
        t = jnp.where(ok, cand, t)
    lo = _key_to_float(t)
    hi = _key_to_float(t + 1)
    need = kf - count_ge(hi)
    r_io = lax.broadcasted_iota(I32, (LANES, LANES), 0)
    c_io = lax.broadcasted_iota(I32, (LANES, LANES), 1)
    tri = jnp.where(r_io <= c_io, 1.0, 0.0).astype(BF16)

    off = jnp.zeros((rows, 1), F32)
    for j in range(nc):
        for u in range(cw // LANES):
            sl = slice(u * LANES, (u + 1) * LANES)
            su = sc_ref[j, :, sl]
            above = su >= hi
            e = jnp.where(above, 0.0, jnp.where(su >= lo, 1.0, 0.0))
            rank = _dot(e.astype(BF16), tri) + off
            off = off + jnp.sum(e, axis=-1, keepdims=True)
            take = jnp.where(above, 1.0, jnp.where(rank <= need, e, 0.0))
            take = jnp.where(allowed_fn(j, u, su), take, 0.0)
            bias_ref[j, :, sl] = jnp.where(take > 0.5, 0.0, NEG_INF)


def _index_scores(qi, w_scaled, ki_a, ki_b):
    acc = None
    for p in range(IDX_HEADS // 2):
        qp = qi[:, p * LANES:(p + 1) * LANES]
        for half, ki in enumerate((ki_a, ki_b)):
            hh = 2 * p + half
            d = _dot_nt(qp, ki) * (IDX_DIM ** -0.5)
            term = jnp.maximum(d, 0.0) * w_scaled[:, IDX_DIM + hh:IDX_DIM + hh + 1]
            acc = term if acc is None else acc + term
    return acc


KEY_CHUNK = 256
N_KEY_CHUNKS = SEQ // KEY_CHUNK
CAUSAL_VARIANTS = 8


def _dsa_prompt_kernel(q_ref, qi_ref, t_ref, k_ref, v_ref, ki2_ref, o_ref, sc_ref, bias_ref):
    i = pl.program_id(1)
    row = i * QBLOCK + lax.broadcasted_iota(I32, (QBLOCK, KEY_CHUNK), 0)
    lane = lax.broadcasted_iota(I32, (QBLOCK, KEY_CHUNK), 1)
    row_g = i * QBLOCK + lax.broadcasted_iota(I32, (QBLOCK, LANES), 0)
    lane_g = lax.broadcasted_iota(I32, (QBLOCK, LANES), 1)

    def causal(j, u, su):
        return j * KEY_CHUNK + u * LANES + lane_g <= row_g

    def attend(n_keys):
        nc = n_keys // KEY_CHUNK
        w_scaled = t_ref[...] * (IDX_HEADS ** -0.5)
        sc = _index_scores(qi_ref[...], w_scaled, ki2_ref[:n_keys, :LANES], ki2_ref[:n_keys, LANES:])
        for j in range(nc):
            sc_ref[j] = jnp.where(j * KEY_CHUNK + lane <= row, sc[:, j * KEY_CHUNK:(j + 1) * KEY_CHUNK], NEG_INF)
        _select_chunks(sc_ref.at[:nc], bias_ref.at[:nc], nc, TOPK, causal)
        bias = jnp.concatenate([bias_ref[j] for j in range(nc)], axis=-1)
        for h in range(N_HEADS):
            sl = slice(h * HEAD_DIM, (h + 1) * HEAD_DIM)
            s = _dot_nt(q_ref[:, sl], k_ref[:n_keys, sl]) * (HEAD_DIM ** -0.5) + bias
            m = jnp.max(s, axis=-1, keepdims=True)
            p = jnp.exp(s - m)
            l = jnp.sum(p, axis=-1, keepdims=True)
            o_ref[:, sl] = (_dot(p.astype(BF16), v_ref[:n_keys, sl]) / l).astype(BF16)

    blocks_per_variant = (SEQ // QBLOCK) // CAUSAL_VARIANTS
    for v in range(CAUSAL_VARIANTS):
        @pl.when(i // blocks_per_variant == v)
        def _(v=v):
            attend((v + 1) * (SEQ // CAUSAL_VARIANTS))


def _dsa_prompt(q, qi, tail, kb, vb, ki2):
    nb = SEQ // QBLOCK

    def qspec(w):
        return pl.BlockSpec((QBLOCK, w), lambda b, i: (b * nb + i, 0))

    def kspec(w):
        return pl.BlockSpec((SEQ, w), lambda b, i: (b, 0))

    return pl.pallas_call(
        _dsa_prompt_kernel,
        grid=(BATCH, nb),
        in_specs=[qspec(MIX_W), qspec(MIX_W), qspec(LANES), kspec(MIX_W), kspec(MIX_W), kspec(2 * LANES)],
        out_specs=qspec(MIX_W),
        out_shape=jax.ShapeDtypeStruct((MP, MIX_W), BF16),
        scratch_shapes=[pltpu.VMEM((N_KEY_CHUNKS, QBLOCK, KEY_CHUNK), F32)] * 2,
        compiler_params=_params(2),
        name="dsa_prompt",
    )(q, qi, tail, kb, vb, ki2)


def _dsa_sample_scores_kernel(pt_ref, qi_ref, w_ref, *rest):
    page_refs, knew_ref, o_ref = rest[:N_PAGES], rest[N_PAGES], rest[N_PAGES + 1]
    qi = qi_ref[0]
    w = w_ref[0] * (IDX_HEADS ** -0.5)

    def chunk_scores(d):
        r = jnp.maximum(d * (IDX_DIM ** -0.5), 0.0) * w
        return jnp.sum(r.reshape(DEC_SEQ, IDX_HEADS, LANES), axis=1)

    for p in range(N_PAGES):
        o_ref[0, :, p * LANES:(p + 1) * LANES] = chunk_scores(_dot(qi, page_refs[p][0, 0].astype(BF16)))
    new = chunk_scores(_dot_nt(qi, knew_ref[0]))
    tq = lax.broadcasted_iota(I32, (DEC_SEQ, LANES), 0)
    jk = lax.broadcasted_iota(I32, (DEC_SEQ, LANES), 1)
    o_ref[0, :, PAST:] = jnp.where(jk <= tq, new, NEG_INF)


def _dsa_sample_scores(page_table, qi32, w32, cache_kidx_t, layer, knew_pad):
    page_specs = [pl.BlockSpec((1, 1, IDX_DIM, PAGE_SIZE),
                               functools.partial(lambda b, pt, p: (layer, pt[b, p], 0, 0), p=p))
                  for p in range(N_PAGES)]
    grid_spec = pltpu.PrefetchScalarGridSpec(
        num_scalar_prefetch=1,
        grid=(DEC_BATCH,),
        in_specs=[pl.BlockSpec((1, DEC_SEQ * IDX_HEADS, IDX_DIM), lambda b, pt: (b, 0, 0)),
                  pl.BlockSpec((1, DEC_SEQ * IDX_HEADS, 1), lambda b, pt: (b, 0, 0))] + page_specs
                 + [pl.BlockSpec((1, LANES, IDX_DIM), lambda b, pt: (b, 0, 0))],
        out_specs=pl.BlockSpec((1, DEC_SEQ, SAMPLE_KEYS), lambda b, pt: (b, 0, 0)),
    )
    return pl.pallas_call(
        _dsa_sample_scores_kernel,
        grid_spec=grid_spec,
        out_shape=jax.ShapeDtypeStruct((DEC_BATCH, DEC_SEQ, SAMPLE_KEYS), F32),
        compiler_params=_params(1),
        name="dsa_sample_scores",
    )(page_table, qi32, w32, *([cache_kidx_t] * N_PAGES), knew_pad)


SEL_ROWS = 128
N_SAMPLE_CHUNKS = SAMPLE_KEYS // LANES


def _sample_select_kernel(in_ref, o_ref, sc_ref, bias_ref):
    for c in range(N_SAMPLE_CHUNKS):
        sc_ref[c] = in_ref[:, c * LANES:(c + 1) * LANES]
    _select_chunks(sc_ref, bias_ref, N_SAMPLE_CHUNKS, TOPK, lambda j, u, su: su > NEG_INF)
    for c in range(N_SAMPLE_CHUNKS):
        o_ref[:, c * LANES:(c + 1) * LANES] = bias_ref[c]


def _sample_select(sc):
    return pl.pallas_call(
        _sample_select_kernel,
        grid=(MS // SEL_ROWS,),
        in_specs=[pl.BlockSpec((SEL_ROWS, SAMPLE_KEYS), lambda i: (i, 0))],
        out_specs=pl.BlockSpec((SEL_ROWS, SAMPLE_KEYS), lambda i: (i, 0)),
        out_shape=jax.ShapeDtypeStruct((MS, SAMPLE_KEYS), F32),
        scratch_shapes=[pltpu.VMEM((N_SAMPLE_CHUNKS, SEL_ROWS, LANES), F32)] * 2,
        compiler_params=_params(1),
        name="sample_select",
    )(sc)


def _dsa_sample_attn_kernel(pt_ref, q_ref, bias_ref, *rest):
    k_refs = rest[:N_PAGES]
    v_refs = rest[N_PAGES:2 * N_PAGES]
    knew_ref, vnew_ref, o_ref, s_ref = rest[2 * N_PAGES:]
    q = q_ref[0]
    def head_rows(h):
        return pl.ds(h, PAGE_SIZE, stride=N_HEADS)

    for h in range(N_HEADS):
        rs = slice(h * DEC_SEQ, (h + 1) * DEC_SEQ)
        hs = slice(h * HEAD_DIM, (h + 1) * HEAD_DIM)
        qh = q[:, hs]
        for p in range(N_PAGES):
            s_ref[rs, p * LANES:(p + 1) * LANES] = _dot_nt(qh, k_refs[p][0, 0, head_rows(h), :].astype(BF16))
        s_ref[rs, PAST:] = _dot_nt(qh, knew_ref[0, :, hs])
    bias = bias_ref[0]
    s = s_ref[...] * (HEAD_DIM ** -0.5) + jnp.concatenate([bias] * N_HEADS, axis=0)
    m = jnp.max(s, axis=-1, keepdims=True)
    pr = jnp.exp(s - m)
    l = jnp.sum(pr, axis=-1, keepdims=True)
    pb = pr.astype(BF16)
    for h in range(N_HEADS):
        rs = slice(h * DEC_SEQ, (h + 1) * DEC_SEQ)
        hs = slice(h * HEAD_DIM, (h + 1) * HEAD_DIM)
        acc = _dot(pb[rs, PAST:], vnew_ref[0, :, hs])
        for p in range(N_PAGES):
            acc = acc + _dot(pb[rs, p * LANES:(p + 1) * LANES], v_refs[p][0, 0, head_rows(h), :].astype(BF16))
        o_ref[0, :, hs] = (acc / l[rs]).astype(BF16)


def _dsa_sample_attn(page_table, q, bias, cache_k, cache_v, layer, knew_pad, vnew_pad):
    def page_spec(p):
        return pl.BlockSpec((1, 1, PAGE_SIZE * N_HEADS, HEAD_DIM),
                            functools.partial(lambda b, pt, p: (layer, pt[b, p], 0, 0), p=p))

    grid_spec = pltpu.PrefetchScalarGridSpec(
        num_scalar_prefetch=1,
        grid=(DEC_BATCH,),
        in_specs=[pl.BlockSpec((1, DEC_SEQ, MIX_W), lambda b, pt: (b, 0, 0)),
                  pl.BlockSpec((1, DEC_SEQ, SAMPLE_KEYS), lambda b, pt: (b, 0, 0))]
                 + [page_spec(p) for p in range(N_PAGES)] + [page_spec(p) for p in range(N_PAGES)]
                 + [pl.BlockSpec((1, LANES, MIX_W), lambda b, pt: (b, 0, 0)),
                    pl.BlockSpec((1, LANES, MIX_W), lambda b, pt: (b, 0, 0))],
        out_specs=pl.BlockSpec((1, DEC_SEQ, MIX_W), lambda b, pt: (b, 0, 0)),
        scratch_shapes=[pltpu.VMEM((N_HEADS * DEC_SEQ, SAMPLE_KEYS), F32)],
    )
    return pl.pallas_call(
        _dsa_sample_attn_kernel,
        grid_spec=grid_spec,
        out_shape=jax.ShapeDtypeStruct((DEC_BATCH, DEC_SEQ, MIX_W), BF16),
        compiler_params=_params(1),
        name="dsa_sample_attn",
    )(page_table, q, bias, *([cache_k] * N_PAGES), *([cache_v] * N_PAGES), knew_pad, vnew_pad)


CONV_TT = 512


def _conv_prompt_kernel(cx_ref, cb_ref, cc_ref, w_ref, yb_ref, buf_ref, carry_ref):
    j = pl.program_id(1)

    @pl.when(j == 0)
    def _():
        carry_ref[...] = jnp.zeros_like(carry_ref)

    z = cc_ref[...] * cx_ref[...]
    row = lax.broadcasted_iota(I32, z.shape, 0)
    c0 = carry_ref[0:1, :]
    c1 = carry_ref[1:2, :]
    zm1 = jnp.where(row == 0, c1, pltpu.roll(z, 1, 0))
    zm2 = jnp.where(row == 0, c0, jnp.where(row == 1, c1, pltpu.roll(z, 2, 0)))
    zc = w_ref[0:1, :] * zm2 + w_ref[1:2, :] * zm1 + w_ref[2:3, :] * z
    yb_ref[...] = (cb_ref[...] * zc).astype(BF16)
    last = z[CONV_TT - 2:CONV_TT, :]
    carry_ref[0:2, :] = last
    buf_ref[0] = last


def _conv_prompt(proj, conv_w):
    nt = SEQ // CONV_TT

    def cb(c):
        return pl.BlockSpec((CONV_TT, MIX_W), lambda b, j: (b * nt + j, c))

    return pl.pallas_call(
        _conv_prompt_kernel,
        grid=(BATCH, nt),
        in_specs=[cb(CB_CX), cb(CB_CB), cb(CB_CC), pl.BlockSpec((3, MIX_W), lambda b, j: (0, 0))],
        out_specs=[pl.BlockSpec((CONV_TT, MIX_W), lambda b, j: (b * nt + j, 0)),
                   pl.BlockSpec((1, 2, MIX_W), lambda b, j: (b, 0, 0))],
        out_shape=[jax.ShapeDtypeStruct((MP, MIX_W), BF16),
                   jax.ShapeDtypeStruct((BATCH, 2, MIX_W), F32)],
        scratch_shapes=[pltpu.VMEM((8, MIX_W), F32)],
        compiler_params=_params(2),
        name="conv_prompt",
    )(proj, proj, proj, conv_w)


def _gmlp_prompt_kernel(gu_ref, gv_ref, w_ref, bt_ref, o_ref):
    r_io = lax.broadcasted_iota(I32, (GMLP_CHUNK, GMLP_CHUNK), 0)
    c_io = lax.broadcasted_iota(I32, (GMLP_CHUNK, GMLP_CHUNK), 1)
    tril = c_io <= r_io
    for g in range(GMLP_GROUPS):
        sl = slice(g * LANES, (g + 1) * LANES)
        wm = jnp.where(tril, w_ref[g], 0.0).astype(BF16)
        z = _dot(wm, gv_ref[:, sl].astype(BF16)) + bt_ref[:, g:g + 1]
        o_ref[:, sl] = (gu_ref[:, sl] * z).astype(BF16)


def _gmlp_prompt(proj, gmlp_w, gmlp_bt):
    def cb(c):
        return pl.BlockSpec((GMLP_CHUNK, MIX_W), lambda i: (i, c))

    return pl.pallas_call(
        _gmlp_prompt_kernel,
        grid=(MP // GMLP_CHUNK,),
        in_specs=[cb(CB_GU), cb(CB_GV),
                  pl.BlockSpec((GMLP_GROUPS, GMLP_CHUNK, GMLP_CHUNK), lambda i: (0, 0, 0)),
                  pl.BlockSpec((GMLP_CHUNK, GMLP_GROUPS), lambda i: (0, 0))],
        out_specs=pl.BlockSpec((GMLP_CHUNK, MIX_W), lambda i: (i, 0)),
        out_shape=jax.ShapeDtypeStruct((MP, MIX_W), BF16),
        compiler_params=_params(1),
        name="gmlp_prompt",
    )(proj, proj, gmlp_w, gmlp_bt)


def _mix_sample_kernel(cx_ref, cb_ref, cc_ref, gu_ref, gv_ref, buf_ref, cw_ref, gw_ref, gb_ref,
                       yb_ref, nbuf_ref, yd_ref):
    zf = [buf_ref[:, 0, :], buf_ref[:, 1, :]]
    for t in range(DEC_SEQ):
        zf.append(cc_ref[:, t, :] * cx_ref[:, t, :])
    for t in range(DEC_SEQ):
        zc = cw_ref[0:1, :] * zf[t] + cw_ref[1:2, :] * zf[t + 1] + cw_ref[2:3, :] * zf[t + 2]
        yb_ref[:, t, :] = (cb_ref[:, t, :] * zc).astype(BF16)
    nbuf_ref[:, 0, :] = zf[DEC_SEQ]
    nbuf_ref[:, 1, :] = zf[DEC_SEQ + 1]
    v = [gv_ref[:, s, :] for s in range(DEC_SEQ)]
    for t in range(DEC_SEQ):
        z = gb_ref[t:t + 1, :]
        for s in range(t + 1):
            z = z + gw_ref[t * DEC_SEQ + s:t * DEC_SEQ + s + 1, :] * v[s]
        yd_ref[:, t, :] = (gu_ref[:, t, :] * z).astype(BF16)


def _mix_sample(proj_s, state_conv_l, conv_w, gw_lane, gb_lane):
    def cb(c):
        return pl.BlockSpec((DEC_BATCH, DEC_SEQ, MIX_W), lambda i: (0, 0, c))

    full3 = pl.BlockSpec((DEC_BATCH, DEC_SEQ, MIX_W), lambda i: (0, 0, 0))
    buf3 = pl.BlockSpec((DEC_BATCH, 2, MIX_W), lambda i: (0, 0, 0))
    return pl.pallas_call(
        _mix_sample_kernel,
        grid=(1,),
        in_specs=[cb(CB_CX), cb(CB_CB), cb(CB_CC), cb(CB_GU), cb(CB_GV), buf3,
                  pl.BlockSpec((3, MIX_W), lambda i: (0, 0)),
                  pl.BlockSpec((DEC_SEQ * DEC_SEQ, MIX_W), lambda i: (0, 0)),
                  pl.BlockSpec((DEC_SEQ, MIX_W), lambda i: (0, 0))],
        out_specs=[full3, buf3, full3],
        out_shape=[jax.ShapeDtypeStruct((DEC_BATCH, DEC_SEQ, MIX_W), BF16),
                   jax.ShapeDtypeStruct((DEC_BATCH, 2, MIX_W), F32),
                   jax.ShapeDtypeStruct((DEC_BATCH, DEC_SEQ, MIX_W), BF16)],
        compiler_params=_params(1),
        name="mix_sample",
    )(proj_s, proj_s, proj_s, proj_s, proj_s, state_conv_l, conv_w, gw_lane, gb_lane)


def _ssm_params_kernel(ar_ref, ai_ref, ldt_ref, br_ref, bi_ref, ab_ref, bbr_ref, bbi_ref):
    ar, ai = ar_ref[...], ai_ref[...]
    dt = jnp.exp(ldt_ref[...])
    mag = jnp.exp(dt * ar)
    abar_re, abar_im = mag * jnp.cos(dt * ai), mag * jnp.sin(dt * ai)
    den = ar * ar + ai * ai
    nr, ni = abar_re - 1.0, abar_im
    coef_re = (nr * ar + ni * ai) / den
    coef_im = (ni * ar - nr * ai) / den
    ab_ref[...] = jnp.zeros_like(ab_ref)
    ab_ref[0:1, :] = abar_re
    ab_ref[1:2, :] = abar_im
    br, bi = br_ref[...], bi_ref[...]
    bbr_ref[...] = (coef_re * br - coef_im * bi).astype(BF16)
    bbi_ref[...] = (coef_re * bi + coef_im * br).astype(BF16)


def _ssm_params(a_re, a_im, ldt, bd_re, bd_im):
    row = pl.BlockSpec((1, SSM_STATE), lambda i: (0, 0))
    mat = pl.BlockSpec((MIX_W, SSM_STATE), lambda i: (0, 0))
    return pl.pallas_call(
        _ssm_params_kernel,
        grid=(1,),
        in_specs=[row, row, row, mat, mat],
        out_specs=[pl.BlockSpec((8, SSM_STATE), lambda i: (0, 0)), mat, mat],
        out_shape=[jax.ShapeDtypeStruct((8, SSM_STATE), F32),
                   jax.ShapeDtypeStruct((MIX_W, SSM_STATE), BF16),
                   jax.ShapeDtypeStruct((MIX_W, SSM_STATE), BF16)],
        compiler_params=_params(1),
        name="ssm_params",
    )(a_re, a_im, ldt, bd_re, bd_im)


def _gelu_tanh(x):
    return 0.5 * x * (1.0 + jnp.tanh(0.7978845608028654 * (x + 0.044715 * (x * x * x))))


SSM_BLOCK_STATES = SSM_STATE // (MIX_W // LANES)


def _ssm_drive(ub, bb_ref):
    return jnp.concatenate(
        [_dot(ub[:, c * LANES:(c + 1) * LANES],
              bb_ref[c * LANES:(c + 1) * LANES, c * SSM_BLOCK_STATES:(c + 1) * SSM_BLOCK_STATES])
         for c in range(MIX_W // LANES)], axis=-1)


def _ssm_readout(u, hr, hi, cr_ref, ci_ref, d_ref, wg_ref, bg_ref):
    hrb, hib = hr.astype(BF16), hi.astype(BF16)
    parts = []
    for c in range(MIX_W // LANES):
        ss = slice(c * SSM_BLOCK_STATES, (c + 1) * SSM_BLOCK_STATES)
        cs = slice(c * LANES, (c + 1) * LANES)
        parts.append(_dot(hrb[:, ss], cr_ref[ss, cs]) - _dot(hib[:, ss], ci_ref[ss, cs]))
    y = jnp.concatenate(parts, axis=-1) + d_ref[...] * u
    g = _gelu_tanh(y)
    return g * _sigmoid(_dot(g.astype(BF16), wg_ref[...]) + bg_ref[...])


SSM_TC = 128
SSM_SLABS = MIX_W // LANES


def _ssm_prompt_kernel(u0, u1, u2, u3, ab_ref, bbr_ref, bbi_ref, cr_ref, ci_ref, d_ref, wg_ref, bg_ref,
                       y_ref, str_ref, sti_ref, hr_ref, hi_ref, carry_ref, il_ref):
    j = pl.program_id(0)

    @pl.when(j == 0)
    def _():
        carry_ref[...] = jnp.zeros_like(carry_ref)

    def batch_rows(b):
        return pl.ds(b, SSM_TC, stride=BATCH)

    for b, u_ref in enumerate((u0, u1, u2, u3)):
        for c in range(SSM_SLABS):
            il_ref[c, batch_rows(b), :] = u_ref[:, c * LANES:(c + 1) * LANES]
    u = jnp.concatenate([il_ref[c] for c in range(SSM_SLABS)], axis=-1)
    ub = u.astype(BF16)
    hr_ref[...] = _ssm_drive(ub, bbr_ref)
    hi_ref[...] = _ssm_drive(ub, bbi_ref)
    ar, ai = ab_ref[0:1, :], ab_ref[1:2, :]

    def step2(t2, carry):
        pr, pi = carry
        rows = pl.ds(pl.multiple_of(t2 * 2 * BATCH, 2 * BATCH), 2 * BATCH)
        br, bi = hr_ref[rows, :], hi_ref[rows, :]
        er = ar * pr - ai * pi + br[:BATCH]
        ei = ar * pi + ai * pr + bi[:BATCH]
        nr = ar * er - ai * ei + br[BATCH:]
        ni = ar * ei + ai * er + bi[BATCH:]
        hr_ref[rows, :] = jnp.concatenate([er, nr], axis=0)
        hi_ref[rows, :] = jnp.concatenate([ei, ni], axis=0)
        return nr, ni

    fr, fi = lax.fori_loop(0, SSM_TC // 2, step2, (carry_ref[0:BATCH, :], carry_ref[BATCH:2 * BATCH, :]), unroll=4)
    carry_ref[0:BATCH, :] = fr
    carry_ref[BATCH:2 * BATCH, :] = fi
    str_ref[...] = fr
    sti_ref[...] = fi
    y = _ssm_readout(u, hr_ref[...], hi_ref[...], cr_ref, ci_ref, d_ref, wg_ref, bg_ref)
    for c in range(SSM_SLABS):
        il_ref[c] = y[:, c * LANES:(c + 1) * LANES]
    for b in range(BATCH):
        for c in range(SSM_SLABS):
            y_ref[b, :, c * LANES:(c + 1) * LANES] = il_ref[c, batch_rows(b), :].astype(BF16)


def _ssm_prompt(proj, ab, bbr, bbi, cdr, cdi, d, wglu, bglu):
    nt = SEQ // SSM_TC

    def const(shape):
        return pl.BlockSpec(shape, lambda j: (0,) * len(shape))

    u_specs = [pl.BlockSpec((SSM_TC, MIX_W), functools.partial(lambda j, b: (b * nt + j, CB_SU), b=b))
               for b in range(BATCH)]
    state = pl.BlockSpec((BATCH, SSM_STATE), lambda j: (0, 0))
    return pl.pallas_call(
        _ssm_prompt_kernel,
        grid=(nt,),
        in_specs=u_specs + [const((8, SSM_STATE)), const((MIX_W, SSM_STATE)), const((MIX_W, SSM_STATE)),
                            const((SSM_STATE, MIX_W)), const((SSM_STATE, MIX_W)), const((1, MIX_W)),
                            const((MIX_W, MIX_W)), const((1, MIX_W))],
        out_specs=[pl.BlockSpec((BATCH, SSM_TC, MIX_W), lambda j: (0, j, 0)), state, state],
        out_shape=[jax.ShapeDtypeStruct((BATCH, SEQ, MIX_W), BF16),
                   jax.ShapeDtypeStruct((BATCH, SSM_STATE), F32),
                   jax.ShapeDtypeStruct((BATCH, SSM_STATE), F32)],
        scratch_shapes=[pltpu.VMEM((BATCH * SSM_TC, SSM_STATE), F32), pltpu.VMEM((BATCH * SSM_TC, SSM_STATE), F32),
                        pltpu.VMEM((2 * BATCH, SSM_STATE), F32),
                        pltpu.VMEM((SSM_SLABS, BATCH * SSM_TC, LANES), F32)],
        compiler_params=_params(1),
        name="ssm_prompt",
    )(proj, proj, proj, proj, ab, bbr, bbi, cdr, cdi, d, wglu, bglu)


def _ssm_sample_kernel(u_ref, h0r_ref, h0i_ref, ab_ref, bbr_ref, bbi_ref, cr_ref, ci_ref, d_ref, wg_ref, bg_ref,
                       y_ref, nr_ref, ni_ref):
    ar, ai = ab_ref[0:1, :], ab_ref[1:2, :]
    hr, hi = h0r_ref[...], h0i_ref[...]
    for t in range(DEC_SEQ):
        u = u_ref[:, t, :]
        ub = u.astype(BF16)
        hr, hi = (ar * hr - ai * hi + _ssm_drive(ub, bbr_ref),
                  ar * hi + ai * hr + _ssm_drive(ub, bbi_ref))
        y_ref[:, t, :] = _ssm_readout(u, hr, hi, cr_ref, ci_ref, d_ref, wg_ref, bg_ref).astype(BF16)
    nr_ref[...] = hr
    ni_ref[...] = hi


def _ssm_sample(proj_s, h0r, h0i, ab, bbr, bbi, cdr, cdi, d, wglu, bglu):
    def const(shape):
        return pl.BlockSpec(shape, lambda i: (0,) * len(shape))

    st = const((DEC_BATCH, SSM_STATE))
    return pl.pallas_call(
        _ssm_sample_kernel,
        grid=(1,),
        in_specs=[pl.BlockSpec((DEC_BATCH, DEC_SEQ, MIX_W), lambda i: (0, 0, CB_SU)), st, st,
                  const((8, SSM_STATE)), const((MIX_W, SSM_STATE)), const((MIX_W, SSM_STATE)),
                  const((SSM_STATE, MIX_W)), const((SSM_STATE, MIX_W)), const((1, MIX_W)),
                  const((MIX_W, MIX_W)), const((1, MIX_W))],
        out_specs=[const((DEC_BATCH, DEC_SEQ, MIX_W)), st, st],
        out_shape=[jax.ShapeDtypeStruct((DEC_BATCH, DEC_SEQ, MIX_W), BF16),
                   jax.ShapeDtypeStruct((DEC_BATCH, SSM_STATE), F32),
                   jax.ShapeDtypeStruct((DEC_BATCH, SSM_STATE), F32)],
        compiler_params=_params(1),
        name="ssm_sample",
    )(proj_s, h0r, h0i, ab, bbr, bbi, cdr, cdi, d, wglu, bglu)


def _rope_tables():
    lane = jnp.arange(LANES)

    def inv(half):
        return ROPE_THETA ** (-jnp.arange(half, dtype=F32) / half)

    inv_qk = jnp.where(lane < ROT_DIM, inv(ROT_DIM // 2)[lane % (ROT_DIM // 2)], 0.0)
    idx = inv(IDX_ROT // 2)[lane % (IDX_ROT // 2)]
    inv_i = jnp.where((lane % IDX_DIM) < IDX_ROT, idx, 0.0)
    inv_t = jnp.where(lane < IDX_ROT, idx, 0.0)
    pos = jnp.concatenate([jnp.arange(SEQ), PAST + jnp.arange(PREP_TM) % DEC_SEQ]).astype(F32)
    tabs = []
    for inv_row in (inv_qk, inv_i, inv_t):
        ang = pos[:, None] * inv_row[None, :].astype(F32)
        tabs += [jnp.cos(ang), jnp.sin(ang)]
    return jnp.stack(tabs)


def _pad_rows(a, rows):
    return jnp.pad(a, ((0, 0), (0, rows - a.shape[1]), (0, 0)))


def kernel(x_prompt, x_sample, cache_k, cache_v, cache_kidx, state_conv, state_ssm_re, state_ssm_im, page_table,
           norm_mix, w_in, conv_w, ssm_a_re, ssm_a_im, ssm_log_dt, ssm_b_re, ssm_b_im, ssm_c_re, ssm_c_im, ssm_d,
           w_glu, b_glu, gmlp_w, gmlp_b, w_branch, w_out, norm_ffn, w_ffn_in, w_ffn_out, norm_final):
    x = jnp.concatenate([x_prompt.reshape(MP, D_MODEL), x_sample.reshape(MS, D_MODEL)], axis=0)
    tables = _rope_tables()
    eye_g = jnp.eye(SSM_GROUPS, dtype=F32)
    cache_kidx_t = jnp.swapaxes(cache_kidx, 2, 3)
    n_pool = cache_k.shape[1]
    cache_k2 = cache_k.reshape(DEPTH, n_pool, PAGE_SIZE * N_HEADS, HEAD_DIM)
    cache_v2 = cache_v.reshape(DEPTH, n_pool, PAGE_SIZE * N_HEADS, HEAD_DIM)
    n_gate0 = w_in.shape[2] - N_BRANCH * D_MODEL
    c_ki = 3 * MIX_W + IDX_HEADS * IDX_DIM
    c_cx = c_ki + IDX_DIM + IDX_HEADS
    w_branch_b, w_out_b = w_branch.astype(BF16), w_out.astype(BF16)
    w_ffn_in_b, w_ffn_out_b = w_ffn_in.astype(BF16), w_ffn_out.astype(BF16)

    st_p, st_s = [], []
    for l in range(DEPTH):
        wl = w_in[l]
        w_main = jnp.concatenate(
            [wl[:, :c_ki], wl[:, c_cx:n_gate0], wl[:, c_ki:c_cx],
             jnp.zeros((D_MODEL, N_MAIN - n_gate0), F32)], axis=1).astype(BF16)
        w_gate = wl[:, n_gate0:].astype(BF16)

        proj = _norm_mm(x, norm_mix[l][None, :], w_main, 1088, 768)
        q_p, k_p, kb_p, v_p, vb_p, qi_p, tail_p, ki2_p = _prep(
            proj, tables, 0, MP, lambda i: i % (SEQ // PREP_TM))
        q_s, k_s, kb_s, v_s, vb_s, qi_s, tail_s, _ = _prep(
            proj, tables, MP, MS, lambda i: SEQ // PREP_TM)

        ya_p = _dsa_prompt(q_p, qi_p, tail_p, kb_p, vb_p, ki2_p)
        tail_s3 = tail_s.reshape(DEC_BATCH, DEC_SEQ, LANES)
        w_s = tail_s3[:, :, IDX_DIM:IDX_DIM + IDX_HEADS].reshape(DEC_BATCH, DEC_SEQ * IDX_HEADS, 1)
        kinew = _pad_rows(tail_s3[:, :, :IDX_DIM].astype(BF16), LANES)
        sc_s = _dsa_sample_scores(page_table, qi_s.reshape(DEC_BATCH, DEC_SEQ * IDX_HEADS, IDX_DIM), w_s,
                                  cache_kidx_t, l, kinew)
        bias_s = _sample_select(sc_s.reshape(MS, SAMPLE_KEYS)).reshape(DEC_BATCH, DEC_SEQ, SAMPLE_KEYS)
        knew = _pad_rows(kb_s.reshape(DEC_BATCH, DEC_SEQ, MIX_W), LANES)
        vnew = _pad_rows(vb_s.reshape(DEC_BATCH, DEC_SEQ, MIX_W), LANES)
        ya_s = _dsa_sample_attn(page_table, q_s.reshape(DEC_BATCH, DEC_SEQ, MIX_W), bias_s, cache_k2, cache_v2, l,
                                knew, vnew)

        proj_s = proj[MP:].reshape(DEC_BATCH, DEC_SEQ, N_MAIN)
        yb_p, buf_p = _conv_prompt(proj, conv_w[l])
        yd_p = _gmlp_prompt(proj, gmlp_w[l], gmlp_b[l].T)
        gw_lane = jnp.repeat(gmlp_w[l][:, :DEC_SEQ, :DEC_SEQ].transpose(1, 2, 0).reshape(DEC_SEQ * DEC_SEQ, GMLP_GROUPS),
                             LANES, axis=1)
        gb_lane = jnp.repeat(gmlp_b[l][:, :DEC_SEQ].T, LANES, axis=1)
        yb_s, buf_s, yd_s = _mix_sample(proj_s, state_conv[l], conv_w[l], gw_lane, gb_lane)

        bd_re = jnp.einsum('gnp,gh->gphn', ssm_b_re[l], eye_g).reshape(MIX_W, SSM_STATE)
        bd_im = jnp.einsum('gnp,gh->gphn', ssm_b_im[l], eye_g).reshape(MIX_W, SSM_STATE)
        cd_re = jnp.einsum('gpn,gh->gnhp', ssm_c_re[l], eye_g).reshape(SSM_STATE, MIX_W).astype(BF16)
        cd_im = jnp.einsum('gpn,gh->gnhp', ssm_c_im[l], eye_g).reshape(SSM_STATE, MIX_W).astype(BF16)
        ldt = jnp.repeat(ssm_log_dt[l], SSM_N)[None, :]
        ab, bbr, bbi = _ssm_params(ssm_a_re[l].reshape(1, SSM_STATE), ssm_a_im[l].reshape(1, SSM_STATE), ldt,
                                   bd_re, bd_im)
        ssm_consts = (ab, bbr, bbi, cd_re, cd_im, ssm_d[l][None, :], w_glu[l].astype(BF16), b_glu[l][None, :])
        yc_p, hre_p, him_p = _ssm_prompt(proj, *ssm_consts)
        yc_p = yc_p.reshape(MP, MIX_W)
        yc_s, nre_s, nim_s = _ssm_sample(proj_s, state_ssm_re[l].reshape(DEC_BATCH, SSM_STATE),
                                         state_ssm_im[l].reshape(DEC_BATCH, SSM_STATE), *ssm_consts)

        ys_sample = tuple(y.reshape(MS, MIX_W) for y in (ya_s, yb_s, yc_s, yd_s))
        merged = _gate_merge(x, norm_mix[l][None, :], w_gate, (ya_p, yb_p, yc_p, yd_p), ys_sample, w_branch_b, l, 512)
        x = _mm_res(merged, w_out_b, l, x, 1088, 512)
        act = _swiglu(x, norm_ffn[l][None, :], w_ffn_in_b, l, 1088, 512)
        x = _mm_res(act, w_ffn_out_b, l, x, 544, 512)

        gv_s = proj[MP:, CB_GV * MIX_W:(CB_GV + 1) * MIX_W]
        st_p.append((k_p.reshape(BATCH, SEQ, N_HEADS, HEAD_DIM), v_p.reshape(BATCH, SEQ, N_HEADS, HEAD_DIM),
                     tail_p[:, :IDX_DIM].reshape(BATCH, SEQ, IDX_DIM), buf_p,
                     hre_p.reshape(BATCH, SSM_GROUPS, SSM_N), him_p.reshape(BATCH, SSM_GROUPS, SSM_N)))
        st_s.append((k_s.reshape(DEC_BATCH, DEC_SEQ, N_HEADS, HEAD_DIM),
                     v_s.reshape(DEC_BATCH, DEC_SEQ, N_HEADS, HEAD_DIM),
                     tail_s[:, :IDX_DIM].reshape(DEC_BATCH, DEC_SEQ, IDX_DIM), buf_s,
                     nre_s.reshape(DEC_BATCH, SSM_GROUPS, SSM_N), nim_s.reshape(DEC_BATCH, SSM_GROUPS, SSM_N),
                     gv_s.reshape(DEC_BATCH, DEC_SEQ, MIX_W)))

    y_p = _final_norm(x, norm_final[None, :], 0, MP, MS)
    y_s = _final_norm(x, norm_final[None, :], MP, MS, MS)
    outs = [y_p.reshape(BATCH, SEQ, D_MODEL), y_s.reshape(DEC_BATCH, DEC_SEQ, D_MODEL)]
    outs += [jnp.stack([s[i] for s in st_p]) for i in range(6)]
    outs += [jnp.stack([s[i] for s in st_s]) for i in range(7)]
    return tuple(outs)
```

```python
import functools

import jax
import jax.numpy as jnp
from jax import lax
from jax.experimental import pallas as pl
from jax.experimental.pallas import tpu as pltpu

F32 = jnp.float32
BF16 = jnp.bfloat16
I32 = jnp.int32

D_MODEL = 2048
BATCH = 4
SEQ = 2048
DEPTH = 2
DEC_BATCH = 128
DEC_SEQ = 4
PAGE_SIZE = 128
N_PAGES = 16
PAST = N_PAGES * PAGE_SIZE
MIX_W = 512
N_HEADS = 4
HEAD_DIM = 128
ROT_DIM = 32
IDX_HEADS = 8
IDX_DIM = 64
IDX_ROT = 16
TOPK = 256
QBLOCK = 128
ROPE_THETA = 500000.0
SSM_P = 16
SSM_GROUPS = 32
SSM_N = 64
SSM_STATE = SSM_GROUPS * SSM_N
GMLP_CHUNK = 128
GMLP_GROUPS = 4
D_FF = 5632
N_BRANCH = 4

MP = BATCH * SEQ
MS = DEC_BATCH * DEC_SEQ
M = MP + MS
LANES = 128
TAIL_COL = 10 * MIX_W
N_MAIN = TAIL_COL + 2 * LANES
CB_Q, CB_K, CB_V, CB_QI, CB_CX, CB_CB, CB_CC, CB_SU, CB_GU, CB_GV = range(10)
SAMPLE_KEYS = PAST + LANES
NEG_INF = float("-inf")
INT_MIN = -2 ** 31
VMEM_LIMIT = 56 * 1024 * 1024


def _params(n_axes, vmem=VMEM_LIMIT):
    return pltpu.CompilerParams(dimension_semantics=("arbitrary",) * n_axes, vmem_limit_bytes=vmem)


def _rms(x, g):
    return x * lax.rsqrt(jnp.mean(x * x, axis=-1, keepdims=True) + 1e-6) * g


def _dot(a, b):
    return jnp.dot(a, b, preferred_element_type=F32)


def _dot_nt(a, b):
    return lax.dot_general(a, b, (((1,), (1,)), ((), ())), preferred_element_type=F32)


def _sigmoid(x):
    return 1.0 / (1.0 + jnp.exp(-x))


def _norm_mm_kernel(x_ref, g_ref, w_ref, o_ref, h_ref):
    @pl.when(pl.program_id(1) == 0)
    def _():
        h_ref[...] = _rms(x_ref[...], g_ref[...]).astype(BF16)

    o_ref[...] = _dot(h_ref[...], w_ref[...])


def _norm_mm(x, g, w, tm, tn):
    m, k = x.shape
    n = w.shape[1]
    return pl.pallas_call(
        _norm_mm_kernel,
        grid=(m // tm, n // tn),
        in_specs=[pl.BlockSpec((tm, k), lambda i, j: (i, 0)),
                  pl.BlockSpec((1, k), lambda i, j: (0, 0)),
                  pl.BlockSpec((k, tn), lambda i, j: (0, j))],
        out_specs=pl.BlockSpec((tm, tn), lambda i, j: (i, j)),
        out_shape=jax.ShapeDtypeStruct((m, n), F32),
        scratch_shapes=[pltpu.VMEM((tm, k), BF16)],
        compiler_params=_params(2),
        name="norm_mm",
    )(x, g, w)


def _gate_merge_kernel(x_ref, g_ref, wg0, wg1, wg2, wg3, yp0, yp1, yp2, yp3, ys0, ys1, ys2, ys3, wb_ref,
                       o_ref, h_ref):
    @pl.when(pl.program_id(1) == 0)
    def _():
        h_ref[...] = _rms(x_ref[...], g_ref[...]).astype(BF16)

    h = h_ref[...]
    is_prompt = pl.program_id(0) < MP // MS
    acc = None
    for kk, (wg, yp, ys) in enumerate(((wg0, yp0, ys0), (wg1, yp1, ys1), (wg2, yp2, ys2), (wg3, yp3, ys3))):
        gate = _sigmoid(_dot(h, wg[...]))
        y = jnp.where(is_prompt, yp[...], ys[...])
        term = gate * _dot(y, wb_ref[0, kk].astype(BF16))
        acc = term if acc is None else acc + term
    o_ref[...] = acc.astype(BF16)


def _gate_merge(x, g, wg, ys_prompt, ys_sample, wb, layer, tn):
    tm = MS
    nj = D_MODEL // tn
    wg_specs = [pl.BlockSpec((D_MODEL, tn), functools.partial(lambda i, j, kk: (0, kk * nj + j), kk=kk))
                for kk in range(N_BRANCH)]
    yp_specs = [pl.BlockSpec((tm, MIX_W), lambda i, j: (jnp.minimum(i, MP // MS - 1), 0)) for _ in range(N_BRANCH)]
    ys_specs = [pl.BlockSpec((tm, MIX_W), lambda i, j: (0, 0)) for _ in range(N_BRANCH)]
    return pl.pallas_call(
        _gate_merge_kernel,
        grid=(M // tm, nj),
        in_specs=[pl.BlockSpec((tm, D_MODEL), lambda i, j: (i, 0)),
                  pl.BlockSpec((1, D_MODEL), lambda i, j: (0, 0))] + wg_specs + yp_specs + ys_specs
                 + [pl.BlockSpec((1, N_BRANCH, MIX_W, tn), lambda i, j: (layer, 0, 0, j))],
        out_specs=pl.BlockSpec((tm, tn), lambda i, j: (i, j)),
        out_shape=jax.ShapeDtypeStruct((M, D_MODEL), BF16),
        scratch_shapes=[pltpu.VMEM((tm, D_MODEL), BF16)],
        compiler_params=_params(2),
        name="gate_merge",
    )(x, g, wg, wg, wg, wg, *ys_prompt, *ys_sample, wb)


def _mm_res_kernel(a_ref, w_ref, r_ref, o_ref):
    o_ref[...] = r_ref[...] + _dot(a_ref[...], w_ref[0].astype(BF16))


def _mm_res(a, w, layer, res, tm, tn):
    m, k = a.shape
    n = w.shape[2]
    return pl.pallas_call(
        _mm_res_kernel,
        grid=(m // tm, n // tn),
        in_specs=[pl.BlockSpec((tm, k), lambda i, j: (i, 0)),
                  pl.BlockSpec((1, k, tn), lambda i, j: (layer, 0, j)),
                  pl.BlockSpec((tm, tn), lambda i, j: (i, j))],
        out_specs=pl.BlockSpec((tm, tn), lambda i, j: (i, j)),
        out_shape=jax.ShapeDtypeStruct((m, n), F32),
        compiler_params=_params(2),
        name="mm_res",
    )(a, w, res)


def _swiglu_kernel(x_ref, g_ref, wa_ref, wb_ref, o_ref, h_ref):
    @pl.when(pl.program_id(1) == 0)
    def _():
        h_ref[...] = _rms(x_ref[...], g_ref[...]).astype(BF16)

    h = h_ref[...]
    a = _dot(h, wa_ref[0].astype(BF16))
    b = _dot(h, wb_ref[0].astype(BF16))
    o_ref[...] = (a * _sigmoid(a) * b).astype(BF16)


def _swiglu(x, g, w, layer, tm, tn):
    nj = D_FF // tn
    return pl.pallas_call(
        _swiglu_kernel,
        grid=(M // tm, nj),
        in_specs=[pl.BlockSpec((tm, D_MODEL), lambda i, j: (i, 0)),
                  pl.BlockSpec((1, D_MODEL), lambda i, j: (0, 0)),
                  pl.BlockSpec((1, D_MODEL, tn), lambda i, j: (layer, 0, j)),
                  pl.BlockSpec((1, D_MODEL, tn), lambda i, j: (layer, 0, nj + j))],
        out_specs=pl.BlockSpec((tm, tn), lambda i, j: (i, j)),
        out_shape=jax.ShapeDtypeStruct((M, D_FF), BF16),
        scratch_shapes=[pltpu.VMEM((tm, D_MODEL), BF16)],
        compiler_params=_params(2),
        name="swiglu",
    )(x, g, w, w)


def _final_norm_kernel(x_ref, g_ref, o_ref):
    o_ref[...] = _rms(x_ref[...], g_ref[...])


def _final_norm(x, g, row0, rows, tm):
    return pl.pallas_call(
        _final_norm_kernel,
        grid=(rows // tm,),
        in_specs=[pl.BlockSpec((tm, D_MODEL), lambda i: (row0 // tm + i, 0)),
                  pl.BlockSpec((1, D_MODEL), lambda i: (0, 0))],
        out_specs=pl.BlockSpec((tm, D_MODEL), lambda i: (i, 0)),
        out_shape=jax.ShapeDtypeStruct((rows, D_MODEL), F32),
        compiler_params=_params(1),
        name="final_norm",
    )(x, g)


PREP_TM = 256


def _prep_kernel(q_ref, k_ref, v_ref, qi_ref, t_ref, tab_ref, qo, ko, kbo, vo, vbo, qio, to, ki2o):
    lane = lax.broadcasted_iota(I32, (PREP_TM, LANES), 1)

    def rope(x, table, half):
        c, s = tab_ref[2 * table], tab_ref[2 * table + 1]
        upper = (lane & half) != 0
        partner = jnp.where(upper, pltpu.roll(x, half, 1), -pltpu.roll(x, LANES - half, 1))
        return x * c + partner * s

    for h in range(N_HEADS):
        sl = slice(h * LANES, (h + 1) * LANES)
        head_rows = pl.ds(h, PREP_TM, stride=N_HEADS)
        qo[:, sl] = rope(q_ref[:, sl], 0, ROT_DIM // 2).astype(BF16)
        kr = rope(k_ref[:, sl], 0, ROT_DIM // 2)
        ko[head_rows, :] = kr
        kbo[:, sl] = kr.astype(BF16)
        v = v_ref[:, sl]
        vo[head_rows, :] = v
        vbo[:, sl] = v.astype(BF16)
        qio[:, sl] = rope(qi_ref[:, sl], 1, IDX_ROT // 2).astype(BF16)
    tr = rope(t_ref[...], 2, IDX_ROT // 2)
    to[...] = tr
    ki = jnp.where(lane < IDX_DIM, tr, 0.0)
    ki2o[:, :LANES] = ki.astype(BF16)
    ki2o[:, LANES:] = pltpu.roll(ki, IDX_DIM, 1).astype(BF16)


def _prep(proj, tables, row0, rows, table_block):
    b0 = row0 // PREP_TM

    def cb(c):
        return pl.BlockSpec((PREP_TM, MIX_W), lambda i: (b0 + i, c))

    row512 = pl.BlockSpec((PREP_TM, MIX_W), lambda i: (i, 0))
    head_rows = pl.BlockSpec((PREP_TM * N_HEADS, HEAD_DIM), lambda i: (i, 0))
    return pl.pallas_call(
        _prep_kernel,
        grid=(rows // PREP_TM,),
        in_specs=[cb(CB_Q), cb(CB_K), cb(CB_V), cb(CB_QI),
                  pl.BlockSpec((PREP_TM, LANES), lambda i: (b0 + i, TAIL_COL // LANES)),
                  pl.BlockSpec((6, PREP_TM, LANES), lambda i: (0, table_block(i), 0))],
        out_specs=[row512, head_rows, row512, head_rows, row512, row512,
                   pl.BlockSpec((PREP_TM, LANES), lambda i: (i, 0)),
                   pl.BlockSpec((PREP_TM, 2 * LANES), lambda i: (i, 0))],
        out_shape=[jax.ShapeDtypeStruct((rows, MIX_W), BF16),
                   jax.ShapeDtypeStruct((rows * N_HEADS, HEAD_DIM), F32),
                   jax.ShapeDtypeStruct((rows, MIX_W), BF16),
                   jax.ShapeDtypeStruct((rows * N_HEADS, HEAD_DIM), F32),
                   jax.ShapeDtypeStruct((rows, MIX_W), BF16),
                   jax.ShapeDtypeStruct((rows, MIX_W), BF16),
                   jax.ShapeDtypeStruct((rows, LANES), F32),
                   jax.ShapeDtypeStruct((rows, 2 * LANES), BF16)],
        compiler_params=_params(1),
        name="prep",
    )(proj, proj, proj, proj, proj, tables)


KEY_NEG_INF = INT_MIN + 0x7FFFFF


def _key_to_float(key):
    return lax.bitcast_convert_type(jnp.where(key < 0, key ^ 0x7FFFFFFF, key), F32)


def _select_chunks(sc_ref, bias_ref, nc, k, allowed_fn):
    _, rows, cw = sc_ref.shape
    kf = float(k)

    def count_ge(c):
        acc = jnp.where(sc_ref[0] >= c, 1.0, 0.0)
        for j in range(1, nc):
            acc = acc + jnp.where(sc_ref[j] >= c, 1.0, 0.0)
        return jnp.sum(acc, axis=-1, keepdims=True)

    t = jnp.where(count_ge(jnp.zeros((rows, 1), F32)) >= kf, 0, INT_MIN).astype(I32)
    for bit in range(30, -1, -1):
        cand = t + (1 << bit)
        ok = jnp.logical_or(count_ge(_key_to_float(cand)) >= kf, cand <= KEY_NEG_INF)
        t = jnp.where(ok, cand, t)
    lo = _key_to_float(t)
    hi = _key_to_float(t + 1)
    need = kf - count_ge(hi)
    r_io = lax.broadcasted_iota(I32, (LANES, LANES), 0)
    c_io = lax.broadcasted_iota(I32, (LANES, LANES), 1)
    tri = jnp.where(r_io <= c_io, 1.0, 0.0).astype(BF16)

    off = jnp.zeros((rows, 1), F32)
    for j in range(nc):
        for u in range(cw // LANES):
            sl = slice(u * LANES, (u + 1) * LANES)
            su = sc_ref[j, :, sl]
            above = su >= hi
            e = jnp.where(above, 0.0, jnp.where(su >= lo, 1.0, 0.0))
            rank = _dot(e.astype(BF16), tri) + off
            off = off + jnp.sum(e, axis=-1, keepdims=True)
            take = jnp.where(above, 1.0, jnp.where(rank <= need, e, 0.0))
            take = jnp.where(allowed_fn(j, u, su), take, 0.0)
            bias_ref[j, :, sl] = jnp.where(take > 0.5, 0.0, NEG_INF)


def _index_scores(qi, w_scaled, ki_a, ki_b):
    acc = None
    for p in range(IDX_HEADS // 2):
        qp = qi[:, p * LANES:(p + 1) * LANES]
        for half, ki in enumerate((ki_a, ki_b)):
            hh = 2 * p + half
            d = _dot_nt(qp, ki) * (IDX_DIM ** -0.5)
            term = jnp.maximum(d, 0.0) * w_scaled[:, IDX_DIM + hh:IDX_DIM + hh + 1]
            acc = term if acc is None else acc + term
    return acc


KEY_CHUNK = 256
N_KEY_CHUNKS = SEQ // KEY_CHUNK
CAUSAL_VARIANTS = 4


def _dsa_prompt_kernel(q_ref, qi_ref, t_ref, k_ref, v_ref, ki2_ref, o_ref, sc_ref, bias_ref):
    i = pl.program_id(1)
    row = i * QBLOCK + lax.broadcasted_iota(I32, (QBLOCK, KEY_CHUNK), 0)
    lane = lax.broadcasted_iota(I32, (QBLOCK, KEY_CHUNK), 1)
    row_g = i * QBLOCK + lax.broadcasted_iota(I32, (QBLOCK, LANES), 0)
    lane_g = lax.broadcasted_iota(I32, (QBLOCK, LANES), 1)

    def causal(j, u, su):
        return j * KEY_CHUNK + u * LANES + lane_g <= row_g

    def attend(n_keys):
        nc = n_keys // KEY_CHUNK
        w_scaled = t_ref[...] * (IDX_HEADS ** -0.5)
        sc = _index_scores(qi_ref[...], w_scaled, ki2_ref[:n_keys, :LANES], ki2_ref[:n_keys, LANES:])
        for j in range(nc):
            sc_ref[j] = jnp.where(j * KEY_CHUNK + lane <= row, sc[:, j * KEY_CHUNK:(j + 1) * KEY_CHUNK], NEG_INF)
        _select_chunks(sc_ref.at[:nc], bias_ref.at[:nc], nc, TOPK, causal)
        bias = jnp.concatenate([bias_ref[j] for j in range(nc)], axis=-1)
        for h in range(N_HEADS):
            sl = slice(h * HEAD_DIM, (h + 1) * HEAD_DIM)
            s = _dot_nt(q_ref[:, sl], k_ref[:n_keys, sl]) * (HEAD_DIM ** -0.5) + bias
            m = jnp.max(s, axis=-1, keepdims=True)
            p = jnp.exp(s - m)
            l = jnp.sum(p, axis=-1, keepdims=True)
            o_ref[:, sl] = (_dot(p.astype(BF16), v_ref[:n_keys, sl]) / l).astype(BF16)

    blocks_per_variant = (SEQ // QBLOCK) // CAUSAL_VARIANTS
    for v in range(CAUSAL_VARIANTS):
        @pl.when(i // blocks_per_variant == v)
        def _(v=v):
            attend((v + 1) * (SEQ // CAUSAL_VARIANTS))


def _dsa_prompt(q, qi, tail, kb, vb, ki2):
    nb = SEQ // QBLOCK

    def qspec(w):
        return pl.BlockSpec((QBLOCK, w), lambda b, i: (b * nb + i, 0))

    def kspec(w):
        return pl.BlockSpec((SEQ, w), lambda b, i: (b, 0))

    return pl.pallas_call(
        _dsa_prompt_kernel,
        grid=(BATCH, nb),
        in_specs=[qspec(MIX_W), qspec(MIX_W), qspec(LANES), kspec(MIX_W), kspec(MIX_W), kspec(2 * LANES)],
        out_specs=qspec(MIX_W),
        out_shape=jax.ShapeDtypeStruct((MP, MIX_W), BF16),
        scratch_shapes=[pltpu.VMEM((N_KEY_CHUNKS, QBLOCK, KEY_CHUNK), F32)] * 2,
        compiler_params=_params(2),
        name="dsa_prompt",
    )(q, qi, tail, kb, vb, ki2)


def _dsa_sample_scores_kernel(pt_ref, qi_ref, w_ref, *rest):
    page_refs, knew_ref, o_ref = rest[:N_PAGES], rest[N_PAGES], rest[N_PAGES + 1]
    qi = qi_ref[0]
    w = w_ref[0] * (IDX_HEADS ** -0.5)

    def chunk_scores(d):
        r = jnp.maximum(d * (IDX_DIM ** -0.5), 0.0) * w
        return jnp.sum(r.reshape(DEC_SEQ, IDX_HEADS, LANES), axis=1)

    for p in range(N_PAGES):
        o_ref[0, :, p * LANES:(p + 1) * LANES] = chunk_scores(_dot(qi, page_refs[p][0, 0].astype(BF16)))
    new = chunk_scores(_dot_nt(qi, knew_ref[0]))
    tq = lax.broadcasted_iota(I32, (DEC_SEQ, LANES), 0)
    jk = lax.broadcasted_iota(I32, (DEC_SEQ, LANES), 1)
    o_ref[0, :, PAST:] = jnp.where(jk <= tq, new, NEG_INF)


def _dsa_sample_scores(page_table, qi32, w32, cache_kidx_t, layer, knew_pad):
    page_specs = [pl.BlockSpec((1, 1, IDX_DIM, PAGE_SIZE),
                               functools.partial(lambda b, pt, p: (layer, pt[b, p], 0, 0), p=p))
                  for p in range(N_PAGES)]
    grid_spec = pltpu.PrefetchScalarGridSpec(
        num_scalar_prefetch=1,
        grid=(DEC_BATCH,),
        in_specs=[pl.BlockSpec((1, DEC_SEQ * IDX_HEADS, IDX_DIM), lambda b, pt: (b, 0, 0)),
                  pl.BlockSpec((1, DEC_SEQ * IDX_HEADS, 1), lambda b, pt: (b, 0, 0))] + page_specs
                 + [pl.BlockSpec((1, LANES, IDX_DIM), lambda b, pt: (b, 0, 0))],
        out_specs=pl.BlockSpec((1, DEC_SEQ, SAMPLE_KEYS), lambda b, pt: (b, 0, 0)),
    )
    return pl.pallas_call(
        _dsa_sample_scores_kernel,
        grid_spec=grid_spec,
        out_shape=jax.ShapeDtypeStruct((DEC_BATCH, DEC_SEQ, SAMPLE_KEYS), F32),
        compiler_params=_params(1),
        name="dsa_sample_scores",
    )(page_table, qi32, w32, *([cache_kidx_t] * N_PAGES), knew_pad)


SEL_ROWS = 128
N_SAMPLE_CHUNKS = SAMPLE_KEYS // LANES


def _sample_select_kernel(in_ref, o_ref, sc_ref, bias_ref):
    for c in range(N_SAMPLE_CHUNKS):
        sc_ref[c] = in_ref[:, c * LANES:(c + 1) * LANES]
    _select_chunks(sc_ref, bias_ref, N_SAMPLE_CHUNKS, TOPK, lambda j, u, su: su > NEG_INF)
    for c in range(N_SAMPLE_CHUNKS):
        o_ref[:, c * LANES:(c + 1) * LANES] = bias_ref[c]


def _sample_select(sc):
    return pl.pallas_call(
        _sample_select_kernel,
        grid=(MS // SEL_ROWS,),
        in_specs=[pl.BlockSpec((SEL_ROWS, SAMPLE_KEYS), lambda i: (i, 0))],
        out_specs=pl.BlockSpec((SEL_ROWS, SAMPLE_KEYS), lambda i: (i, 0)),
        out_shape=jax.ShapeDtypeStruct((MS, SAMPLE_KEYS), F32),
        scratch_shapes=[pltpu.VMEM((N_SAMPLE_CHUNKS, SEL_ROWS, LANES), F32)] * 2,
        compiler_params=_params(1),
        name="sample_select",
    )(sc)


def _dsa_sample_attn_kernel(pt_ref, q_ref, bias_ref, *rest):
    k_refs = rest[:N_PAGES]
    v_refs = rest[N_PAGES:2 * N_PAGES]
    knew_ref, vnew_ref, o_ref, s_ref = rest[2 * N_PAGES:]
    q = q_ref[0]
    def head_rows(h):
        return pl.ds(h, PAGE_SIZE, stride=N_HEADS)

    for h in range(N_HEADS):
        rs = slice(h * DEC_SEQ, (h + 1) * DEC_SEQ)
        hs = slice(h * HEAD_DIM, (h + 1) * HEAD_DIM)
        qh = q[:, hs]
        for p in range(N_PAGES):
            s_ref[rs, p * LANES:(p + 1) * LANES] = _dot_nt(qh, k_refs[p][0, 0, head_rows(h), :].astype(BF16))
        s_ref[rs, PAST:] = _dot_nt(qh, knew_ref[0, :, hs])
    bias = bias_ref[0]
    s = s_ref[...] * (HEAD_DIM ** -0.5) + jnp.concatenate([bias] * N_HEADS, axis=0)
    m = jnp.max(s, axis=-1, keepdims=True)
    pr = jnp.exp(s - m)
    l = jnp.sum(pr, axis=-1, keepdims=True)
    pb = pr.astype(BF16)
    for h in range(N_HEADS):
        rs = slice(h * DEC_SEQ, (h + 1) * DEC_SEQ)
        hs = slice(h * HEAD_DIM, (h + 1) * HEAD_DIM)
        acc = _dot(pb[rs, PAST:], vnew_ref[0, :, hs])
        for p in range(N_PAGES):
            acc = acc + _dot(pb[rs, p * LANES:(p + 1) * LANES], v_refs[p][0, 0, head_rows(h), :].astype(BF16))
        o_ref[0, :, hs] = (acc / l[rs]).astype(BF16)


def _dsa_sample_attn(page_table, q, bias, cache_k, cache_v, layer, knew_pad, vnew_pad):
    def page_spec(p):
        return pl.BlockSpec((1, 1, PAGE_SIZE * N_HEADS, HEAD_DIM),
                            functools.partial(lambda b, pt, p: (layer, pt[b, p], 0, 0), p=p))

    grid_spec = pltpu.PrefetchScalarGridSpec(
        num_scalar_prefetch=1,
        grid=(DEC_BATCH,),
        in_specs=[pl.BlockSpec((1, DEC_SEQ, MIX_W), lambda b, pt: (b, 0, 0)),
                  pl.BlockSpec((1, DEC_SEQ, SAMPLE_KEYS), lambda b, pt: (b, 0, 0))]
                 + [page_spec(p) for p in range(N_PAGES)] + [page_spec(p) for p in range(N_PAGES)]
                 + [pl.BlockSpec((1, LANES, MIX_W), lambda b, pt: (b, 0, 0)),
                    pl.BlockSpec((1, LANES, MIX_W), lambda b, pt: (b, 0, 0))],
        out_specs=pl.BlockSpec((1, DEC_SEQ, MIX_W), lambda b, pt: (b, 0, 0)),
        scratch_shapes=[pltpu.VMEM((N_HEADS * DEC_SEQ, SAMPLE_KEYS), F32)],
    )
    return pl.pallas_call(
        _dsa_sample_attn_kernel,
        grid_spec=grid_spec,
        out_shape=jax.ShapeDtypeStruct((DEC_BATCH, DEC_SEQ, MIX_W), BF16),
        compiler_params=_params(1),
        name="dsa_sample_attn",
    )(page_table, q, bias, *([cache_k] * N_PAGES), *([cache_v] * N_PAGES), knew_pad, vnew_pad)


CONV_TT = 512


def _conv_prompt_kernel(cx_ref, cb_ref, cc_ref, w_ref, yb_ref, buf_ref, carry_ref):
    j = pl.program_id(1)

    @pl.when(j == 0)
    def _():
        carry_ref[...] = jnp.zeros_like(carry_ref)

    z = cc_ref[...] * cx_ref[...]
    row = lax.broadcasted_iota(I32, z.shape, 0)
    c0 = carry_ref[0:1, :]
    c1 = carry_ref[1:2, :]
    zm1 = jnp.where(row == 0, c1, pltpu.roll(z, 1, 0))
    zm2 = jnp.where(row == 0, c0, jnp.where(row == 1, c1, pltpu.roll(z, 2, 0)))
    zc = w_ref[0:1, :] * zm2 + w_ref[1:2, :] * zm1 + w_ref[2:3, :] * z
    yb_ref[...] = (cb_ref[...] * zc).astype(BF16)
    last = z[CONV_TT - 2:CONV_TT, :]
    carry_ref[0:2, :] = last
    buf_ref[0] = last


def _conv_prompt(proj, conv_w):
    nt = SEQ // CONV_TT

    def cb(c):
        return pl.BlockSpec((CONV_TT, MIX_W), lambda b, j: (b * nt + j, c))

    return pl.pallas_call(
        _conv_prompt_kernel,
        grid=(BATCH, nt),
        in_specs=[cb(CB_CX), cb(CB_CB), cb(CB_CC), pl.BlockSpec((3, MIX_W), lambda b, j: (0, 0))],
        out_specs=[pl.BlockSpec((CONV_TT, MIX_W), lambda b, j: (b * nt + j, 0)),
                   pl.BlockSpec((1, 2, MIX_W), lambda b, j: (b, 0, 0))],
        out_shape=[jax.ShapeDtypeStruct((MP, MIX_W), BF16),
                   jax.ShapeDtypeStruct((BATCH, 2, MIX_W), F32)],
        scratch_shapes=[pltpu.VMEM((8, MIX_W), F32)],
        compiler_params=_params(2),
        name="conv_prompt",
    )(proj, proj, proj, conv_w)


def _gmlp_prompt_kernel(gu_ref, gv_ref, w_ref, bt_ref, o_ref):
    r_io = lax.broadcasted_iota(I32, (GMLP_CHUNK, GMLP_CHUNK), 0)
    c_io = lax.broadcasted_iota(I32, (GMLP_CHUNK, GMLP_CHUNK), 1)
    tril = c_io <= r_io
    for g in range(GMLP_GROUPS):
        sl = slice(g * LANES, (g + 1) * LANES)
        wm = jnp.where(tril, w_ref[g], 0.0).astype(BF16)
        z = _dot(wm, gv_ref[:, sl].astype(BF16)) + bt_ref[:, g:g + 1]
        o_ref[:, sl] = (gu_ref[:, sl] * z).astype(BF16)


def _gmlp_prompt(proj, gmlp_w, gmlp_bt):
    def cb(c):
        return pl.BlockSpec((GMLP_CHUNK, MIX_W), lambda i: (i, c))

    return pl.pallas_call(
        _gmlp_prompt_kernel,
        grid=(MP // GMLP_CHUNK,),
        in_specs=[cb(CB_GU), cb(CB_GV),
                  pl.BlockSpec((GMLP_GROUPS, GMLP_CHUNK, GMLP_CHUNK), lambda i: (0, 0, 0)),
                  pl.BlockSpec((GMLP_CHUNK, GMLP_GROUPS), lambda i: (0, 0))],
        out_specs=pl.BlockSpec((GMLP_CHUNK, MIX_W), lambda i: (i, 0)),
        out_shape=jax.ShapeDtypeStruct((MP, MIX_W), BF16),
        compiler_params=_params(1),
        name="gmlp_prompt",
    )(proj, proj, gmlp_w, gmlp_bt)


def _mix_sample_kernel(cx_ref, cb_ref, cc_ref, gu_ref, gv_ref, buf_ref, cw_ref, gw_ref, gb_ref,
                       yb_ref, nbuf_ref, yd_ref):
    zf = [buf_ref[:, 0, :], buf_ref[:, 1, :]]
    for t in range(DEC_SEQ):
        zf.append(cc_ref[:, t, :] * cx_ref[:, t, :])
    for t in range(DEC_SEQ):
        zc = cw_ref[0:1, :] * zf[t] + cw_ref[1:2, :] * zf[t + 1] + cw_ref[2:3, :] * zf[t + 2]
        yb_ref[:, t, :] = (cb_ref[:, t, :] * zc).astype(BF16)
    nbuf_ref[:, 0, :] = zf[DEC_SEQ]
    nbuf_ref[:, 1, :] = zf[DEC_SEQ + 1]
    v = [gv_ref[:, s, :] for s in range(DEC_SEQ)]
    for t in range(DEC_SEQ):
        z = gb_ref[t:t + 1, :]
        for s in range(t + 1):
            z = z + gw_ref[t * DEC_SEQ + s:t * DEC_SEQ + s + 1, :] * v[s]
        yd_ref[:, t, :] = (gu_ref[:, t, :] * z).astype(BF16)


def _mix_sample(proj_s, state_conv_l, conv_w, gw_lane, gb_lane):
    def cb(c):
        return pl.BlockSpec((DEC_BATCH, DEC_SEQ, MIX_W), lambda i: (0, 0, c))

    full3 = pl.BlockSpec((DEC_BATCH, DEC_SEQ, MIX_W), lambda i: (0, 0, 0))
    buf3 = pl.BlockSpec((DEC_BATCH, 2, MIX_W), lambda i: (0, 0, 0))
    return pl.pallas_call(
        _mix_sample_kernel,
        grid=(1,),
        in_specs=[cb(CB_CX), cb(CB_CB), cb(CB_CC), cb(CB_GU), cb(CB_GV), buf3,
                  pl.BlockSpec((3, MIX_W), lambda i: (0, 0)),
                  pl.BlockSpec((DEC_SEQ * DEC_SEQ, MIX_W), lambda i: (0, 0)),
                  pl.BlockSpec((DEC_SEQ, MIX_W), lambda i: (0, 0))],
        out_specs=[full3, buf3, full3],
        out_shape=[jax.ShapeDtypeStruct((DEC_BATCH, DEC_SEQ, MIX_W), BF16),
                   jax.ShapeDtypeStruct((DEC_BATCH, 2, MIX_W), F32),
                   jax.ShapeDtypeStruct((DEC_BATCH, DEC_SEQ, MIX_W), BF16)],
        compiler_params=_params(1),
        name="mix_sample",
    )(proj_s, proj_s, proj_s, proj_s, proj_s, state_conv_l, conv_w, gw_lane, gb_lane)


def _ssm_params_kernel(ar_ref, ai_ref, ldt_ref, br_ref, bi_ref, ab_ref, bbr_ref, bbi_ref):
    ar, ai = ar_ref[...], ai_ref[...]
    dt = jnp.exp(ldt_ref[...])
    mag = jnp.exp(dt * ar)
    abar_re, abar_im = mag * jnp.cos(dt * ai), mag * jnp.sin(dt * ai)
    den = ar * ar + ai * ai
    nr, ni = abar_re - 1.0, abar_im
    coef_re = (nr * ar + ni * ai) / den
    coef_im = (ni * ar - nr * ai) / den
    ab_ref[...] = jnp.zeros_like(ab_ref)
    ab_ref[0:1, :] = abar_re
    ab_ref[1:2, :] = abar_im
    br, bi = br_ref[...], bi_ref[...]
    bbr_ref[...] = (coef_re * br - coef_im * bi).astype(BF16)
    bbi_ref[...] = (coef_re * bi + coef_im * br).astype(BF16)


def _ssm_params(a_re, a_im, ldt, bd_re, bd_im):
    row = pl.BlockSpec((1, SSM_STATE), lambda i: (0, 0))
    mat = pl.BlockSpec((MIX_W, SSM_STATE), lambda i: (0, 0))
    return pl.pallas_call(
        _ssm_params_kernel,
        grid=(1,),
        in_specs=[row, row, row, mat, mat],
        out_specs=[pl.BlockSpec((8, SSM_STATE), lambda i: (0, 0)), mat, mat],
        out_shape=[jax.ShapeDtypeStruct((8, SSM_STATE), F32),
                   jax.ShapeDtypeStruct((MIX_W, SSM_STATE), BF16),
                   jax.ShapeDtypeStruct((MIX_W, SSM_STATE), BF16)],
        compiler_params=_params(1),
        name="ssm_params",
    )(a_re, a_im, ldt, bd_re, bd_im)


def _gelu_tanh(x):
    return 0.5 * x * (1.0 + jnp.tanh(0.7978845608028654 * (x + 0.044715 * (x * x * x))))


SSM_BLOCK_STATES = SSM_STATE // (MIX_W // LANES)


def _ssm_drive(ub, bb_ref):
    return jnp.concatenate(
        [_dot(ub[:, c * LANES:(c + 1) * LANES],
              bb_ref[c * LANES:(c + 1) * LANES, c * SSM_BLOCK_STATES:(c + 1) * SSM_BLOCK_STATES])
         for c in range(MIX_W // LANES)], axis=-1)


def _ssm_readout(u, hr, hi, cr_ref, ci_ref, d_ref, wg_ref, bg_ref):
    hrb, hib = hr.astype(BF16), hi.astype(BF16)
    parts = []
    for c in range(MIX_W // LANES):
        ss = slice(c * SSM_BLOCK_STATES, (c + 1) * SSM_BLOCK_STATES)
        cs = slice(c * LANES, (c + 1) * LANES)
        parts.append(_dot(hrb[:, ss], cr_ref[ss, cs]) - _dot(hib[:, ss], ci_ref[ss, cs]))
    y = jnp.concatenate(parts, axis=-1) + d_ref[...] * u
    g = _gelu_tanh(y)
    return g * _sigmoid(_dot(g.astype(BF16), wg_ref[...]) + bg_ref[...])


SSM_TC = 128
SSM_SLABS = MIX_W // LANES


def _ssm_prompt_kernel(u0, u1, u2, u3, ab_ref, bbr_ref, bbi_ref, cr_ref, ci_ref, d_ref, wg_ref, bg_ref,
                       y_ref, str_ref, sti_ref, hr_ref, hi_ref, carry_ref, il_ref):
    j = pl.program_id(0)

    @pl.when(j == 0)
    def _():
        carry_ref[...] = jnp.zeros_like(carry_ref)

    def batch_rows(b):
        return pl.ds(b, SSM_TC, stride=BATCH)

    for b, u_ref in enumerate((u0, u1, u2, u3)):
        for c in range(SSM_SLABS):
            il_ref[c, batch_rows(b), :] = u_ref[:, c * LANES:(c + 1) * LANES]
    u = jnp.concatenate([il_ref[c] for c in range(SSM_SLABS)], axis=-1)
    ub = u.astype(BF16)
    hr_ref[...] = _ssm_drive(ub, bbr_ref)
    hi_ref[...] = _ssm_drive(ub, bbi_ref)
    ar, ai = ab_ref[0:1, :], ab_ref[1:2, :]

    def step2(t2, carry):
        pr, pi = carry
        rows = pl.ds(pl.multiple_of(t2 * 2 * BATCH, 2 * BATCH), 2 * BATCH)
        br, bi = hr_ref[rows, :], hi_ref[rows, :]
        er = ar * pr - ai * pi + br[:BATCH]
        ei = ar * pi + ai * pr + bi[:BATCH]
        nr = ar * er - ai * ei + br[BATCH:]
        ni = ar * ei + ai * er + bi[BATCH:]
        hr_ref[rows, :] = jnp.concatenate([er, nr], axis=0)
        hi_ref[rows, :] = jnp.concatenate([ei, ni], axis=0)
        return nr, ni

    fr, fi = lax.fori_loop(0, SSM_TC // 2, step2, (carry_ref[0:BATCH, :], carry_ref[BATCH:2 * BATCH, :]), unroll=4)
    carry_ref[0:BATCH, :] = fr
    carry_ref[BATCH:2 * BATCH, :] = fi
    str_ref[...] = fr
    sti_ref[...] = fi
    y = _ssm_readout(u, hr_ref[...], hi_ref[...], cr_ref, ci_ref, d_ref, wg_ref, bg_ref)
    for c in range(SSM_SLABS):
        il_ref[c] = y[:, c * LANES:(c + 1) * LANES]
    for b in range(BATCH):
        for c in range(SSM_SLABS):
            y_ref[b, :, c * LANES:(c + 1) * LANES] = il_ref[c, batch_rows(b), :].astype(BF16)


def _ssm_prompt(proj, ab, bbr, bbi, cdr, cdi, d, wglu, bglu):
    nt = SEQ // SSM_TC

    def const(shape):
        return pl.BlockSpec(shape, lambda j: (0,) * len(shape))

    u_specs = [pl.BlockSpec((SSM_TC, MIX_W), functools.partial(lambda j, b: (b * nt + j, CB_SU), b=b))
               for b in range(BATCH)]
    state = pl.BlockSpec((BATCH, SSM_STATE), lambda j: (0, 0))
    return pl.pallas_call(
        _ssm_prompt_kernel,
        grid=(nt,),
        in_specs=u_specs + [const((8, SSM_STATE)), const((MIX_W, SSM_STATE)), const((MIX_W, SSM_STATE)),
                            const((SSM_STATE, MIX_W)), const((SSM_STATE, MIX_W)), const((1, MIX_W)),
                            const((MIX_W, MIX_W)), const((1, MIX_W))],
        out_specs=[pl.BlockSpec((BATCH, SSM_TC, MIX_W), lambda j: (0, j, 0)), state, state],
        out_shape=[jax.ShapeDtypeStruct((BATCH, SEQ, MIX_W), BF16),
                   jax.ShapeDtypeStruct((BATCH, SSM_STATE), F32),
                   jax.ShapeDtypeStruct((BATCH, SSM_STATE), F32)],
        scratch_shapes=[pltpu.VMEM((BATCH * SSM_TC, SSM_STATE), F32), pltpu.VMEM((BATCH * SSM_TC, SSM_STATE), F32),
                        pltpu.VMEM((2 * BATCH, SSM_STATE), F32),
                        pltpu.VMEM((SSM_SLABS, BATCH * SSM_TC, LANES), F32)],
        compiler_params=_params(1),
        name="ssm_prompt",
    )(proj, proj, proj, proj, ab, bbr, bbi, cdr, cdi, d, wglu, bglu)


def _ssm_sample_kernel(u_ref, h0r_ref, h0i_ref, ab_ref, bbr_ref, bbi_ref, cr_ref, ci_ref, d_ref, wg_ref, bg_ref,
                       y_ref, nr_ref, ni_ref):
    ar, ai = ab_ref[0:1, :], ab_ref[1:2, :]
    hr, hi = h0r_ref[...], h0i_ref[...]
    for t in range(DEC_SEQ):
        u = u_ref[:, t, :]
        ub = u.astype(BF16)
        hr, hi = (ar * hr - ai * hi + _ssm_drive(ub, bbr_ref),
                  ar * hi + ai * hr + _ssm_drive(ub, bbi_ref))
        y_ref[:, t, :] = _ssm_readout(u, hr, hi, cr_ref, ci_ref, d_ref, wg_ref, bg_ref).astype(BF16)
    nr_ref[...] = hr
    ni_ref[...] = hi


def _ssm_sample(proj_s, h0r, h0i, ab, bbr, bbi, cdr, cdi, d, wglu, bglu):
    def const(shape):
        return pl.BlockSpec(shape, lambda i: (0,) * len(shape))

    st = const((DEC_BATCH, SSM_STATE))
    return pl.pallas_call(
        _ssm_sample_kernel,
        grid=(1,),
        in_specs=[pl.BlockSpec((DEC_BATCH, DEC_SEQ, MIX_W), lambda i: (0, 0, CB_SU)), st, st,
                  const((8, SSM_STATE)), const((MIX_W, SSM_STATE)), const((MIX_W, SSM_STATE)),
                  const((SSM_STATE, MIX_W)), const((SSM_STATE, MIX_W)), const((1, MIX_W)),
                  const((MIX_W, MIX_W)), const((1, MIX_W))],
        out_specs=[const((DEC_BATCH, DEC_SEQ, MIX_W)), st, st],
        out_shape=[jax.ShapeDtypeStruct((DEC_BATCH, DEC_SEQ, MIX_W), BF16),
                   jax.ShapeDtypeStruct((DEC_BATCH, SSM_STATE), F32),
                   jax.ShapeDtypeStruct((DEC_BATCH, SSM_STATE), F32)],
        compiler_params=_params(1),
        name="ssm_sample",
    )(proj_s, h0r, h0i, ab, bbr, bbi, cdr, cdi, d, wglu, bglu)


def _rope_tables():
    lane = jnp.arange(LANES)

    def inv(half):
        return ROPE_THETA ** (-jnp.arange(half, dtype=F32) / half)

    inv_qk = jnp.where(lane < ROT_DIM, inv(ROT_DIM // 2)[lane % (ROT_DIM // 2)], 0.0)
    idx = inv(IDX_ROT // 2)[lane % (IDX_ROT // 2)]
    inv_i = jnp.where((lane % IDX_DIM) < IDX_ROT, idx, 0.0)
    inv_t = jnp.where(lane < IDX_ROT, idx, 0.0)
    pos = jnp.concatenate([jnp.arange(SEQ), PAST + jnp.arange(PREP_TM) % DEC_SEQ]).astype(F32)
    tabs = []
    for inv_row in (inv_qk, inv_i, inv_t):
        ang = pos[:, None] * inv_row[None, :].astype(F32)
        tabs += [jnp.cos(ang), jnp.sin(ang)]
    return jnp.stack(tabs)


def _pad_rows(a, rows):
    return jnp.pad(a, ((0, 0), (0, rows - a.shape[1]), (0, 0)))


def kernel(x_prompt, x_sample, cache_k, cache_v, cache_kidx, state_conv, state_ssm_re, state_ssm_im, page_table,
           norm_mix, w_in, conv_w, ssm_a_re, ssm_a_im, ssm_log_dt, ssm_b_re, ssm_b_im, ssm_c_re, ssm_c_im, ssm_d,
           w_glu, b_glu, gmlp_w, gmlp_b, w_branch, w_out, norm_ffn, w_ffn_in, w_ffn_out, norm_final):
    x = jnp.concatenate([x_prompt.reshape(MP, D_MODEL), x_sample.reshape(MS, D_MODEL)], axis=0)
    tables = _rope_tables()
    eye_g = jnp.eye(SSM_GROUPS, dtype=F32)
    cache_kidx_t = jnp.swapaxes(cache_kidx, 2, 3)
    n_pool = cache_k.shape[1]
    cache_k2 = cache_k.reshape(DEPTH, n_pool, PAGE_SIZE * N_HEADS, HEAD_DIM)
    cache_v2 = cache_v.reshape(DEPTH, n_pool, PAGE_SIZE * N_HEADS, HEAD_DIM)
    n_gate0 = w_in.shape[2] - N_BRANCH * D_MODEL
    c_ki = 3 * MIX_W + IDX_HEADS * IDX_DIM
    c_cx = c_ki + IDX_DIM + IDX_HEADS

    st_p, st_s = [], []
    for l in range(DEPTH):
        wl = w_in[l]
        w_main = jnp.concatenate(
            [wl[:, :c_ki], wl[:, c_cx:n_gate0], wl[:, c_ki:c_cx],
             jnp.zeros((D_MODEL, N_MAIN - n_gate0), F32)], axis=1).astype(BF16)
        w_gate = wl[:, n_gate0:].astype(BF16)

        proj = _norm_mm(x, norm_mix[l][None, :], w_main, 1088, 768)
        q_p, k_p, kb_p, v_p, vb_p, qi_p, tail_p, ki2_p = _prep(
            proj, tables, 0, MP, lambda i: i % (SEQ // PREP_TM))
        q_s, k_s, kb_s, v_s, vb_s, qi_s, tail_s, _ = _prep(
            proj, tables, MP, MS, lambda i: SEQ // PREP_TM)

        ya_p = _dsa_prompt(q_p, qi_p, tail_p, kb_p, vb_p, ki2_p)
        tail_s3 = tail_s.reshape(DEC_BATCH, DEC_SEQ, LANES)
        w_s = tail_s3[:, :, IDX_DIM:IDX_DIM + IDX_HEADS].reshape(DEC_BATCH, DEC_SEQ * IDX_HEADS, 1)
        kinew = _pad_rows(tail_s3[:, :, :IDX_DIM].astype(BF16), LANES)
        sc_s = _dsa_sample_scores(page_table, qi_s.reshape(DEC_BATCH, DEC_SEQ * IDX_HEADS, IDX_DIM), w_s,
                                  cache_kidx_t, l, kinew)
        bias_s = _sample_select(sc_s.reshape(MS, SAMPLE_KEYS)).reshape(DEC_BATCH, DEC_SEQ, SAMPLE_KEYS)
        knew = _pad_rows(kb_s.reshape(DEC_BATCH, DEC_SEQ, MIX_W), LANES)
        vnew = _pad_rows(vb_s.reshape(DEC_BATCH, DEC_SEQ, MIX_W), LANES)
        ya_s = _dsa_sample_attn(page_table, q_s.reshape(DEC_BATCH, DEC_SEQ, MIX_W), bias_s, cache_k2, cache_v2, l,
                                knew, vnew)

        proj_s = proj[MP:].reshape(DEC_BATCH, DEC_SEQ, N_MAIN)
        yb_p, buf_p = _conv_prompt(proj, conv_w[l])
        yd_p = _gmlp_prompt(proj, gmlp_w[l], gmlp_b[l].T)
        gw_lane = jnp.repeat(gmlp_w[l][:, :DEC_SEQ, :DEC_SEQ].transpose(1, 2, 0).reshape(DEC_SEQ * DEC_SEQ, GMLP_GROUPS),
                             LANES, axis=1)
        gb_lane = jnp.repeat(gmlp_b[l][:, :DEC_SEQ].T, LANES, axis=1)
        yb_s, buf_s, yd_s = _mix_sample(proj_s, state_conv[l], conv_w[l], gw_lane, gb_lane)

        bd_re = jnp.einsum('gnp,gh->gphn', ssm_b_re[l], eye_g).reshape(MIX_W, SSM_STATE)
        bd_im = jnp.einsum('gnp,gh->gphn', ssm_b_im[l], eye_g).reshape(MIX_W, SSM_STATE)
        cd_re = jnp.einsum('gpn,gh->gnhp', ssm_c_re[l], eye_g).reshape(SSM_STATE, MIX_W).astype(BF16)
        cd_im = jnp.einsum('gpn,gh->gnhp', ssm_c_im[l], eye_g).reshape(SSM_STATE, MIX_W).astype(BF16)
        ldt = jnp.repeat(ssm_log_dt[l], SSM_N)[None, :]
        ab, bbr, bbi = _ssm_params(ssm_a_re[l].reshape(1, SSM_STATE), ssm_a_im[l].reshape(1, SSM_STATE), ldt,
                                   bd_re, bd_im)
        ssm_consts = (ab, bbr, bbi, cd_re, cd_im, ssm_d[l][None, :], w_glu[l].astype(BF16), b_glu[l][None, :])
        yc_p, hre_p, him_p = _ssm_prompt(proj, *ssm_consts)
        yc_p = yc_p.reshape(MP, MIX_W)
        yc_s, nre_s, nim_s = _ssm_sample(proj_s, state_ssm_re[l].reshape(DEC_BATCH, SSM_STATE),
                                         state_ssm_im[l].reshape(DEC_BATCH, SSM_STATE), *ssm_consts)

        ys_sample = tuple(y.reshape(MS, MIX_W) for y in (ya_s, yb_s, yc_s, yd_s))
        merged = _gate_merge(x, norm_mix[l][None, :], w_gate, (ya_p, yb_p, yc_p, yd_p), ys_sample, w_branch, l, 512)
        x = _mm_res(merged, w_out, l, x, 1088, 512)
        act = _swiglu(x, norm_ffn[l][None, :], w_ffn_in, l, 1088, 512)
        x = _mm_res(act, w_ffn_out, l, x, 1088, 256)

        gv_s = proj[MP:, CB_GV * MIX_W:(CB_GV + 1) * MIX_W]
        st_p.append((k_p.reshape(BATCH, SEQ, N_HEADS, HEAD_DIM), v_p.reshape(BATCH, SEQ, N_HEADS, HEAD_DIM),
                     tail_p[:, :IDX_DIM].reshape(BATCH, SEQ, IDX_DIM), buf_p,
                     hre_p.reshape(BATCH, SSM_GROUPS, SSM_N), him_p.reshape(BATCH, SSM_GROUPS, SSM_N)))
        st_s.append((k_s.reshape(DEC_BATCH, DEC_SEQ, N_HEADS, HEAD_DIM),
                     v_s.reshape(DEC_BATCH, DEC_SEQ, N_HEADS, HEAD_DIM),
                     tail_s[:, :IDX_DIM].reshape(DEC_BATCH, DEC_SEQ, IDX_DIM), buf_s,
                     nre_s.reshape(DEC_BATCH, SSM_GROUPS, SSM_N), nim_s.reshape(DEC_BATCH, SSM_GROUPS, SSM_N),
                     gv_s.reshape(DEC_BATCH, DEC_SEQ, MIX_W)))

    y_p = _final_norm(x, norm_final[None, :], 0, MP, MS)
    y_s = _final_norm(x, norm_final[None, :], MP, MS, MS)
    outs = [y_p.reshape(BATCH, SEQ, D_MODEL), y_s.reshape(DEC_BATCH, DEC_SEQ, D_MODEL)]
    outs += [jnp.stack([s[i] for s in st_p]) for i in range(6)]
    outs += [jnp.stack([s[i] for s in st_s]) for i in range(7)]
    return tuple(outs)
```

```python
import functools

import jax
import jax.numpy as jnp
from jax import lax
from jax.experimental import pallas as pl
from jax.experimental.pallas import tpu as pltpu

F32 = jnp.float32
BF16 = jnp.bfloat16
I32 = jnp.int32

D_MODEL = 2048
BATCH = 4
SEQ = 2048
DEPTH = 2
DEC_BATCH = 128
DEC_SEQ = 4
PAGE_SIZE = 128
N_PAGES = 16
PAST = N_PAGES * PAGE_SIZE
MIX_W = 512
N_HEADS = 4
HEAD_DIM = 128
ROT_DIM = 32
IDX_HEADS = 8
IDX_DIM = 64
IDX_ROT = 16
TOPK = 256
QBLOCK = 128
ROPE_THETA = 500000.0
SSM_P = 16
SSM_GROUPS = 32
SSM_N = 64
SSM_STATE = SSM_GROUPS * SSM_N
GMLP_CHUNK = 128
GMLP_GROUPS = 4
D_FF = 5632
N_BRANCH = 4

MP = BATCH * SEQ
MS = DEC_BATCH * DEC_SEQ
M = MP + MS
LANES = 128
TAIL_COL = 10 * MIX_W
N_MAIN = TAIL_COL + 2 * LANES
CB_Q, CB_K, CB_V, CB_QI, CB_CX, CB_CB, CB_CC, CB_SU, CB_GU, CB_GV = range(10)
SAMPLE_KEYS = PAST + LANES
NEG_INF = float("-inf")
INT_MIN = -2 ** 31
VMEM_LIMIT = 56 * 1024 * 1024


def _params(n_axes, vmem=VMEM_LIMIT):
    return pltpu.CompilerParams(dimension_semantics=("arbitrary",) * n_axes, vmem_limit_bytes=vmem)


def _rms(x, g):
    return x * lax.rsqrt(jnp.mean(x * x, axis=-1, keepdims=True) + 1e-6) * g


def _dot(a, b):
    return jnp.dot(a, b, preferred_element_type=F32)


def _dot_nt(a, b):
    return lax.dot_general(a, b, (((1,), (1,)), ((), ())), preferred_element_type=F32)


def _sigmoid(x):
    return 1.0 / (1.0 + jnp.exp(-x))


def _norm_mm_kernel(x_ref, g_ref, w_ref, o_ref, h_ref):
    @pl.when(pl.program_id(1) == 0)
    def _():
        h_ref[...] = _rms(x_ref[...], g_ref[...]).astype(BF16)

    o_ref[...] = _dot(h_ref[...], w_ref[...])


def _norm_mm(x, g, w, tm, tn):
    m, k = x.shape
    n = w.shape[1]
    return pl.pallas_call(
        _norm_mm_kernel,
        grid=(m // tm, n // tn),
        in_specs=[pl.BlockSpec((tm, k), lambda i, j: (i, 0)),
                  pl.BlockSpec((1, k), lambda i, j: (0, 0)),
                  pl.BlockSpec((k, tn), lambda i, j: (0, j))],
        out_specs=pl.BlockSpec((tm, tn), lambda i, j: (i, j)),
        out_shape=jax.ShapeDtypeStruct((m, n), F32),
        scratch_shapes=[pltpu.VMEM((tm, k), BF16)],
        compiler_params=_params(2),
        name="norm_mm",
    )(x, g, w)


def _gate_merge_kernel(x_ref, g_ref, wg0, wg1, wg2, wg3, yp0, yp1, yp2, yp3, ys0, ys1, ys2, ys3, wb_ref,
                       o_ref, h_ref):
    @pl.when(pl.program_id(1) == 0)
    def _():
        h_ref[...] = _rms(x_ref[...], g_ref[...]).astype(BF16)

    h = h_ref[...]
    is_prompt = pl.program_id(0) < MP // MS
    acc = None
    for kk, (wg, yp, ys) in enumerate(((wg0, yp0, ys0), (wg1, yp1, ys1), (wg2, yp2, ys2), (wg3, yp3, ys3))):
        gate = _sigmoid(_dot(h, wg[...]))
        y = jnp.where(is_prompt, yp[...], ys[...])
        term = gate * _dot(y, wb_ref[0, kk].astype(BF16))
        acc = term if acc is None else acc + term
    o_ref[...] = acc.astype(BF16)


def _gate_merge(x, g, wg, ys_prompt, ys_sample, wb, layer, tn):
    tm = MS
    nj = D_MODEL // tn
    wg_specs = [pl.BlockSpec((D_MODEL, tn), functools.partial(lambda i, j, kk: (0, kk * nj + j), kk=kk))
                for kk in range(N_BRANCH)]
    yp_specs = [pl.BlockSpec((tm, MIX_W), lambda i, j: (jnp.minimum(i, MP // MS - 1), 0)) for _ in range(N_BRANCH)]
    ys_specs = [pl.BlockSpec((tm, MIX_W), lambda i, j: (0, 0)) for _ in range(N_BRANCH)]
    return pl.pallas_call(
        _gate_merge_kernel,
        grid=(M // tm, nj),
        in_specs=[pl.BlockSpec((tm, D_MODEL), lambda i, j: (i, 0)),
                  pl.BlockSpec((1, D_MODEL), lambda i, j: (0, 0))] + wg_specs + yp_specs + ys_specs
                 + [pl.BlockSpec((1, N_BRANCH, MIX_W, tn), lambda i, j: (layer, 0, 0, j))],
        out_specs=pl.BlockSpec((tm, tn), lambda i, j: (i, j)),
        out_shape=jax.ShapeDtypeStruct((M, D_MODEL), BF16),
        scratch_shapes=[pltpu.VMEM((tm, D_MODEL), BF16)],
        compiler_params=_params(2),
        name="gate_merge",
    )(x, g, wg, wg, wg, wg, *ys_prompt, *ys_sample, wb)


def _mm_res_kernel(a_ref, w_ref, r_ref, o_ref):
    o_ref[...] = r_ref[...] + _dot(a_ref[...], w_ref[0].astype(BF16))


def _mm_res(a, w, layer, res, tm, tn):
    m, k = a.shape
    n = w.shape[2]
    return pl.pallas_call(
        _mm_res_kernel,
        grid=(m // tm, n // tn),
        in_specs=[pl.BlockSpec((tm, k), lambda i, j: (i, 0)),
                  pl.BlockSpec((1, k, tn), lambda i, j: (layer, 0, j)),
                  pl.BlockSpec((tm, tn), lambda i, j: (i, j))],
        out_specs=pl.BlockSpec((tm, tn), lambda i, j: (i, j)),
        out_shape=jax.ShapeDtypeStruct((m, n), F32),
        compiler_params=_params(2),
        name="mm_res",
    )(a, w, res)


def _swiglu_kernel(x_ref, g_ref, wa_ref, wb_ref, o_ref, h_ref):
    @pl.when(pl.program_id(1) == 0)
    def _():
        h_ref[...] = _rms(x_ref[...], g_ref[...]).astype(BF16)

    h = h_ref[...]
    a = _dot(h, wa_ref[0].astype(BF16))
    b = _dot(h, wb_ref[0].astype(BF16))
    o_ref[...] = (a * _sigmoid(a) * b).astype(BF16)


def _swiglu(x, g, w, layer, tm, tn):
    nj = D_FF // tn
    return pl.pallas_call(
        _swiglu_kernel,
        grid=(M // tm, nj),
        in_specs=[pl.BlockSpec((tm, D_MODEL), lambda i, j: (i, 0)),
                  pl.BlockSpec((1, D_MODEL), lambda i, j: (0, 0)),
                  pl.BlockSpec((1, D_MODEL, tn), lambda i, j: (layer, 0, j)),
                  pl.BlockSpec((1, D_MODEL, tn), lambda i, j: (layer, 0, nj + j))],
        out_specs=pl.BlockSpec((tm, tn), lambda i, j: (i, j)),
        out_shape=jax.ShapeDtypeStruct((M, D_FF), BF16),
        scratch_shapes=[pltpu.VMEM((tm, D_MODEL), BF16)],
        compiler_params=_params(2),
        name="swiglu",
    )(x, g, w, w)


def _final_norm_kernel(x_ref, g_ref, o_ref):
    o_ref[...] = _rms(x_ref[...], g_ref[...])


def _final_norm(x, g, row0, rows, tm):
    return pl.pallas_call(
        _final_norm_kernel,
        grid=(rows // tm,),
        in_specs=[pl.BlockSpec((tm, D_MODEL), lambda i: (row0 // tm + i, 0)),
                  pl.BlockSpec((1, D_MODEL), lambda i: (0, 0))],
        out_specs=pl.BlockSpec((tm, D_MODEL), lambda i: (i, 0)),
        out_shape=jax.ShapeDtypeStruct((rows, D_MODEL), F32),
        compiler_params=_params(1),
        name="final_norm",
    )(x, g)


PREP_TM = 256


def _prep_kernel(q_ref, k_ref, v_ref, qi_ref, t_ref, tab_ref, qo, ko, kbo, vo, vbo, qio, to, ki2o):
    lane = lax.broadcasted_iota(I32, (PREP_TM, LANES), 1)

    def rope(x, table, half):
        c, s = tab_ref[2 * table], tab_ref[2 * table + 1]
        upper = (lane & half) != 0
        partner = jnp.where(upper, pltpu.roll(x, half, 1), -pltpu.roll(x, LANES - half, 1))
        return x * c + partner * s

    for h in range(N_HEADS):
        sl = slice(h * LANES, (h + 1) * LANES)
        head_rows = pl.ds(h, PREP_TM, stride=N_HEADS)
        qo[:, sl] = rope(q_ref[:, sl], 0, ROT_DIM // 2).astype(BF16)
        kr = rope(k_ref[:, sl], 0, ROT_DIM // 2)
        ko[head_rows, :] = kr
        kbo[:, sl] = kr.astype(BF16)
        v = v_ref[:, sl]
        vo[head_rows, :] = v
        vbo[:, sl] = v.astype(BF16)
        qio[:, sl] = rope(qi_ref[:, sl], 1, IDX_ROT // 2).astype(BF16)
    tr = rope(t_ref[...], 2, IDX_ROT // 2)
    to[...] = tr
    ki = jnp.where(lane < IDX_DIM, tr, 0.0)
    ki2o[:, :LANES] = ki.astype(BF16)
    ki2o[:, LANES:] = pltpu.roll(ki, IDX_DIM, 1).astype(BF16)


def _prep(proj, tables, row0, rows, table_block):
    b0 = row0 // PREP_TM

    def cb(c):
        return pl.BlockSpec((PREP_TM, MIX_W), lambda i: (b0 + i, c))

    row512 = pl.BlockSpec((PREP_TM, MIX_W), lambda i: (i, 0))
    head_rows = pl.BlockSpec((PREP_TM * N_HEADS, HEAD_DIM), lambda i: (i, 0))
    return pl.pallas_call(
        _prep_kernel,
        grid=(rows // PREP_TM,),
        in_specs=[cb(CB_Q), cb(CB_K), cb(CB_V), cb(CB_QI),
                  pl.BlockSpec((PREP_TM, LANES), lambda i: (b0 + i, TAIL_COL // LANES)),
                  pl.BlockSpec((6, PREP_TM, LANES), lambda i: (0, table_block(i), 0))],
        out_specs=[row512, head_rows, row512, head_rows, row512, row512,
                   pl.BlockSpec((PREP_TM, LANES), lambda i: (i, 0)),
                   pl.BlockSpec((PREP_TM, 2 * LANES), lambda i: (i, 0))],
        out_shape=[jax.ShapeDtypeStruct((rows, MIX_W), BF16),
                   jax.ShapeDtypeStruct((rows * N_HEADS, HEAD_DIM), F32),
                   jax.ShapeDtypeStruct((rows, MIX_W), BF16),
                   jax.ShapeDtypeStruct((rows * N_HEADS, HEAD_DIM), F32),
                   jax.ShapeDtypeStruct((rows, MIX_W), BF16),
                   jax.ShapeDtypeStruct((rows, MIX_W), BF16),
                   jax.ShapeDtypeStruct((rows, LANES), F32),
                   jax.ShapeDtypeStruct((rows, 2 * LANES), BF16)],
        compiler_params=_params(1),
        name="prep",
    )(proj, proj, proj, proj, proj, tables)


KEY_NEG_INF = INT_MIN + 0x7FFFFF


def _key_to_float(key):
    return lax.bitcast_convert_type(jnp.where(key < 0, key ^ 0x7FFFFFFF, key), F32)


def _select_chunks(sc_ref, bias_ref, nc, k, allowed_fn):
    _, rows, cw = sc_ref.shape
    kf = float(k)

    def count_ge(c):
        acc = jnp.where(sc_ref[0] >= c, 1.0, 0.0)
        for j in range(1, nc):
            acc = acc + jnp.where(sc_ref[j] >= c, 1.0, 0.0)
        return jnp.sum(acc, axis=-1, keepdims=True)

    t = jnp.where(count_ge(jnp.zeros((rows, 1), F32)) >= kf, 0, INT_MIN).astype(I32)
    for bit in range(30, -1, -1):
        cand = t + (1 << bit)
        ok = jnp.logical_or(count_ge(_key_to_float(cand)) >= kf, cand <= KEY_NEG_INF)
        t = jnp.where(ok, cand, t)
    lo = _key_to_float(t)
    hi = _key_to_float(t + 1)
    need = kf - count_ge(hi)
    r_io = lax.broadcasted_iota(I32, (LANES, LANES), 0)
    c_io = lax.broadcasted_iota(I32, (LANES, LANES), 1)
    tri = jnp.where(r_io <= c_io, 1.0, 0.0).astype(BF16)

    off = jnp.zeros((rows, 1), F32)
    for j in range(nc):
        for u in range(cw // LANES):
            sl = slice(u * LANES, (u + 1) * LANES)
            su = sc_ref[j, :, sl]
            above = su >= hi
            tied = su >= lo
            e = jnp.where(above, 0.0, jnp.where(tied, 1.0, 0.0))
            rank = _dot(e.astype(BF16), tri) + off
            off = off + jnp.sum(e, axis=-1, keepdims=True)
            bias = jnp.where(above, 0.0, jnp.where(tied, jnp.where(rank <= need, 0.0, NEG_INF), NEG_INF))
            allowed = allowed_fn(j, u, su)
            bias_ref[j, :, sl] = bias if allowed is None else jnp.where(allowed, bias, NEG_INF)


def _index_scores(qi, w_scaled, ki_a, ki_b):
    acc = None
    for p in range(IDX_HEADS // 2):
        qp = qi[:, p * LANES:(p + 1) * LANES]
        for half, ki in enumerate((ki_a, ki_b)):
            hh = 2 * p + half
            term = jnp.maximum(_dot_nt(qp, ki), 0.0) * w_scaled[:, IDX_DIM + hh:IDX_DIM + hh + 1]
            acc = term if acc is None else acc + term
    return acc


KEY_CHUNK = 256
N_KEY_CHUNKS = SEQ // KEY_CHUNK
CAUSAL_VARIANTS = 4


def _dsa_prompt_kernel(q_ref, qi_ref, t_ref, k_ref, v_ref, ki2_ref, o_ref, sc_ref, bias_ref):
    i = pl.program_id(1)
    row = i * QBLOCK + lax.broadcasted_iota(I32, (QBLOCK, KEY_CHUNK), 0)
    lane = lax.broadcasted_iota(I32, (QBLOCK, KEY_CHUNK), 1)
    row_g = i * QBLOCK + lax.broadcasted_iota(I32, (QBLOCK, LANES), 0)
    lane_g = lax.broadcasted_iota(I32, (QBLOCK, LANES), 1)

    def attend(n_keys, n_visible_chunks):
        nc = n_keys // KEY_CHUNK

        def causal(j, u, su):
            return None if j < n_visible_chunks else j * KEY_CHUNK + u * LANES + lane_g <= row_g

        w_scaled = (t_ref[...] * (IDX_HEADS ** -0.5)) * (IDX_DIM ** -0.5)
        sc = _index_scores(qi_ref[...], w_scaled, ki2_ref[:n_keys, :LANES], ki2_ref[:n_keys, LANES:])
        for j in range(nc):
            sc_j = sc[:, j * KEY_CHUNK:(j + 1) * KEY_CHUNK]
            sc_ref[j] = sc_j if j < n_visible_chunks else jnp.where(j * KEY_CHUNK + lane <= row, sc_j, NEG_INF)
        _select_chunks(sc_ref.at[:nc], bias_ref.at[:nc], nc, TOPK, causal)
        bias = jnp.concatenate([bias_ref[j] for j in range(nc)], axis=-1)
        for h in range(N_HEADS):
            sl = slice(h * HEAD_DIM, (h + 1) * HEAD_DIM)
            s = _dot_nt(q_ref[:, sl], k_ref[:n_keys, sl]) * (HEAD_DIM ** -0.5) + bias
            m = jnp.max(s, axis=-1, keepdims=True)
            p = jnp.exp(s - m)
            l = jnp.sum(p, axis=-1, keepdims=True)
            o_ref[:, sl] = (_dot(p.astype(BF16), v_ref[:n_keys, sl]) / l).astype(BF16)

    blocks_per_variant = (SEQ // QBLOCK) // CAUSAL_VARIANTS
    for v in range(CAUSAL_VARIANTS):
        @pl.when(i // blocks_per_variant == v)
        def _(v=v):
            attend((v + 1) * (SEQ // CAUSAL_VARIANTS), (v * blocks_per_variant * QBLOCK) // KEY_CHUNK)


def _dsa_prompt(q, qi, tail, kb, vb, ki2):
    nb = SEQ // QBLOCK

    def qspec(w):
        return pl.BlockSpec((QBLOCK, w), lambda b, i: (b * nb + i, 0))

    def kspec(w):
        return pl.BlockSpec((SEQ, w), lambda b, i: (b, 0))

    return pl.pallas_call(
        _dsa_prompt_kernel,
        grid=(BATCH, nb),
        in_specs=[qspec(MIX_W), qspec(MIX_W), qspec(LANES), kspec(MIX_W), kspec(MIX_W), kspec(2 * LANES)],
        out_specs=qspec(MIX_W),
        out_shape=jax.ShapeDtypeStruct((MP, MIX_W), BF16),
        scratch_shapes=[pltpu.VMEM((N_KEY_CHUNKS, QBLOCK, KEY_CHUNK), F32)] * 2,
        compiler_params=_params(2),
        name="dsa_prompt",
    )(q, qi, tail, kb, vb, ki2)


def _dsa_sample_scores_kernel(pt_ref, qi_ref, w_ref, *rest):
    page_refs, knew_ref, o_ref = rest[:N_PAGES], rest[N_PAGES], rest[N_PAGES + 1]
    qi = qi_ref[0]
    w = w_ref[0] * (IDX_HEADS ** -0.5)

    def chunk_scores(d):
        r = jnp.maximum(d * (IDX_DIM ** -0.5), 0.0) * w
        return jnp.sum(r.reshape(DEC_SEQ, IDX_HEADS, LANES), axis=1)

    for p in range(N_PAGES):
        o_ref[0, :, p * LANES:(p + 1) * LANES] = chunk_scores(_dot(qi, page_refs[p][0, 0].astype(BF16)))
    new = chunk_scores(_dot_nt(qi, knew_ref[0]))
    tq = lax.broadcasted_iota(I32, (DEC_SEQ, LANES), 0)
    jk = lax.broadcasted_iota(I32, (DEC_SEQ, LANES), 1)
    o_ref[0, :, PAST:] = jnp.where(jk <= tq, new, NEG_INF)


def _dsa_sample_scores(page_table, qi32, w32, cache_kidx_t, layer, knew_pad):
    page_specs = [pl.BlockSpec((1, 1, IDX_DIM, PAGE_SIZE),
                               functools.partial(lambda b, pt, p: (layer, pt[b, p], 0, 0), p=p))
                  for p in range(N_PAGES)]
    grid_spec = pltpu.PrefetchScalarGridSpec(
        num_scalar_prefetch=1,
        grid=(DEC_BATCH,),
        in_specs=[pl.BlockSpec((1, DEC_SEQ * IDX_HEADS, IDX_DIM), lambda b, pt: (b, 0, 0)),
                  pl.BlockSpec((1, DEC_SEQ * IDX_HEADS, 1), lambda b, pt: (b, 0, 0))] + page_specs
                 + [pl.BlockSpec((1, LANES, IDX_DIM), lambda b, pt: (b, 0, 0))],
        out_specs=pl.BlockSpec((1, DEC_SEQ, SAMPLE_KEYS), lambda b, pt: (b, 0, 0)),
    )
    return pl.pallas_call(
        _dsa_sample_scores_kernel,
        grid_spec=grid_spec,
        out_shape=jax.ShapeDtypeStruct((DEC_BATCH, DEC_SEQ, SAMPLE_KEYS), F32),
        compiler_params=_params(1),
        name="dsa_sample_scores",
    )(page_table, qi32, w32, *([cache_kidx_t] * N_PAGES), knew_pad)


SEL_ROWS = 128
N_SAMPLE_CHUNKS = SAMPLE_KEYS // LANES


def _sample_select_kernel(in_ref, o_ref, sc_ref, bias_ref):
    for c in range(N_SAMPLE_CHUNKS):
        sc_ref[c] = in_ref[:, c * LANES:(c + 1) * LANES]
    _select_chunks(sc_ref, bias_ref, N_SAMPLE_CHUNKS, TOPK, lambda j, u, su: su > NEG_INF)
    for c in range(N_SAMPLE_CHUNKS):
        o_ref[:, c * LANES:(c + 1) * LANES] = bias_ref[c]


def _sample_select(sc):
    return pl.pallas_call(
        _sample_select_kernel,
        grid=(MS // SEL_ROWS,),
        in_specs=[pl.BlockSpec((SEL_ROWS, SAMPLE_KEYS), lambda i: (i, 0))],
        out_specs=pl.BlockSpec((SEL_ROWS, SAMPLE_KEYS), lambda i: (i, 0)),
        out_shape=jax.ShapeDtypeStruct((MS, SAMPLE_KEYS), F32),
        scratch_shapes=[pltpu.VMEM((N_SAMPLE_CHUNKS, SEL_ROWS, LANES), F32)] * 2,
        compiler_params=_params(1),
        name="sample_select",
    )(sc)


def _dsa_sample_attn_kernel(pt_ref, q_ref, bias_ref, *rest):
    k_refs = rest[:N_PAGES]
    v_refs = rest[N_PAGES:2 * N_PAGES]
    knew_ref, vnew_ref, o_ref, s_ref = rest[2 * N_PAGES:]
    q = q_ref[0]
    def head_rows(h):
        return pl.ds(h, PAGE_SIZE, stride=N_HEADS)

    for h in range(N_HEADS):
        rs = slice(h * DEC_SEQ, (h + 1) * DEC_SEQ)
        hs = slice(h * HEAD_DIM, (h + 1) * HEAD_DIM)
        qh = q[:, hs]
        for p in range(N_PAGES):
            s_ref[rs, p * LANES:(p + 1) * LANES] = _dot_nt(qh, k_refs[p][0, 0, head_rows(h), :].astype(BF16))
        s_ref[rs, PAST:] = _dot_nt(qh, knew_ref[0, :, hs])
    bias = bias_ref[0]
    s = s_ref[...] * (HEAD_DIM ** -0.5) + jnp.concatenate([bias] * N_HEADS, axis=0)
    m = jnp.max(s, axis=-1, keepdims=True)
    pr = jnp.exp(s - m)
    l = jnp.sum(pr, axis=-1, keepdims=True)
    pb = pr.astype(BF16)
    for h in range(N_HEADS):
        rs = slice(h * DEC_SEQ, (h + 1) * DEC_SEQ)
        hs = slice(h * HEAD_DIM, (h + 1) * HEAD_DIM)
        acc = _dot(pb[rs, PAST:], vnew_ref[0, :, hs])
        for p in range(N_PAGES):
            acc = acc + _dot(pb[rs, p * LANES:(p + 1) * LANES], v_refs[p][0, 0, head_rows(h), :].astype(BF16))
        o_ref[0, :, hs] = (acc / l[rs]).astype(BF16)


def _dsa_sample_attn(page_table, q, bias, cache_k, cache_v, layer, knew_pad, vnew_pad):
    def page_spec(p):
        return pl.BlockSpec((1, 1, PAGE_SIZE * N_HEADS, HEAD_DIM),
                            functools.partial(lambda b, pt, p: (layer, pt[b, p], 0, 0), p=p))

    grid_spec = pltpu.PrefetchScalarGridSpec(
        num_scalar_prefetch=1,
        grid=(DEC_BATCH,),
        in_specs=[pl.BlockSpec((1, DEC_SEQ, MIX_W), lambda b, pt: (b, 0, 0)),
                  pl.BlockSpec((1, DEC_SEQ, SAMPLE_KEYS), lambda b, pt: (b, 0, 0))]
                 + [page_spec(p) for p in range(N_PAGES)] + [page_spec(p) for p in range(N_PAGES)]
                 + [pl.BlockSpec((1, LANES, MIX_W), lambda b, pt: (b, 0, 0)),
                    pl.BlockSpec((1, LANES, MIX_W), lambda b, pt: (b, 0, 0))],
        out_specs=pl.BlockSpec((1, DEC_SEQ, MIX_W), lambda b, pt: (b, 0, 0)),
        scratch_shapes=[pltpu.VMEM((N_HEADS * DEC_SEQ, SAMPLE_KEYS), F32)],
    )
    return pl.pallas_call(
        _dsa_sample_attn_kernel,
        grid_spec=grid_spec,
        out_shape=jax.ShapeDtypeStruct((DEC_BATCH, DEC_SEQ, MIX_W), BF16),
        compiler_params=_params(1),
        name="dsa_sample_attn",
    )(page_table, q, bias, *([cache_k] * N_PAGES), *([cache_v] * N_PAGES), knew_pad, vnew_pad)


CONV_TT = 512


def _conv_prompt_kernel(cx_ref, cb_ref, cc_ref, w_ref, yb_ref, buf_ref, carry_ref):
    j = pl.program_id(1)

    @pl.when(j == 0)
    def _():
        carry_ref[...] = jnp.zeros_like(carry_ref)

    z = cc_ref[...] * cx_ref[...]
    row = lax.broadcasted_iota(I32, z.shape, 0)
    c0 = carry_ref[0:1, :]
    c1 = carry_ref[1:2, :]
    zm1 = jnp.where(row == 0, c1, pltpu.roll(z, 1, 0))
    zm2 = jnp.where(row == 0, c0, jnp.where(row == 1, c1, pltpu.roll(z, 2, 0)))
    zc = w_ref[0:1, :] * zm2 + w_ref[1:2, :] * zm1 + w_ref[2:3, :] * z
    yb_ref[...] = (cb_ref[...] * zc).astype(BF16)
    last = z[CONV_TT - 2:CONV_TT, :]
    carry_ref[0:2, :] = last
    buf_ref[0] = last


def _conv_prompt(proj, conv_w):
    nt = SEQ // CONV_TT

    def cb(c):
        return pl.BlockSpec((CONV_TT, MIX_W), lambda b, j: (b * nt + j, c))

    return pl.pallas_call(
        _conv_prompt_kernel,
        grid=(BATCH, nt),
        in_specs=[cb(CB_CX), cb(CB_CB), cb(CB_CC), pl.BlockSpec((3, MIX_W), lambda b, j: (0, 0))],
        out_specs=[pl.BlockSpec((CONV_TT, MIX_W), lambda b, j: (b * nt + j, 0)),
                   pl.BlockSpec((1, 2, MIX_W), lambda b, j: (b, 0, 0))],
        out_shape=[jax.ShapeDtypeStruct((MP, MIX_W), BF16),
                   jax.ShapeDtypeStruct((BATCH, 2, MIX_W), F32)],
        scratch_shapes=[pltpu.VMEM((8, MIX_W), F32)],
        compiler_params=_params(2),
        name="conv_prompt",
    )(proj, proj, proj, conv_w)


def _gmlp_prompt_kernel(gu_ref, gv_ref, w_ref, bt_ref, o_ref):
    r_io = lax.broadcasted_iota(I32, (GMLP_CHUNK, GMLP_CHUNK), 0)
    c_io = lax.broadcasted_iota(I32, (GMLP_CHUNK, GMLP_CHUNK), 1)
    tril = c_io <= r_io
    for g in range(GMLP_GROUPS):
        sl = slice(g * LANES, (g + 1) * LANES)
        wm = jnp.where(tril, w_ref[g], 0.0).astype(BF16)
        for c in range(GMLP_STEP_CHUNKS):
            rows = slice(c * GMLP_CHUNK, (c + 1) * GMLP_CHUNK)
            z = _dot(wm, gv_ref[rows, sl].astype(BF16)) + bt_ref[:, g:g + 1]
            o_ref[rows, sl] = (gu_ref[rows, sl] * z).astype(BF16)


GMLP_STEP_CHUNKS = 4


def _gmlp_prompt(proj, gmlp_w, gmlp_bt):
    rows = GMLP_STEP_CHUNKS * GMLP_CHUNK

    def cb(c):
        return pl.BlockSpec((rows, MIX_W), lambda i: (i, c))

    return pl.pallas_call(
        _gmlp_prompt_kernel,
        grid=(MP // rows,),
        in_specs=[cb(CB_GU), cb(CB_GV),
                  pl.BlockSpec((GMLP_GROUPS, GMLP_CHUNK, GMLP_CHUNK), lambda i: (0, 0, 0)),
                  pl.BlockSpec((GMLP_CHUNK, GMLP_GROUPS), lambda i: (0, 0))],
        out_specs=pl.BlockSpec((rows, MIX_W), lambda i: (i, 0)),
        out_shape=jax.ShapeDtypeStruct((MP, MIX_W), BF16),
        compiler_params=_params(1),
        name="gmlp_prompt",
    )(proj, proj, gmlp_w, gmlp_bt)


def _mix_sample_kernel(cx_ref, cb_ref, cc_ref, gu_ref, gv_ref, buf_ref, cw_ref, gw_ref, gb_ref,
                       yb_ref, nbuf_ref, yd_ref):
    zf = [buf_ref[:, 0, :], buf_ref[:, 1, :]]
    for t in range(DEC_SEQ):
        zf.append(cc_ref[:, t, :] * cx_ref[:, t, :])
    for t in range(DEC_SEQ):
        zc = cw_ref[0:1, :] * zf[t] + cw_ref[1:2, :] * zf[t + 1] + cw_ref[2:3, :] * zf[t + 2]
        yb_ref[:, t, :] = (cb_ref[:, t, :] * zc).astype(BF16)
    nbuf_ref[:, 0, :] = zf[DEC_SEQ]
    nbuf_ref[:, 1, :] = zf[DEC_SEQ + 1]
    v = [gv_ref[:, s, :] for s in range(DEC_SEQ)]
    for t in range(DEC_SEQ):
        z = gb_ref[t:t + 1, :]
        for s in range(t + 1):
            z = z + gw_ref[t * DEC_SEQ + s:t * DEC_SEQ + s + 1, :] * v[s]
        yd_ref[:, t, :] = (gu_ref[:, t, :] * z).astype(BF16)


def _mix_sample(proj_s, state_conv_l, conv_w, gw_lane, gb_lane):
    def cb(c):
        return pl.BlockSpec((DEC_BATCH, DEC_SEQ, MIX_W), lambda i: (0, 0, c))

    full3 = pl.BlockSpec((DEC_BATCH, DEC_SEQ, MIX_W), lambda i: (0, 0, 0))
    buf3 = pl.BlockSpec((DEC_BATCH, 2, MIX_W), lambda i: (0, 0, 0))
    return pl.pallas_call(
        _mix_sample_kernel,
        grid=(1,),
        in_specs=[cb(CB_CX), cb(CB_CB), cb(CB_CC), cb(CB_GU), cb(CB_GV), buf3,
                  pl.BlockSpec((3, MIX_W), lambda i: (0, 0)),
                  pl.BlockSpec((DEC_SEQ * DEC_SEQ, MIX_W), lambda i: (0, 0)),
                  pl.BlockSpec((DEC_SEQ, MIX_W), lambda i: (0, 0))],
        out_specs=[full3, buf3, full3],
        out_shape=[jax.ShapeDtypeStruct((DEC_BATCH, DEC_SEQ, MIX_W), BF16),
                   jax.ShapeDtypeStruct((DEC_BATCH, 2, MIX_W), F32),
                   jax.ShapeDtypeStruct((DEC_BATCH, DEC_SEQ, MIX_W), BF16)],
        compiler_params=_params(1),
        name="mix_sample",
    )(proj_s, proj_s, proj_s, proj_s, proj_s, state_conv_l, conv_w, gw_lane, gb_lane)


def _ssm_params_kernel(ar_ref, ai_ref, ldt_ref, br_ref, bi_ref, ab_ref, bbr_ref, bbi_ref):
    ar, ai = ar_ref[...], ai_ref[...]
    dt = jnp.exp(ldt_ref[...])
    mag = jnp.exp(dt * ar)
    abar_re, abar_im = mag * jnp.cos(dt * ai), mag * jnp.sin(dt * ai)
    den = ar * ar + ai * ai
    nr, ni = abar_re - 1.0, abar_im
    coef_re = (nr * ar + ni * ai) / den
    coef_im = (ni * ar - nr * ai) / den
    ab_ref[...] = jnp.zeros_like(ab_ref)
    ab_ref[0:1, :] = abar_re
    ab_ref[1:2, :] = abar_im
    br, bi = br_ref[...], bi_ref[...]
    bbr_ref[...] = (coef_re * br - coef_im * bi).astype(BF16)
    bbi_ref[...] = (coef_re * bi + coef_im * br).astype(BF16)


def _ssm_params(a_re, a_im, ldt, bd_re, bd_im):
    row = pl.BlockSpec((1, SSM_STATE), lambda i: (0, 0))
    mat = pl.BlockSpec((MIX_W, SSM_STATE), lambda i: (0, 0))
    return pl.pallas_call(
        _ssm_params_kernel,
        grid=(1,),
        in_specs=[row, row, row, mat, mat],
        out_specs=[pl.BlockSpec((8, SSM_STATE), lambda i: (0, 0)), mat, mat],
        out_shape=[jax.ShapeDtypeStruct((8, SSM_STATE), F32),
                   jax.ShapeDtypeStruct((MIX_W, SSM_STATE), BF16),
                   jax.ShapeDtypeStruct((MIX_W, SSM_STATE), BF16)],
        compiler_params=_params(1),
        name="ssm_params",
    )(a_re, a_im, ldt, bd_re, bd_im)


def _gelu_tanh(x):
    return 0.5 * x * (1.0 + jnp.tanh(0.7978845608028654 * (x + 0.044715 * (x * x * x))))


SSM_BLOCK_STATES = SSM_STATE // (MIX_W // LANES)


def _ssm_drive(ub, bb_ref):
    return jnp.concatenate(
        [_dot(ub[:, c * LANES:(c + 1) * LANES],
              bb_ref[c * LANES:(c + 1) * LANES, c * SSM_BLOCK_STATES:(c + 1) * SSM_BLOCK_STATES])
         for c in range(MIX_W // LANES)], axis=-1)


def _ssm_readout(u, hr, hi, cr_ref, ci_ref, d_ref, wg_ref, bg_ref):
    hrb, hib = hr.astype(BF16), hi.astype(BF16)
    parts = []
    for c in range(MIX_W // LANES):
        ss = slice(c * SSM_BLOCK_STATES, (c + 1) * SSM_BLOCK_STATES)
        cs = slice(c * LANES, (c + 1) * LANES)
        parts.append(_dot(hrb[:, ss], cr_ref[ss, cs]) - _dot(hib[:, ss], ci_ref[ss, cs]))
    y = jnp.concatenate(parts, axis=-1) + d_ref[...] * u
    g = _gelu_tanh(y)
    return g * _sigmoid(_dot(g.astype(BF16), wg_ref[...]) + bg_ref[...])


SSM_TC = 128
SSM_SLABS = MIX_W // LANES


def _ssm_prompt_kernel(u0, u1, u2, u3, ab_ref, bbr_ref, bbi_ref, cr_ref, ci_ref, d_ref, wg_ref, bg_ref,
                       y_ref, str_ref, sti_ref, hr_ref, hi_ref, carry_ref, il_ref):
    j = pl.program_id(0)

    @pl.when(j == 0)
    def _():
        carry_ref[...] = jnp.zeros_like(carry_ref)

    def batch_rows(b):
        return pl.ds(b, SSM_TC, stride=BATCH)

    for b, u_ref in enumerate((u0, u1, u2, u3)):
        for c in range(SSM_SLABS):
            il_ref[c, batch_rows(b), :] = u_ref[:, c * LANES:(c + 1) * LANES]
    u = jnp.concatenate([il_ref[c] for c in range(SSM_SLABS)], axis=-1)
    ub = u.astype(BF16)
    hr_ref[...] = _ssm_drive(ub, bbr_ref)
    hi_ref[...] = _ssm_drive(ub, bbi_ref)
    ar, ai = ab_ref[0:1, :], ab_ref[1:2, :]

    def step2(t2, carry):
        pr, pi = carry
        rows = pl.ds(pl.multiple_of(t2 * 2 * BATCH, 2 * BATCH), 2 * BATCH)
        br, bi = hr_ref[rows, :], hi_ref[rows, :]
        er = ar * pr - ai * pi + br[:BATCH]
        ei = ar * pi + ai * pr + bi[:BATCH]
        nr = ar * er - ai * ei + br[BATCH:]
        ni = ar * ei + ai * er + bi[BATCH:]
        hr_ref[rows, :] = jnp.concatenate([er, nr], axis=0)
        hi_ref[rows, :] = jnp.concatenate([ei, ni], axis=0)
        return nr, ni

    fr, fi = lax.fori_loop(0, SSM_TC // 2, step2, (carry_ref[0:BATCH, :], carry_ref[BATCH:2 * BATCH, :]), unroll=4)
    carry_ref[0:BATCH, :] = fr
    carry_ref[BATCH:2 * BATCH, :] = fi
    str_ref[...] = fr
    sti_ref[...] = fi
    y = _ssm_readout(u, hr_ref[...], hi_ref[...], cr_ref, ci_ref, d_ref, wg_ref, bg_ref)
    for c in range(SSM_SLABS):
        il_ref[c] = y[:, c * LANES:(c + 1) * LANES]
    for b in range(BATCH):
        for c in range(SSM_SLABS):
            y_ref[b, :, c * LANES:(c + 1) * LANES] = il_ref[c, batch_rows(b), :].astype(BF16)


def _ssm_prompt(proj, ab, bbr, bbi, cdr, cdi, d, wglu, bglu):
    nt = SEQ // SSM_TC

    def const(shape):
        return pl.BlockSpec(shape, lambda j: (0,) * len(shape))

    u_specs = [pl.BlockSpec((SSM_TC, MIX_W), functools.partial(lambda j, b: (b * nt + j, CB_SU), b=b))
               for b in range(BATCH)]
    state = pl.BlockSpec((BATCH, SSM_STATE), lambda j: (0, 0))
    return pl.pallas_call(
        _ssm_prompt_kernel,
        grid=(nt,),
        in_specs=u_specs + [const((8, SSM_STATE)), const((MIX_W, SSM_STATE)), const((MIX_W, SSM_STATE)),
                            const((SSM_STATE, MIX_W)), const((SSM_STATE, MIX_W)), const((1, MIX_W)),
                            const((MIX_W, MIX_W)), const((1, MIX_W))],
        out_specs=[pl.BlockSpec((BATCH, SSM_TC, MIX_W), lambda j: (0, j, 0)), state, state],
        out_shape=[jax.ShapeDtypeStruct((BATCH, SEQ, MIX_W), BF16),
                   jax.ShapeDtypeStruct((BATCH, SSM_STATE), F32),
                   jax.ShapeDtypeStruct((BATCH, SSM_STATE), F32)],
        scratch_shapes=[pltpu.VMEM((BATCH * SSM_TC, SSM_STATE), F32), pltpu.VMEM((BATCH * SSM_TC, SSM_STATE), F32),
                        pltpu.VMEM((2 * BATCH, SSM_STATE), F32),
                        pltpu.VMEM((SSM_SLABS, BATCH * SSM_TC, LANES), F32)],
        compiler_params=_params(1),
        name="ssm_prompt",
    )(proj, proj, proj, proj, ab, bbr, bbi, cdr, cdi, d, wglu, bglu)


def _ssm_sample_kernel(u_ref, h0r_ref, h0i_ref, ab_ref, bbr_ref, bbi_ref, cr_ref, ci_ref, d_ref, wg_ref, bg_ref,
                       y_ref, nr_ref, ni_ref):
    ar, ai = ab_ref[0:1, :], ab_ref[1:2, :]
    hr, hi = h0r_ref[...], h0i_ref[...]
    for t in range(DEC_SEQ):
        u = u_ref[:, t, :]
        ub = u.astype(BF16)
        hr, hi = (ar * hr - ai * hi + _ssm_drive(ub, bbr_ref),
                  ar * hi + ai * hr + _ssm_drive(ub, bbi_ref))
        y_ref[:, t, :] = _ssm_readout(u, hr, hi, cr_ref, ci_ref, d_ref, wg_ref, bg_ref).astype(BF16)
    nr_ref[...] = hr
    ni_ref[...] = hi


def _ssm_sample(proj_s, h0r, h0i, ab, bbr, bbi, cdr, cdi, d, wglu, bglu):
    def const(shape):
        return pl.BlockSpec(shape, lambda i: (0,) * len(shape))

    st = const((DEC_BATCH, SSM_STATE))
    return pl.pallas_call(
        _ssm_sample_kernel,
        grid=(1,),
        in_specs=[pl.BlockSpec((DEC_BATCH, DEC_SEQ, MIX_W), lambda i: (0, 0, CB_SU)), st, st,
                  const((8, SSM_STATE)), const((MIX_W, SSM_STATE)), const((MIX_W, SSM_STATE)),
                  const((SSM_STATE, MIX_W)), const((SSM_STATE, MIX_W)), const((1, MIX_W)),
                  const((MIX_W, MIX_W)), const((1, MIX_W))],
        out_specs=[const((DEC_BATCH, DEC_SEQ, MIX_W)), st, st],
        out_shape=[jax.ShapeDtypeStruct((DEC_BATCH, DEC_SEQ, MIX_W), BF16),
                   jax.ShapeDtypeStruct((DEC_BATCH, SSM_STATE), F32),
                   jax.ShapeDtypeStruct((DEC_BATCH, SSM_STATE), F32)],
        compiler_params=_params(1),
        name="ssm_sample",
    )(proj_s, h0r, h0i, ab, bbr, bbi, cdr, cdi, d, wglu, bglu)


def _rope_tables():
    lane = jnp.arange(LANES)

    def inv(half):
        return ROPE_THETA ** (-jnp.arange(half, dtype=F32) / half)

    inv_qk = jnp.where(lane < ROT_DIM, inv(ROT_DIM // 2)[lane % (ROT_DIM // 2)], 0.0)
    idx = inv(IDX_ROT // 2)[lane % (IDX_ROT // 2)]
    inv_i = jnp.where((lane % IDX_DIM) < IDX_ROT, idx, 0.0)
    inv_t = jnp.where(lane < IDX_ROT, idx, 0.0)
    pos = jnp.concatenate([jnp.arange(SEQ), PAST + jnp.arange(PREP_TM) % DEC_SEQ]).astype(F32)
    tabs = []
    for inv_row in (inv_qk, inv_i, inv_t):
        ang = pos[:, None] * inv_row[None, :].astype(F32)
        tabs += [jnp.cos(ang), jnp.sin(ang)]
    return jnp.stack(tabs)


def _pad_rows(a, rows):
    return jnp.pad(a, ((0, 0), (0, rows - a.shape[1]), (0, 0)))


def kernel(x_prompt, x_sample, cache_k, cache_v, cache_kidx, state_conv, state_ssm_re, state_ssm_im, page_table,
           norm_mix, w_in, conv_w, ssm_a_re, ssm_a_im, ssm_log_dt, ssm_b_re, ssm_b_im, ssm_c_re, ssm_c_im, ssm_d,
           w_glu, b_glu, gmlp_w, gmlp_b, w_branch, w_out, norm_ffn, w_ffn_in, w_ffn_out, norm_final):
    x = jnp.concatenate([x_prompt.reshape(MP, D_MODEL), x_sample.reshape(MS, D_MODEL)], axis=0)
    tables = _rope_tables()
    eye_g = jnp.eye(SSM_GROUPS, dtype=F32)
    cache_kidx_t = jnp.swapaxes(cache_kidx, 2, 3)
    n_pool = cache_k.shape[1]
    cache_k2 = cache_k.reshape(DEPTH, n_pool, PAGE_SIZE * N_HEADS, HEAD_DIM)
    cache_v2 = cache_v.reshape(DEPTH, n_pool, PAGE_SIZE * N_HEADS, HEAD_DIM)
    n_gate0 = w_in.shape[2] - N_BRANCH * D_MODEL
    c_ki = 3 * MIX_W + IDX_HEADS * IDX_DIM
    c_cx = c_ki + IDX_DIM + IDX_HEADS
    w_ffn_out_b = w_ffn_out.astype(BF16)

    st_p, st_s = [], []
    for l in range(DEPTH):
        wl = w_in[l]
        w_main = jnp.concatenate(
            [wl[:, :c_ki], wl[:, c_cx:n_gate0], wl[:, c_ki:c_cx],
             jnp.zeros((D_MODEL, N_MAIN - n_gate0), F32)], axis=1).astype(BF16)
        w_gate = wl[:, n_gate0:].astype(BF16)

        proj = _norm_mm(x, norm_mix[l][None, :], w_main, 1088, 768)
        q_p, k_p, kb_p, v_p, vb_p, qi_p, tail_p, ki2_p = _prep(
            proj, tables, 0, MP, lambda i: i % (SEQ // PREP_TM))
        q_s, k_s, kb_s, v_s, vb_s, qi_s, tail_s, _ = _prep(
            proj, tables, MP, MS, lambda i: SEQ // PREP_TM)

        ya_p = _dsa_prompt(q_p, qi_p, tail_p, kb_p, vb_p, ki2_p)
        tail_s3 = tail_s.reshape(DEC_BATCH, DEC_SEQ, LANES)
        w_s = tail_s3[:, :, IDX_DIM:IDX_DIM + IDX_HEADS].reshape(DEC_BATCH, DEC_SEQ * IDX_HEADS, 1)
        kinew = _pad_rows(tail_s3[:, :, :IDX_DIM].astype(BF16), LANES)
        sc_s = _dsa_sample_scores(page_table, qi_s.reshape(DEC_BATCH, DEC_SEQ * IDX_HEADS, IDX_DIM), w_s,
                                  cache_kidx_t, l, kinew)
        bias_s = _sample_select(sc_s.reshape(MS, SAMPLE_KEYS)).reshape(DEC_BATCH, DEC_SEQ, SAMPLE_KEYS)
        knew = _pad_rows(kb_s.reshape(DEC_BATCH, DEC_SEQ, MIX_W), LANES)
        vnew = _pad_rows(vb_s.reshape(DEC_BATCH, DEC_SEQ, MIX_W), LANES)
        ya_s = _dsa_sample_attn(page_table, q_s.reshape(DEC_BATCH, DEC_SEQ, MIX_W), bias_s, cache_k2, cache_v2, l,
                                knew, vnew)

        proj_s = proj[MP:].reshape(DEC_BATCH, DEC_SEQ, N_MAIN)
        yb_p, buf_p = _conv_prompt(proj, conv_w[l])
        yd_p = _gmlp_prompt(proj, gmlp_w[l], gmlp_b[l].T)
        gw_lane = jnp.repeat(gmlp_w[l][:, :DEC_SEQ, :DEC_SEQ].transpose(1, 2, 0).reshape(DEC_SEQ * DEC_SEQ, GMLP_GROUPS),
                             LANES, axis=1)
        gb_lane = jnp.repeat(gmlp_b[l][:, :DEC_SEQ].T, LANES, axis=1)
        yb_s, buf_s, yd_s = _mix_sample(proj_s, state_conv[l], conv_w[l], gw_lane, gb_lane)

        bd_re = jnp.einsum('gnp,gh->gphn', ssm_b_re[l], eye_g).reshape(MIX_W, SSM_STATE)
        bd_im = jnp.einsum('gnp,gh->gphn', ssm_b_im[l], eye_g).reshape(MIX_W, SSM_STATE)
        cd_re = jnp.einsum('gpn,gh->gnhp', ssm_c_re[l], eye_g).reshape(SSM_STATE, MIX_W).astype(BF16)
        cd_im = jnp.einsum('gpn,gh->gnhp', ssm_c_im[l], eye_g).reshape(SSM_STATE, MIX_W).astype(BF16)
        ldt = jnp.repeat(ssm_log_dt[l], SSM_N)[None, :]
        ab, bbr, bbi = _ssm_params(ssm_a_re[l].reshape(1, SSM_STATE), ssm_a_im[l].reshape(1, SSM_STATE), ldt,
                                   bd_re, bd_im)
        ssm_consts = (ab, bbr, bbi, cd_re, cd_im, ssm_d[l][None, :], w_glu[l].astype(BF16), b_glu[l][None, :])
        yc_p, hre_p, him_p = _ssm_prompt(proj, *ssm_consts)
        yc_p = yc_p.reshape(MP, MIX_W)
        yc_s, nre_s, nim_s = _ssm_sample(proj_s, state_ssm_re[l].reshape(DEC_BATCH, SSM_STATE),
                                         state_ssm_im[l].reshape(DEC_BATCH, SSM_STATE), *ssm_consts)

        ys_sample = tuple(y.reshape(MS, MIX_W) for y in (ya_s, yb_s, yc_s, yd_s))
        merged = _gate_merge(x, norm_mix[l][None, :], w_gate, (ya_p, yb_p, yc_p, yd_p), ys_sample, w_branch, l, 512)
        x = _mm_res(merged, w_out, l, x, 1088, 512)
        act = _swiglu(x, norm_ffn[l][None, :], w_ffn_in, l, 1088, 512)
        x = _mm_res(act, w_ffn_out_b, l, x, 1088, 256)

        gv_s = proj[MP:, CB_GV * MIX_W:(CB_GV + 1) * MIX_W]
        st_p.append((k_p.reshape(BATCH, SEQ, N_HEADS, HEAD_DIM), v_p.reshape(BATCH, SEQ, N_HEADS, HEAD_DIM),
                     tail_p[:, :IDX_DIM].reshape(BATCH, SEQ, IDX_DIM), buf_p,
                     hre_p.reshape(BATCH, SSM_GROUPS, SSM_N), him_p.reshape(BATCH, SSM_GROUPS, SSM_N)))
        st_s.append((k_s.reshape(DEC_BATCH, DEC_SEQ, N_HEADS, HEAD_DIM),
                     v_s.reshape(DEC_BATCH, DEC_SEQ, N_HEADS, HEAD_DIM),
                     tail_s[:, :IDX_DIM].reshape(DEC_BATCH, DEC_SEQ, IDX_DIM), buf_s,
                     nre_s.reshape(DEC_BATCH, SSM_GROUPS, SSM_N), nim_s.reshape(DEC_BATCH, SSM_GROUPS, SSM_N),
                     gv_s.reshape(DEC_BATCH, DEC_SEQ, MIX_W)))

    y_p = _final_norm(x, norm_final[None, :], 0, MP, MS)
    y_s = _final_norm(x, norm_final[None, :], MP, MS, MS)
    outs = [y_p.reshape(BATCH, SEQ, D_MODEL), y_s.reshape(DEC_BATCH, DEC_SEQ, D_MODEL)]
    outs += [jnp.stack([s[i] for s in st_p]) for i in range(6)]
    outs += [jnp.stack([s[i] for s in st_s]) for i in range(7)]
    return tuple(outs)
```

```python
import functools

import jax
import jax.numpy as jnp
from jax import lax
from jax.experimental import pallas as pl
from jax.experimental.pallas import tpu as pltpu

F32 = jnp.float32
BF16 = jnp.bfloat16
I32 = jnp.int32

D_MODEL = 2048
BATCH = 4
SEQ = 2048
DEPTH = 2
DEC_BATCH = 128
DEC_SEQ = 4
PAGE_SIZE = 128
N_PAGES = 16
PAST = N_PAGES * PAGE_SIZE
MIX_W = 512
N_HEADS = 4
HEAD_DIM = 128
ROT_DIM = 32
IDX_HEADS = 8
IDX_DIM = 64
IDX_ROT = 16
TOPK = 256
QBLOCK = 128
ROPE_THETA = 500000.0
SSM_P = 16
SSM_GROUPS = 32
SSM_N = 64
SSM_STATE = SSM_GROUPS * SSM_N
GMLP_CHUNK = 128
GMLP_GROUPS = 4
D_FF = 5632
N_BRANCH = 4

MP = BATCH * SEQ
MS = DEC_BATCH * DEC_SEQ
M = MP + MS
LANES = 128
TAIL_COL = 10 * MIX_W
W_TILE = 512
N_HEAD_BLOCKS = 4
N_MAIN_BLOCKS = 11
N_MAIN = N_MAIN_BLOCKS * W_TILE
C_TAIL = N_HEAD_BLOCKS * W_TILE
C_MIX = C_TAIL + IDX_DIM + IDX_HEADS
W_SHIFT = -C_MIX % W_TILE
CB_Q, CB_K, CB_V, CB_QI, CB_CX, CB_CB, CB_CC, CB_SU, CB_GU, CB_GV = range(10)
SAMPLE_KEYS = PAST + LANES
NEG_INF = float("-inf")
INT_MIN = -2 ** 31
VMEM_LIMIT = 56 * 1024 * 1024


def _params(n_axes, vmem=VMEM_LIMIT):
    return pltpu.CompilerParams(dimension_semantics=("arbitrary",) * n_axes, vmem_limit_bytes=vmem)


def _rms(x, g):
    return x * lax.rsqrt(jnp.mean(x * x, axis=-1, keepdims=True) + 1e-6) * g


def _dot(a, b):
    return jnp.dot(a, b, preferred_element_type=F32)


def _dot_nt(a, b):
    return lax.dot_general(a, b, (((1,), (1,)), ((), ())), preferred_element_type=F32)


def _sigmoid(x):
    return 1.0 / (1.0 + jnp.exp(-x))


def _norm_mm_kernel(x_ref, g_ref, wh_ref, ws_ref, o_ref, h_ref):
    j = pl.program_id(1)

    @pl.when(j == 0)
    def _():
        h_ref[...] = _rms(x_ref[...], g_ref[...]).astype(BF16)

    from_head = jnp.logical_or(j < N_HEAD_BLOCKS, j == N_MAIN_BLOCKS - 1)

    @pl.when(from_head)
    def _():
        o_ref[...] = _dot(h_ref[...], wh_ref[0])

    @pl.when(jnp.logical_not(from_head))
    def _():
        o_ref[...] = _dot(h_ref[...], ws_ref[0])


def _norm_mm(x, g, w_head, w_shift, layer, tm):
    m, k = x.shape
    nb = N_MAIN_BLOCKS
    return pl.pallas_call(
        _norm_mm_kernel,
        grid=(m // tm, nb),
        in_specs=[pl.BlockSpec((tm, k), lambda i, j: (i, 0)),
                  pl.BlockSpec((1, k), lambda i, j: (0, 0)),
                  pl.BlockSpec((1, k, W_TILE), lambda i, j: (
                      layer, 0, jnp.where(j < N_HEAD_BLOCKS, j, jnp.where(j < nb - 1, N_HEAD_BLOCKS - 1, N_HEAD_BLOCKS)))),
                  pl.BlockSpec((1, k, W_TILE), lambda i, j: (
                      layer, 0, jnp.clip(j + 1, N_HEAD_BLOCKS + 1, nb - 1)))],
        out_specs=pl.BlockSpec((tm, W_TILE), lambda i, j: (i, j)),
        out_shape=jax.ShapeDtypeStruct((m, N_MAIN), F32),
        scratch_shapes=[pltpu.VMEM((tm, k), BF16)],
        compiler_params=_params(2),
        name="norm_mm",
    )(x, g, w_head, w_shift)


def _gate_merge_kernel(x_ref, g_ref, wg0, wg1, wg2, wg3, yp0, yp1, yp2, yp3, ys0, ys1, ys2, ys3, wb_ref,
                       o_ref, h_ref):
    @pl.when(pl.program_id(1) == 0)
    def _():
        h_ref[...] = _rms(x_ref[...], g_ref[...]).astype(BF16)

    h = h_ref[...]
    is_prompt = pl.program_id(0) < MP // MS
    acc = None
    for kk, (wg, yp, ys) in enumerate(((wg0, yp0, ys0), (wg1, yp1, ys1), (wg2, yp2, ys2), (wg3, yp3, ys3))):
        gate = _sigmoid(_dot(h, wg[0]))
        y = jnp.where(is_prompt, yp[...], ys[...])
        term = gate * _dot(y, wb_ref[0, kk].astype(BF16))
        acc = term if acc is None else acc + term
    o_ref[...] = acc.astype(BF16)


def _gate_merge(x, g, wg, ys_prompt, ys_sample, wb, layer, tn):
    tm = MS
    nj = D_MODEL // tn
    gate0 = (wg.shape[2] - N_BRANCH * D_MODEL) // tn
    wg_specs = [pl.BlockSpec((1, D_MODEL, tn), functools.partial(lambda i, j, kk: (layer, 0, gate0 + kk * nj + j), kk=kk))
                for kk in range(N_BRANCH)]
    yp_specs = [pl.BlockSpec((tm, MIX_W), lambda i, j: (jnp.minimum(i, MP // MS - 1), 0)) for _ in range(N_BRANCH)]
    ys_specs = [pl.BlockSpec((tm, MIX_W), lambda i, j: (0, 0)) for _ in range(N_BRANCH)]
    return pl.pallas_call(
        _gate_merge_kernel,
        grid=(M // tm, nj),
        in_specs=[pl.BlockSpec((tm, D_MODEL), lambda i, j: (i, 0)),
                  pl.BlockSpec((1, D_MODEL), lambda i, j: (0, 0))] + wg_specs + yp_specs + ys_specs
                 + [pl.BlockSpec((1, N_BRANCH, MIX_W, tn), lambda i, j: (layer, 0, 0, j))],
        out_specs=pl.BlockSpec((tm, tn), lambda i, j: (i, j)),
        out_shape=jax.ShapeDtypeStruct((M, D_MODEL), BF16),
        scratch_shapes=[pltpu.VMEM((tm, D_MODEL), BF16)],
        compiler_params=_params(2),
        name="gate_merge",
    )(x, g, wg, wg, wg, wg, *ys_prompt, *ys_sample, wb)


def _mm_res_kernel(a_ref, w_ref, r_ref, o_ref):
    o_ref[...] = r_ref[...] + _dot(a_ref[...], w_ref[0].astype(BF16))


def _mm_res(a, w, layer, res, tm, tn):
    m, k = a.shape
    n = w.shape[2]
    return pl.pallas_call(
        _mm_res_kernel,
        grid=(m // tm, n // tn),
        in_specs=[pl.BlockSpec((tm, k), lambda i, j: (i, 0)),
                  pl.BlockSpec((1, k, tn), lambda i, j: (layer, 0, j)),
                  pl.BlockSpec((tm, tn), lambda i, j: (i, j))],
        out_specs=pl.BlockSpec((tm, tn), lambda i, j: (i, j)),
        out_shape=jax.ShapeDtypeStruct((m, n), F32),
        compiler_params=_params(2),
        name="mm_res",
    )(a, w, res)


def _swiglu_kernel(x_ref, g_ref, wa_ref, wb_ref, o_ref, h_ref):
    @pl.when(pl.program_id(1) == 0)
    def _():
        h_ref[...] = _rms(x_ref[...], g_ref[...]).astype(BF16)

    h = h_ref[...]
    a = _dot(h, wa_ref[0].astype(BF16))
    b = _dot(h, wb_ref[0].astype(BF16))
    o_ref[...] = (a * _sigmoid(a) * b).astype(BF16)


def _swiglu(x, g, w, layer, tm, tn):
    nj = D_FF // tn
    return pl.pallas_call(
        _swiglu_kernel,
        grid=(M // tm, nj),
        in_specs=[pl.BlockSpec((tm, D_MODEL), lambda i, j: (i, 0)),
                  pl.BlockSpec((1, D_MODEL), lambda i, j: (0, 0)),
                  pl.BlockSpec((1, D_MODEL, tn), lambda i, j: (layer, 0, j)),
                  pl.BlockSpec((1, D_MODEL, tn), lambda i, j: (layer, 0, nj + j))],
        out_specs=pl.BlockSpec((tm, tn), lambda i, j: (i, j)),
        out_shape=jax.ShapeDtypeStruct((M, D_FF), BF16),
        scratch_shapes=[pltpu.VMEM((tm, D_MODEL), BF16)],
        compiler_params=_params(2),
        name="swiglu",
    )(x, g, w, w)


def _final_norm_kernel(x_ref, g_ref, o_ref):
    o_ref[...] = _rms(x_ref[...], g_ref[...])


def _final_norm(x, g, row0, rows, tm):
    return pl.pallas_call(
        _final_norm_kernel,
        grid=(rows // tm,),
        in_specs=[pl.BlockSpec((tm, D_MODEL), lambda i: (row0 // tm + i, 0)),
                  pl.BlockSpec((1, D_MODEL), lambda i: (0, 0))],
        out_specs=pl.BlockSpec((tm, D_MODEL), lambda i: (i, 0)),
        out_shape=jax.ShapeDtypeStruct((rows, D_MODEL), F32),
        compiler_params=_params(1),
        name="final_norm",
    )(x, g)


PREP_TM = 256


def _prep_kernel(q_ref, k_ref, v_ref, qi_ref, t_ref, tab_ref, qo, ko, kbo, vo, vbo, qio, to, ki2o):
    lane = lax.broadcasted_iota(I32, (PREP_TM, LANES), 1)

    def rope(x, table, half):
        c, s = tab_ref[2 * table], tab_ref[2 * table + 1]
        upper = (lane & half) != 0
        partner = jnp.where(upper, pltpu.roll(x, half, 1), -pltpu.roll(x, LANES - half, 1))
        return x * c + partner * s

    for h in range(N_HEADS):
        sl = slice(h * LANES, (h + 1) * LANES)
        head_rows = pl.ds(h, PREP_TM, stride=N_HEADS)
        qo[:, sl] = rope(q_ref[:, sl], 0, ROT_DIM // 2).astype(BF16)
        kr = rope(k_ref[:, sl], 0, ROT_DIM // 2)
        ko[head_rows, :] = kr
        kbo[:, sl] = kr.astype(BF16)
        v = v_ref[:, sl]
        vo[head_rows, :] = v
        vbo[:, sl] = v.astype(BF16)
        qio[:, sl] = rope(qi_ref[:, sl], 1, IDX_ROT // 2).astype(BF16)
    tr = rope(t_ref[...], 2, IDX_ROT // 2)
    to[...] = tr
    ki = jnp.where(lane < IDX_DIM, tr, 0.0)
    ki2o[:, :LANES] = ki.astype(BF16)
    ki2o[:, LANES:] = pltpu.roll(ki, IDX_DIM, 1).astype(BF16)


def _prep(proj, tables, row0, rows, table_block):
    b0 = row0 // PREP_TM

    def cb(c):
        return pl.BlockSpec((PREP_TM, MIX_W), lambda i: (b0 + i, c))

    row512 = pl.BlockSpec((PREP_TM, MIX_W), lambda i: (i, 0))
    head_rows = pl.BlockSpec((PREP_TM * N_HEADS, HEAD_DIM), lambda i: (i, 0))
    return pl.pallas_call(
        _prep_kernel,
        grid=(rows // PREP_TM,),
        in_specs=[cb(CB_Q), cb(CB_K), cb(CB_V), cb(CB_QI),
                  pl.BlockSpec((PREP_TM, LANES), lambda i: (b0 + i, TAIL_COL // LANES)),
                  pl.BlockSpec((6, PREP_TM, LANES), lambda i: (0, table_block(i), 0))],
        out_specs=[row512, head_rows, row512, head_rows, row512, row512,
                   pl.BlockSpec((PREP_TM, LANES), lambda i: (i, 0)),
                   pl.BlockSpec((PREP_TM, 2 * LANES), lambda i: (i, 0))],
        out_shape=[jax.ShapeDtypeStruct((rows, MIX_W), BF16),
                   jax.ShapeDtypeStruct((rows * N_HEADS, HEAD_DIM), F32),
                   jax.ShapeDtypeStruct((rows, MIX_W), BF16),
                   jax.ShapeDtypeStruct((rows * N_HEADS, HEAD_DIM), F32),
                   jax.ShapeDtypeStruct((rows, MIX_W), BF16),
                   jax.ShapeDtypeStruct((rows, MIX_W), BF16),
                   jax.ShapeDtypeStruct((rows, LANES), F32),
                   jax.ShapeDtypeStruct((rows, 2 * LANES), BF16)],
        compiler_params=_params(1),
        name="prep",
    )(proj, proj, proj, proj, proj, tables)


KEY_NEG_INF = INT_MIN + 0x7FFFFF


def _key_to_float(key):
    return lax.bitcast_convert_type(jnp.where(key < 0, key ^ 0x7FFFFFFF, key), F32)


def _select_chunks(sc_ref, bias_ref, nc, k, allowed_fn):
    _, rows, cw = sc_ref.shape
    kf = float(k)

    def count_ge(c):
        acc = jnp.where(sc_ref[0] >= c, 1.0, 0.0)
        for j in range(1, nc):
            acc = acc + jnp.where(sc_ref[j] >= c, 1.0, 0.0)
        return jnp.sum(acc, axis=-1, keepdims=True)

    t = jnp.where(count_ge(jnp.zeros((rows, 1), F32)) >= kf, 0, INT_MIN).astype(I32)
    for bit in range(30, -1, -1):
        cand = t + (1 << bit)
        ok = jnp.logical_or(count_ge(_key_to_float(cand)) >= kf, cand <= KEY_NEG_INF)
        t = jnp.where(ok, cand, t)
    lo = _key_to_float(t)
    hi = _key_to_float(t + 1)
    need = kf - count_ge(hi)
    r_io = lax.broadcasted_iota(I32, (LANES, LANES), 0)
    c_io = lax.broadcasted_iota(I32, (LANES, LANES), 1)
    tri = jnp.where(r_io <= c_io, 1.0, 0.0).astype(BF16)

    off = jnp.zeros((rows, 1), F32)
    for j in range(nc):
        for u in range(cw // LANES):
            sl = slice(u * LANES, (u + 1) * LANES)
            su = sc_ref[j, :, sl]
            above = su >= hi
            tied = su >= lo
            e = jnp.where(above, 0.0, jnp.where(tied, 1.0, 0.0))
            rank = _dot(e.astype(BF16), tri) + off
            off = off + jnp.sum(e, axis=-1, keepdims=True)
            bias = jnp.where(above, 0.0, jnp.where(tied, jnp.where(rank <= need, 0.0, NEG_INF), NEG_INF))
            allowed = allowed_fn(j, u, su)
            bias_ref[j, :, sl] = bias if allowed is None else jnp.where(allowed, bias, NEG_INF)


def _index_scores(qi, w_scaled, ki_a, ki_b):
    acc = None
    for p in range(IDX_HEADS // 2):
        qp = qi[:, p * LANES:(p + 1) * LANES]
        for half, ki in enumerate((ki_a, ki_b)):
            hh = 2 * p + half
            term = jnp.maximum(_dot_nt(qp, ki), 0.0) * w_scaled[:, IDX_DIM + hh:IDX_DIM + hh + 1]
            acc = term if acc is None else acc + term
    return acc


KEY_CHUNK = 256
N_KEY_CHUNKS = SEQ // KEY_CHUNK
CAUSAL_VARIANTS = 4


def _dsa_prompt_kernel(q_ref, qi_ref, t_ref, k_ref, v_ref, ki2_ref, o_ref, sc_ref, bias_ref):
    i = pl.program_id(1)
    row = i * QBLOCK + lax.broadcasted_iota(I32, (QBLOCK, KEY_CHUNK), 0)
    lane = lax.broadcasted_iota(I32, (QBLOCK, KEY_CHUNK), 1)
    row_g = i * QBLOCK + lax.broadcasted_iota(I32, (QBLOCK, LANES), 0)
    lane_g = lax.broadcasted_iota(I32, (QBLOCK, LANES), 1)

    def attend(n_keys, n_visible_chunks):
        nc = n_keys // KEY_CHUNK

        def causal(j, u, su):
            return None if j < n_visible_chunks else j * KEY_CHUNK + u * LANES + lane_g <= row_g

        w_scaled = (t_ref[...] * (IDX_HEADS ** -0.5)) * (IDX_DIM ** -0.5)
        sc = _index_scores(qi_ref[...], w_scaled, ki2_ref[:n_keys, :LANES], ki2_ref[:n_keys, LANES:])
        for j in range(nc):
            sc_j = sc[:, j * KEY_CHUNK:(j + 1) * KEY_CHUNK]
            sc_ref[j] = sc_j if j < n_visible_chunks else jnp.where(j * KEY_CHUNK + lane <= row, sc_j, NEG_INF)
        _select_chunks(sc_ref.at[:nc], bias_ref.at[:nc], nc, TOPK, causal)
        bias = jnp.concatenate([bias_ref[j] for j in range(nc)], axis=-1)
        for h in range(N_HEADS):
            sl = slice(h * HEAD_DIM, (h + 1) * HEAD_DIM)
            s = _dot_nt(q_ref[:, sl], k_ref[:n_keys, sl]) * (HEAD_DIM ** -0.5) + bias
            m = jnp.max(s, axis=-1, keepdims=True)
            p = jnp.exp(s - m)
            l = jnp.sum(p, axis=-1, keepdims=True)
            o_ref[:, sl] = (_dot(p.astype(BF16), v_ref[:n_keys, sl]) / l).astype(BF16)

    blocks_per_variant = (SEQ // QBLOCK) // CAUSAL_VARIANTS
    for v in range(CAUSAL_VARIANTS):
        @pl.when(i // blocks_per_variant == v)
        def _(v=v):
            attend((v + 1) * (SEQ // CAUSAL_VARIANTS), (v * blocks_per_variant * QBLOCK) // KEY_CHUNK)


def _dsa_prompt(q, qi, tail, kb, vb, ki2):
    nb = SEQ // QBLOCK

    def qspec(w):
        return pl.BlockSpec((QBLOCK, w), lambda b, i: (b * nb + i, 0))

    def kspec(w):
        return pl.BlockSpec((SEQ, w), lambda b, i: (b, 0))

    return pl.pallas_call(
        _dsa_prompt_kernel,
        grid=(BATCH, nb),
        in_specs=[qspec(MIX_W), qspec(MIX_W), qspec(LANES), kspec(MIX_W), kspec(MIX_W), kspec(2 * LANES)],
        out_specs=qspec(MIX_W),
        out_shape=jax.ShapeDtypeStruct((MP, MIX_W), BF16),
        scratch_shapes=[pltpu.VMEM((N_KEY_CHUNKS, QBLOCK, KEY_CHUNK), F32)] * 2,
        compiler_params=_params(2),
        name="dsa_prompt",
    )(q, qi, tail, kb, vb, ki2)


def _dsa_sample_scores_kernel(pt_ref, qi_ref, w_ref, *rest):
    page_refs, knew_ref, o_ref = rest[:N_PAGES], rest[N_PAGES], rest[N_PAGES + 1]
    qi = qi_ref[0]
    w = w_ref[0] * (IDX_HEADS ** -0.5)

    def chunk_scores(d):
        r = jnp.maximum(d * (IDX_DIM ** -0.5), 0.0) * w
        return jnp.sum(r.reshape(DEC_SEQ, IDX_HEADS, LANES), axis=1)

    for p in range(N_PAGES):
        o_ref[0, :, p * LANES:(p + 1) * LANES] = chunk_scores(_dot(qi, page_refs[p][0, 0].astype(BF16)))
    new = chunk_scores(_dot_nt(qi, knew_ref[0]))
    tq = lax.broadcasted_iota(I32, (DEC_SEQ, LANES), 0)
    jk = lax.broadcasted_iota(I32, (DEC_SEQ, LANES), 1)
    o_ref[0, :, PAST:] = jnp.where(jk <= tq, new, NEG_INF)


def _dsa_sample_scores(page_table, qi32, w32, cache_kidx_t, layer, knew_pad):
    page_specs = [pl.BlockSpec((1, 1, IDX_DIM, PAGE_SIZE),
                               functools.partial(lambda b, pt, p: (layer, pt[b, p], 0, 0), p=p))
                  for p in range(N_PAGES)]
    grid_spec = pltpu.PrefetchScalarGridSpec(
        num_scalar_prefetch=1,
        grid=(DEC_BATCH,),
        in_specs=[pl.BlockSpec((1, DEC_SEQ * IDX_HEADS, IDX_DIM), lambda b, pt: (b, 0, 0)),
                  pl.BlockSpec((1, DEC_SEQ * IDX_HEADS, 1), lambda b, pt: (b, 0, 0))] + page_specs
                 + [pl.BlockSpec((1, LANES, IDX_DIM), lambda b, pt: (b, 0, 0))],
        out_specs=pl.BlockSpec((1, DEC_SEQ, SAMPLE_KEYS), lambda b, pt: (b, 0, 0)),
    )
    return pl.pallas_call(
        _dsa_sample_scores_kernel,
        grid_spec=grid_spec,
        out_shape=jax.ShapeDtypeStruct((DEC_BATCH, DEC_SEQ, SAMPLE_KEYS), F32),
        compiler_params=_params(1),
        name="dsa_sample_scores",
    )(page_table, qi32, w32, *([cache_kidx_t] * N_PAGES), knew_pad)


SEL_ROWS = 128
N_SAMPLE_CHUNKS = SAMPLE_KEYS // LANES


def _sample_select_kernel(in_ref, o_ref, sc_ref, bias_ref):
    for c in range(N_SAMPLE_CHUNKS):
        sc_ref[c] = in_ref[:, c * LANES:(c + 1) * LANES]
    _select_chunks(sc_ref, bias_ref, N_SAMPLE_CHUNKS, TOPK, lambda j, u, su: su > NEG_INF)
    for c in range(N_SAMPLE_CHUNKS):
        o_ref[:, c * LANES:(c + 1) * LANES] = bias_ref[c]


def _sample_select(sc):
    return pl.pallas_call(
        _sample_select_kernel,
        grid=(MS // SEL_ROWS,),
        in_specs=[pl.BlockSpec((SEL_ROWS, SAMPLE_KEYS), lambda i: (i, 0))],
        out_specs=pl.BlockSpec((SEL_ROWS, SAMPLE_KEYS), lambda i: (i, 0)),
        out_shape=jax.ShapeDtypeStruct((MS, SAMPLE_KEYS), F32),
        scratch_shapes=[pltpu.VMEM((N_SAMPLE_CHUNKS, SEL_ROWS, LANES), F32)] * 2,
        compiler_params=_params(1),
        name="sample_select",
    )(sc)


def _dsa_sample_attn_kernel(pt_ref, q_ref, bias_ref, *rest):
    k_refs = rest[:N_PAGES]
    v_refs = rest[N_PAGES:2 * N_PAGES]
    knew_ref, vnew_ref, o_ref, s_ref = rest[2 * N_PAGES:]
    q = q_ref[0]
    def head_rows(h):
        return pl.ds(h, PAGE_SIZE, stride=N_HEADS)

    for h in range(N_HEADS):
        rs = slice(h * DEC_SEQ, (h + 1) * DEC_SEQ)
        hs = slice(h * HEAD_DIM, (h + 1) * HEAD_DIM)
        qh = q[:, hs]
        for p in range(N_PAGES):
            s_ref[rs, p * LANES:(p + 1) * LANES] = _dot_nt(qh, k_refs[p][0, 0, head_rows(h), :].astype(BF16))
        s_ref[rs, PAST:] = _dot_nt(qh, knew_ref[0, :, hs])
    bias = bias_ref[0]
    s = s_ref[...] * (HEAD_DIM ** -0.5) + jnp.concatenate([bias] * N_HEADS, axis=0)
    m = jnp.max(s, axis=-1, keepdims=True)
    pr = jnp.exp(s - m)
    l = jnp.sum(pr, axis=-1, keepdims=True)
    pb = pr.astype(BF16)
    for h in range(N_HEADS):
        rs = slice(h * DEC_SEQ, (h + 1) * DEC_SEQ)
        hs = slice(h * HEAD_DIM, (h + 1) * HEAD_DIM)
        acc = _dot(pb[rs, PAST:], vnew_ref[0, :, hs])
        for p in range(N_PAGES):
            acc = acc + _dot(pb[rs, p * LANES:(p + 1) * LANES], v_refs[p][0, 0, head_rows(h), :].astype(BF16))
        o_ref[0, :, hs] = (acc / l[rs]).astype(BF16)


def _dsa_sample_attn(page_table, q, bias, cache_k, cache_v, layer, knew_pad, vnew_pad):
    def page_spec(p):
        return pl.BlockSpec((1, 1, PAGE_SIZE * N_HEADS, HEAD_DIM),
                            functools.partial(lambda b, pt, p: (layer, pt[b, p], 0, 0), p=p))

    grid_spec = pltpu.PrefetchScalarGridSpec(
        num_scalar_prefetch=1,
        grid=(DEC_BATCH,),
        in_specs=[pl.BlockSpec((1, DEC_SEQ, MIX_W), lambda b, pt: (b, 0, 0)),
                  pl.BlockSpec((1, DEC_SEQ, SAMPLE_KEYS), lambda b, pt: (b, 0, 0))]
                 + [page_spec(p) for p in range(N_PAGES)] + [page_spec(p) for p in range(N_PAGES)]
                 + [pl.BlockSpec((1, LANES, MIX_W), lambda b, pt: (b, 0, 0)),
                    pl.BlockSpec((1, LANES, MIX_W), lambda b, pt: (b, 0, 0))],
        out_specs=pl.BlockSpec((1, DEC_SEQ, MIX_W), lambda b, pt: (b, 0, 0)),
        scratch_shapes=[pltpu.VMEM((N_HEADS * DEC_SEQ, SAMPLE_KEYS), F32)],
    )
    return pl.pallas_call(
        _dsa_sample_attn_kernel,
        grid_spec=grid_spec,
        out_shape=jax.ShapeDtypeStruct((DEC_BATCH, DEC_SEQ, MIX_W), BF16),
        compiler_params=_params(1),
        name="dsa_sample_attn",
    )(page_table, q, bias, *([cache_k] * N_PAGES), *([cache_v] * N_PAGES), knew_pad, vnew_pad)


CONV_TT = 512


def _conv_prompt_kernel(cx_ref, cb_ref, cc_ref, w_ref, yb_ref, buf_ref, carry_ref):
    j = pl.program_id(1)

    @pl.when(j == 0)
    def _():
        carry_ref[...] = jnp.zeros_like(carry_ref)

    z = cc_ref[...] * cx_ref[...]
    row = lax.broadcasted_iota(I32, z.shape, 0)
    c0 = carry_ref[0:1, :]
    c1 = carry_ref[1:2, :]
    zm1 = jnp.where(row == 0, c1, pltpu.roll(z, 1, 0))
    zm2 = jnp.where(row == 0, c0, jnp.where(row == 1, c1, pltpu.roll(z, 2, 0)))
    zc = w_ref[0:1, :] * zm2 + w_ref[1:2, :] * zm1 + w_ref[2:3, :] * z
    yb_ref[...] = (cb_ref[...] * zc).astype(BF16)
    last = z[CONV_TT - 2:CONV_TT, :]
    carry_ref[0:2, :] = last
    buf_ref[0] = last


def _conv_prompt(proj, conv_w):
    nt = SEQ // CONV_TT

    def cb(c):
        return pl.BlockSpec((CONV_TT, MIX_W), lambda b, j: (b * nt + j, c))

    return pl.pallas_call(
        _conv_prompt_kernel,
        grid=(BATCH, nt),
        in_specs=[cb(CB_CX), cb(CB_CB), cb(CB_CC), pl.BlockSpec((3, MIX_W), lambda b, j: (0, 0))],
        out_specs=[pl.BlockSpec((CONV_TT, MIX_W), lambda b, j: (b * nt + j, 0)),
                   pl.BlockSpec((1, 2, MIX_W), lambda b, j: (b, 0, 0))],
        out_shape=[jax.ShapeDtypeStruct((MP, MIX_W), BF16),
                   jax.ShapeDtypeStruct((BATCH, 2, MIX_W), F32)],
        scratch_shapes=[pltpu.VMEM((8, MIX_W), F32)],
        compiler_params=_params(2),
        name="conv_prompt",
    )(proj, proj, proj, conv_w)


def _gmlp_prompt_kernel(gu_ref, gv_ref, w_ref, bt_ref, o_ref):
    r_io = lax.broadcasted_iota(I32, (GMLP_CHUNK, GMLP_CHUNK), 0)
    c_io = lax.broadcasted_iota(I32, (GMLP_CHUNK, GMLP_CHUNK), 1)
    tril = c_io <= r_io
    for g in range(GMLP_GROUPS):
        sl = slice(g * LANES, (g + 1) * LANES)
        wm = jnp.where(tril, w_ref[g], 0.0).astype(BF16)
        for c in range(GMLP_STEP_CHUNKS):
            rows = slice(c * GMLP_CHUNK, (c + 1) * GMLP_CHUNK)
            z = _dot(wm, gv_ref[rows, sl].astype(BF16)) + bt_ref[:, g:g + 1]
            o_ref[rows, sl] = (gu_ref[rows, sl] * z).astype(BF16)


GMLP_STEP_CHUNKS = 4


def _gmlp_prompt(proj, gmlp_w, gmlp_bt):
    rows = GMLP_STEP_CHUNKS * GMLP_CHUNK

    def cb(c):
        return pl.BlockSpec((rows, MIX_W), lambda i: (i, c))

    return pl.pallas_call(
        _gmlp_prompt_kernel,
        grid=(MP // rows,),
        in_specs=[cb(CB_GU), cb(CB_GV),
                  pl.BlockSpec((GMLP_GROUPS, GMLP_CHUNK, GMLP_CHUNK), lambda i: (0, 0, 0)),
                  pl.BlockSpec((GMLP_CHUNK, GMLP_GROUPS), lambda i: (0, 0))],
        out_specs=pl.BlockSpec((rows, MIX_W), lambda i: (i, 0)),
        out_shape=jax.ShapeDtypeStruct((MP, MIX_W), BF16),
        compiler_params=_params(1),
        name="gmlp_prompt",
    )(proj, proj, gmlp_w, gmlp_bt)


def _mix_sample_kernel(cx_ref, cb_ref, cc_ref, gu_ref, gv_ref, buf_ref, cw_ref, gw_ref, gb_ref,
                       yb_ref, nbuf_ref, yd_ref):
    zf = [buf_ref[:, 0, :], buf_ref[:, 1, :]]
    for t in range(DEC_SEQ):
        zf.append(cc_ref[:, t, :] * cx_ref[:, t, :])
    for t in range(DEC_SEQ):
        zc = cw_ref[0:1, :] * zf[t] + cw_ref[1:2, :] * zf[t + 1] + cw_ref[2:3, :] * zf[t + 2]
        yb_ref[:, t, :] = (cb_ref[:, t, :] * zc).astype(BF16)
    nbuf_ref[:, 0, :] = zf[DEC_SEQ]
    nbuf_ref[:, 1, :] = zf[DEC_SEQ + 1]
    v = [gv_ref[:, s, :] for s in range(DEC_SEQ)]
    for t in range(DEC_SEQ):
        z = gb_ref[t:t + 1, :]
        for s in range(t + 1):
            z = z + gw_ref[t * DEC_SEQ + s:t * DEC_SEQ + s + 1, :] * v[s]
        yd_ref[:, t, :] = (gu_ref[:, t, :] * z).astype(BF16)


def _mix_sample(proj_s, state_conv_l, conv_w, gw_lane, gb_lane):
    def cb(c):
        return pl.BlockSpec((DEC_BATCH, DEC_SEQ, MIX_W), lambda i: (0, 0, c))

    full3 = pl.BlockSpec((DEC_BATCH, DEC_SEQ, MIX_W), lambda i: (0, 0, 0))
    buf3 = pl.BlockSpec((DEC_BATCH, 2, MIX_W), lambda i: (0, 0, 0))
    return pl.pallas_call(
        _mix_sample_kernel,
        grid=(1,),
        in_specs=[cb(CB_CX), cb(CB_CB), cb(CB_CC), cb(CB_GU), cb(CB_GV), buf3,
                  pl.BlockSpec((3, MIX_W), lambda i: (0, 0)),
                  pl.BlockSpec((DEC_SEQ * DEC_SEQ, MIX_W), lambda i: (0, 0)),
                  pl.BlockSpec((DEC_SEQ, MIX_W), lambda i: (0, 0))],
        out_specs=[full3, buf3, full3],
        out_shape=[jax.ShapeDtypeStruct((DEC_BATCH, DEC_SEQ, MIX_W), BF16),
                   jax.ShapeDtypeStruct((DEC_BATCH, 2, MIX_W), F32),
                   jax.ShapeDtypeStruct((DEC_BATCH, DEC_SEQ, MIX_W), BF16)],
        compiler_params=_params(1),
        name="mix_sample",
    )(proj_s, proj_s, proj_s, proj_s, proj_s, state_conv_l, conv_w, gw_lane, gb_lane)


def _ssm_params_kernel(ar_ref, ai_ref, ldt_ref, br_ref, bi_ref, ab_ref, bbr_ref, bbi_ref):
    ar, ai = ar_ref[...], ai_ref[...]
    dt = jnp.exp(ldt_ref[...])
    mag = jnp.exp(dt * ar)
    abar_re, abar_im = mag * jnp.cos(dt * ai), mag * jnp.sin(dt * ai)
    den = ar * ar + ai * ai
    nr, ni = abar_re - 1.0, abar_im
    coef_re = (nr * ar + ni * ai) / den
    coef_im = (ni * ar - nr * ai) / den
    ab_ref[...] = jnp.zeros_like(ab_ref)
    ab_ref[0:1, :] = abar_re
    ab_ref[1:2, :] = abar_im
    br, bi = br_ref[...], bi_ref[...]
    bbr_ref[...] = (coef_re * br - coef_im * bi).astype(BF16)
    bbi_ref[...] = (coef_re * bi + coef_im * br).astype(BF16)


def _ssm_params(a_re, a_im, ldt, bd_re, bd_im):
    row = pl.BlockSpec((1, SSM_STATE), lambda i: (0, 0))
    mat = pl.BlockSpec((MIX_W, SSM_STATE), lambda i: (0, 0))
    return pl.pallas_call(
        _ssm_params_kernel,
        grid=(1,),
        in_specs=[row, row, row, mat, mat],
        out_specs=[pl.BlockSpec((8, SSM_STATE), lambda i: (0, 0)), mat, mat],
        out_shape=[jax.ShapeDtypeStruct((8, SSM_STATE), F32),
                   jax.ShapeDtypeStruct((MIX_W, SSM_STATE), BF16),
                   jax.ShapeDtypeStruct((MIX_W, SSM_STATE), BF16)],
        compiler_params=_params(1),
        name="ssm_params",
    )(a_re, a_im, ldt, bd_re, bd_im)


def _gelu_tanh(x):
    return 0.5 * x * (1.0 + jnp.tanh(0.7978845608028654 * (x + 0.044715 * (x * x * x))))


SSM_BLOCK_STATES = SSM_STATE // (MIX_W // LANES)


def _ssm_drive(ub, bb_ref):
    return jnp.concatenate(
        [_dot(ub[:, c * LANES:(c + 1) * LANES],
              bb_ref[c * LANES:(c + 1) * LANES, c * SSM_BLOCK_STATES:(c + 1) * SSM_BLOCK_STATES])
         for c in range(MIX_W // LANES)], axis=-1)


def _ssm_readout(u, hr, hi, cr_ref, ci_ref, d_ref, wg_ref, bg_ref):
    hrb, hib = hr.astype(BF16), hi.astype(BF16)
    parts = []
    for c in range(MIX_W // LANES):
        ss = slice(c * SSM_BLOCK_STATES, (c + 1) * SSM_BLOCK_STATES)
        cs = slice(c * LANES, (c + 1) * LANES)
        parts.append(_dot(hrb[:, ss], cr_ref[ss, cs]) - _dot(hib[:, ss], ci_ref[ss, cs]))
    y = jnp.concatenate(parts, axis=-1) + d_ref[...] * u
    g = _gelu_tanh(y)
    return g * _sigmoid(_dot(g.astype(BF16), wg_ref[...]) + bg_ref[...])


SSM_TC = 128
SSM_SLABS = MIX_W // LANES


def _ssm_prompt_kernel(u0, u1, u2, u3, ab_ref, bbr_ref, bbi_ref, cr_ref, ci_ref, d_ref, wg_ref, bg_ref,
                       y_ref, str_ref, sti_ref, hr_ref, hi_ref, carry_ref, il_ref):
    j = pl.program_id(0)

    @pl.when(j == 0)
    def _():
        carry_ref[...] = jnp.zeros_like(carry_ref)

    def batch_rows(b):
        return pl.ds(b, SSM_TC, stride=BATCH)

    for b, u_ref in enumerate((u0, u1, u2, u3)):
        for c in range(SSM_SLABS):
            il_ref[c, batch_rows(b), :] = u_ref[:, c * LANES:(c + 1) * LANES]
    u = jnp.concatenate([il_ref[c] for c in range(SSM_SLABS)], axis=-1)
    ub = u.astype(BF16)
    hr_ref[...] = _ssm_drive(ub, bbr_ref)
    hi_ref[...] = _ssm_drive(ub, bbi_ref)
    ar, ai = ab_ref[0:1, :], ab_ref[1:2, :]

    def step2(t2, carry):
        pr, pi = carry
        rows = pl.ds(pl.multiple_of(t2 * 2 * BATCH, 2 * BATCH), 2 * BATCH)
        br, bi = hr_ref[rows, :], hi_ref[rows, :]
        er = ar * pr - ai * pi + br[:BATCH]
        ei = ar * pi + ai * pr + bi[:BATCH]
        nr = ar * er - ai * ei + br[BATCH:]
        ni = ar * ei + ai * er + bi[BATCH:]
        hr_ref[rows, :] = jnp.concatenate([er, nr], axis=0)
        hi_ref[rows, :] = jnp.concatenate([ei, ni], axis=0)
        return nr, ni

    fr, fi = lax.fori_loop(0, SSM_TC // 2, step2, (carry_ref[0:BATCH, :], carry_ref[BATCH:2 * BATCH, :]), unroll=4)
    carry_ref[0:BATCH, :] = fr
    carry_ref[BATCH:2 * BATCH, :] = fi
    str_ref[...] = fr
    sti_ref[...] = fi
    y = _ssm_readout(u, hr_ref[...], hi_ref[...], cr_ref, ci_ref, d_ref, wg_ref, bg_ref)
    for c in range(SSM_SLABS):
        il_ref[c] = y[:, c * LANES:(c + 1) * LANES]
    for b in range(BATCH):
        for c in range(SSM_SLABS):
            y_ref[b, :, c * LANES:(c + 1) * LANES] = il_ref[c, batch_rows(b), :].astype(BF16)


def _ssm_prompt(proj, ab, bbr, bbi, cdr, cdi, d, wglu, bglu):
    nt = SEQ // SSM_TC

    def const(shape):
        return pl.BlockSpec(shape, lambda j: (0,) * len(shape))

    u_specs = [pl.BlockSpec((SSM_TC, MIX_W), functools.partial(lambda j, b: (b * nt + j, CB_SU), b=b))
               for b in range(BATCH)]
    state = pl.BlockSpec((BATCH, SSM_STATE), lambda j: (0, 0))
    return pl.pallas_call(
        _ssm_prompt_kernel,
        grid=(nt,),
        in_specs=u_specs + [const((8, SSM_STATE)), const((MIX_W, SSM_STATE)), const((MIX_W, SSM_STATE)),
                            const((SSM_STATE, MIX_W)), const((SSM_STATE, MIX_W)), const((1, MIX_W)),
                            const((MIX_W, MIX_W)), const((1, MIX_W))],
        out_specs=[pl.BlockSpec((BATCH, SSM_TC, MIX_W), lambda j: (0, j, 0)), state, state],
        out_shape=[jax.ShapeDtypeStruct((BATCH, SEQ, MIX_W), BF16),
                   jax.ShapeDtypeStruct((BATCH, SSM_STATE), F32),
                   jax.ShapeDtypeStruct((BATCH, SSM_STATE), F32)],
        scratch_shapes=[pltpu.VMEM((BATCH * SSM_TC, SSM_STATE), F32), pltpu.VMEM((BATCH * SSM_TC, SSM_STATE), F32),
                        pltpu.VMEM((2 * BATCH, SSM_STATE), F32),
                        pltpu.VMEM((SSM_SLABS, BATCH * SSM_TC, LANES), F32)],
        compiler_params=_params(1),
        name="ssm_prompt",
    )(proj, proj, proj, proj, ab, bbr, bbi, cdr, cdi, d, wglu, bglu)


def _ssm_sample_kernel(u_ref, h0r_ref, h0i_ref, ab_ref, bbr_ref, bbi_ref, cr_ref, ci_ref, d_ref, wg_ref, bg_ref,
                       y_ref, nr_ref, ni_ref):
    ar, ai = ab_ref[0:1, :], ab_ref[1:2, :]
    hr, hi = h0r_ref[...], h0i_ref[...]
    for t in range(DEC_SEQ):
        u = u_ref[:, t, :]
        ub = u.astype(BF16)
        hr, hi = (ar * hr - ai * hi + _ssm_drive(ub, bbr_ref),
                  ar * hi + ai * hr + _ssm_drive(ub, bbi_ref))
        y_ref[:, t, :] = _ssm_readout(u, hr, hi, cr_ref, ci_ref, d_ref, wg_ref, bg_ref).astype(BF16)
    nr_ref[...] = hr
    ni_ref[...] = hi


def _ssm_sample(proj_s, h0r, h0i, ab, bbr, bbi, cdr, cdi, d, wglu, bglu):
    def const(shape):
        return pl.BlockSpec(shape, lambda i: (0,) * len(shape))

    st = const((DEC_BATCH, SSM_STATE))
    return pl.pallas_call(
        _ssm_sample_kernel,
        grid=(1,),
        in_specs=[pl.BlockSpec((DEC_BATCH, DEC_SEQ, MIX_W), lambda i: (0, 0, CB_SU)), st, st,
                  const((8, SSM_STATE)), const((MIX_W, SSM_STATE)), const((MIX_W, SSM_STATE)),
                  const((SSM_STATE, MIX_W)), const((SSM_STATE, MIX_W)), const((1, MIX_W)),
                  const((MIX_W, MIX_W)), const((1, MIX_W))],
        out_specs=[const((DEC_BATCH, DEC_SEQ, MIX_W)), st, st],
        out_shape=[jax.ShapeDtypeStruct((DEC_BATCH, DEC_SEQ, MIX_W), BF16),
                   jax.ShapeDtypeStruct((DEC_BATCH, SSM_STATE), F32),
                   jax.ShapeDtypeStruct((DEC_BATCH, SSM_STATE), F32)],
        compiler_params=_params(1),
        name="ssm_sample",
    )(proj_s, h0r, h0i, ab, bbr, bbi, cdr, cdi, d, wglu, bglu)


def _rope_tables():
    lane = jnp.arange(LANES)

    def inv(half):
        return ROPE_THETA ** (-jnp.arange(half, dtype=F32) / half)

    inv_qk = jnp.where(lane < ROT_DIM, inv(ROT_DIM // 2)[lane % (ROT_DIM // 2)], 0.0)
    idx = inv(IDX_ROT // 2)[lane % (IDX_ROT // 2)]
    inv_i = jnp.where((lane % IDX_DIM) < IDX_ROT, idx, 0.0)
    inv_t = jnp.where(lane < IDX_ROT, idx, 0.0)
    pos = jnp.concatenate([jnp.arange(SEQ), PAST + jnp.arange(PREP_TM) % DEC_SEQ]).astype(F32)
    tabs = []
    for inv_row in (inv_qk, inv_i, inv_t):
        ang = pos[:, None] * inv_row[None, :].astype(F32)
        tabs += [jnp.cos(ang), jnp.sin(ang)]
    return jnp.stack(tabs)


def _pad_rows(a, rows):
    return jnp.pad(a, ((0, 0), (0, rows - a.shape[1]), (0, 0)))


def kernel(x_prompt, x_sample, cache_k, cache_v, cache_kidx, state_conv, state_ssm_re, state_ssm_im, page_table,
           norm_mix, w_in, conv_w, ssm_a_re, ssm_a_im, ssm_log_dt, ssm_b_re, ssm_b_im, ssm_c_re, ssm_c_im, ssm_d,
           w_glu, b_glu, gmlp_w, gmlp_b, w_branch, w_out, norm_ffn, w_ffn_in, w_ffn_out, norm_final):
    x = jnp.concatenate([x_prompt.reshape(MP, D_MODEL), x_sample.reshape(MS, D_MODEL)], axis=0)
    tables = _rope_tables()
    eye_g = jnp.eye(SSM_GROUPS, dtype=F32)
    cache_kidx_t = jnp.swapaxes(cache_kidx, 2, 3)
    n_pool = cache_k.shape[1]
    cache_k2 = cache_k.reshape(DEPTH, n_pool, PAGE_SIZE * N_HEADS, HEAD_DIM)
    cache_v2 = cache_v.reshape(DEPTH, n_pool, PAGE_SIZE * N_HEADS, HEAD_DIM)
    w_ffn_out_b = w_ffn_out.astype(BF16)
    assert (w_in.shape[2] - N_BRANCH * D_MODEL + W_SHIFT) % W_TILE == 0
    w_head = w_in[:, :, :C_TAIL + W_TILE].astype(BF16)
    w_shift = jnp.pad(w_in, ((0, 0), (0, 0), (W_SHIFT, 0))).astype(BF16)

    st_p, st_s = [], []
    for l in range(DEPTH):
        proj = _norm_mm(x, norm_mix[l][None, :], w_head, w_shift, l, 1088)
        q_p, k_p, kb_p, v_p, vb_p, qi_p, tail_p, ki2_p = _prep(
            proj, tables, 0, MP, lambda i: i % (SEQ // PREP_TM))
        q_s, k_s, kb_s, v_s, vb_s, qi_s, tail_s, _ = _prep(
            proj, tables, MP, MS, lambda i: SEQ // PREP_TM)

        ya_p = _dsa_prompt(q_p, qi_p, tail_p, kb_p, vb_p, ki2_p)
        tail_s3 = tail_s.reshape(DEC_BATCH, DEC_SEQ, LANES)
        w_s = tail_s3[:, :, IDX_DIM:IDX_DIM + IDX_HEADS].reshape(DEC_BATCH, DEC_SEQ * IDX_HEADS, 1)
        kinew = _pad_rows(tail_s3[:, :, :IDX_DIM].astype(BF16), LANES)
        sc_s = _dsa_sample_scores(page_table, qi_s.reshape(DEC_BATCH, DEC_SEQ * IDX_HEADS, IDX_DIM), w_s,
                                  cache_kidx_t, l, kinew)
        bias_s = _sample_select(sc_s.reshape(MS, SAMPLE_KEYS)).reshape(DEC_BATCH, DEC_SEQ, SAMPLE_KEYS)
        knew = _pad_rows(kb_s.reshape(DEC_BATCH, DEC_SEQ, MIX_W), LANES)
        vnew = _pad_rows(vb_s.reshape(DEC_BATCH, DEC_SEQ, MIX_W), LANES)
        ya_s = _dsa_sample_attn(page_table, q_s.reshape(DEC_BATCH, DEC_SEQ, MIX_W), bias_s, cache_k2, cache_v2, l,
                                knew, vnew)

        proj_s = proj[MP:].reshape(DEC_BATCH, DEC_SEQ, N_MAIN)
        yb_p, buf_p = _conv_prompt(proj, conv_w[l])
        yd_p = _gmlp_prompt(proj, gmlp_w[l], gmlp_b[l].T)
        gw_lane = jnp.repeat(gmlp_w[l][:, :DEC_SEQ, :DEC_SEQ].transpose(1, 2, 0).reshape(DEC_SEQ * DEC_SEQ, GMLP_GROUPS),
                             LANES, axis=1)
        gb_lane = jnp.repeat(gmlp_b[l][:, :DEC_SEQ].T, LANES, axis=1)
        yb_s, buf_s, yd_s = _mix_sample(proj_s, state_conv[l], conv_w[l], gw_lane, gb_lane)

        bd_re = jnp.einsum('gnp,gh->gphn', ssm_b_re[l], eye_g).reshape(MIX_W, SSM_STATE)
        bd_im = jnp.einsum('gnp,gh->gphn', ssm_b_im[l], eye_g).reshape(MIX_W, SSM_STATE)
        cd_re = jnp.einsum('gpn,gh->gnhp', ssm_c_re[l], eye_g).reshape(SSM_STATE, MIX_W).astype(BF16)
        cd_im = jnp.einsum('gpn,gh->gnhp', ssm_c_im[l], eye_g).reshape(SSM_STATE, MIX_W).astype(BF16)
        ldt = jnp.repeat(ssm_log_dt[l], SSM_N)[None, :]
        ab, bbr, bbi = _ssm_params(ssm_a_re[l].reshape(1, SSM_STATE), ssm_a_im[l].reshape(1, SSM_STATE), ldt,
                                   bd_re, bd_im)
        ssm_consts = (ab, bbr, bbi, cd_re, cd_im, ssm_d[l][None, :], w_glu[l].astype(BF16), b_glu[l][None, :])
        yc_p, hre_p, him_p = _ssm_prompt(proj, *ssm_consts)
        yc_p = yc_p.reshape(MP, MIX_W)
        yc_s, nre_s, nim_s = _ssm_sample(proj_s, state_ssm_re[l].reshape(DEC_BATCH, SSM_STATE),
                                         state_ssm_im[l].reshape(DEC_BATCH, SSM_STATE), *ssm_consts)

        ys_sample = tuple(y.reshape(MS, MIX_W) for y in (ya_s, yb_s, yc_s, yd_s))
        merged = _gate_merge(x, norm_mix[l][None, :], w_shift, (ya_p, yb_p, yc_p, yd_p), ys_sample, w_branch, l,
                             W_TILE)
        x = _mm_res(merged, w_out, l, x, 1088, 512)
        act = _swiglu(x, norm_ffn[l][None, :], w_ffn_in, l, 1088, 512)
        x = _mm_res(act, w_ffn_out_b, l, x, 1088, 256)

        gv_s = proj[MP:, CB_GV * MIX_W:(CB_GV + 1) * MIX_W]
        st_p.append((k_p.reshape(BATCH, SEQ, N_HEADS, HEAD_DIM), v_p.reshape(BATCH, SEQ, N_HEADS, HEAD_DIM),
                     tail_p[:, :IDX_DIM].reshape(BATCH, SEQ, IDX_DIM), buf_p,
                     hre_p.reshape(BATCH, SSM_GROUPS, SSM_N), him_p.reshape(BATCH, SSM_GROUPS, SSM_N)))
        st_s.append((k_s.reshape(DEC_BATCH, DEC_SEQ, N_HEADS, HEAD_DIM),
                     v_s.reshape(DEC_BATCH, DEC_SEQ, N_HEADS, HEAD_DIM),
                     tail_s[:, :IDX_DIM].reshape(DEC_BATCH, DEC_SEQ, IDX_DIM), buf_s,
                     nre_s.reshape(DEC_BATCH, SSM_GROUPS, SSM_N), nim_s.reshape(DEC_BATCH, SSM_GROUPS, SSM_N),
                     gv_s.reshape(DEC_BATCH, DEC_SEQ, MIX_W)))

    y_p = _final_norm(x, norm_final[None, :], 0, MP, MS)
    y_s = _final_norm(x, norm_final[None, :], MP, MS, MS)
    outs = [y_p.reshape(BATCH, SEQ, D_MODEL), y_s.reshape(DEC_BATCH, DEC_SEQ, D_MODEL)]
    outs += [jnp.stack([s[i] for s in st_p]) for i in range(6)]
    outs += [jnp.stack([s[i] for s in st_s]) for i in range(7)]
    return tuple(outs)
```

```python
import functools

import jax
import jax.numpy as jnp
from jax import lax
from jax.experimental import pallas as pl
from jax.experimental.pallas import tpu as pltpu

F32 = jnp.float32
BF16 = jnp.bfloat16
I32 = jnp.int32

D_MODEL = 2048
BATCH = 4
SEQ = 2048
DEPTH = 2
DEC_BATCH = 128
DEC_SEQ = 4
PAGE_SIZE = 128
N_PAGES = 16
PAST = N_PAGES * PAGE_SIZE
MIX_W = 512
N_HEADS = 4
HEAD_DIM = 128
ROT_DIM = 32
IDX_HEADS = 8
IDX_DIM = 64
IDX_ROT = 16
TOPK = 256
QBLOCK = 128
ROPE_THETA = 500000.0
SSM_P = 16
SSM_GROUPS = 32
SSM_N = 64
SSM_STATE = SSM_GROUPS * SSM_N
GMLP_CHUNK = 128
GMLP_GROUPS = 4
D_FF = 5632
N_BRANCH = 4

MP = BATCH * SEQ
MS = DEC_BATCH * DEC_SEQ
M = MP + MS
LANES = 128
TAIL_COL = 10 * MIX_W
N_MAIN = TAIL_COL + 2 * LANES
CB_Q, CB_K, CB_V, CB_QI, CB_CX, CB_CB, CB_CC, CB_SU, CB_GU, CB_GV = range(10)
SAMPLE_KEYS = PAST + LANES
NEG_INF = float("-inf")
INT_MIN = -2 ** 31
VMEM_LIMIT = 56 * 1024 * 1024


def _params(n_axes, vmem=VMEM_LIMIT):
    return pltpu.CompilerParams(dimension_semantics=("arbitrary",) * n_axes, vmem_limit_bytes=vmem)


def _rms(x, g):
    return x * lax.rsqrt(jnp.mean(x * x, axis=-1, keepdims=True) + 1e-6) * g


def _dot(a, b):
    return jnp.dot(a, b, preferred_element_type=F32)


def _dot_nt(a, b):
    return lax.dot_general(a, b, (((1,), (1,)), ((), ())), preferred_element_type=F32)


def _sigmoid(x):
    return 1.0 / (1.0 + jnp.exp(-x))


def _norm_mm_kernel(x_ref, g_ref, w_ref, o_ref, h_ref):
    @pl.when(pl.program_id(1) == 0)
    def _():
        h_ref[...] = _rms(x_ref[...], g_ref[...]).astype(BF16)

    o_ref[...] = _dot(h_ref[...], w_ref[...])


def _norm_mm(x, g, w, tm, tn):
    m, k = x.shape
    n = w.shape[1]
    return pl.pallas_call(
        _norm_mm_kernel,
        grid=(m // tm, n // tn),
        in_specs=[pl.BlockSpec((tm, k), lambda i, j: (i, 0)),
                  pl.BlockSpec((1, k), lambda i, j: (0, 0)),
                  pl.BlockSpec((k, tn), lambda i, j: (0, j))],
        out_specs=pl.BlockSpec((tm, tn), lambda i, j: (i, j)),
        out_shape=jax.ShapeDtypeStruct((m, n), F32),
        scratch_shapes=[pltpu.VMEM((tm, k), BF16)],
        compiler_params=_params(2),
        name="norm_mm",
    )(x, g, w)


def _gate_merge_kernel(x_ref, g_ref, wg0, wg1, wg2, wg3, yp0, yp1, yp2, yp3, ys0, ys1, ys2, ys3, wb_ref,
                       o_ref, h_ref):
    @pl.when(pl.program_id(1) == 0)
    def _():
        h_ref[...] = _rms(x_ref[...], g_ref[...]).astype(BF16)

    h = h_ref[...]
    is_prompt = pl.program_id(0) < MP // MS
    acc = None
    for kk, (wg, yp, ys) in enumerate(((wg0, yp0, ys0), (wg1, yp1, ys1), (wg2, yp2, ys2), (wg3, yp3, ys3))):
        gate = _sigmoid(_dot(h, wg[...]))
        y = jnp.where(is_prompt, yp[...], ys[...])
        term = gate * _dot(y, wb_ref[0, kk].astype(BF16))
        acc = term if acc is None else acc + term
    o_ref[...] = acc.astype(BF16)


def _gate_merge(x, g, wg, ys_prompt, ys_sample, wb, layer, tn):
    tm = MS
    nj = D_MODEL // tn
    wg_specs = [pl.BlockSpec((D_MODEL, tn), functools.partial(lambda i, j, kk: (0, kk * nj + j), kk=kk))
                for kk in range(N_BRANCH)]
    yp_specs = [pl.BlockSpec((tm, MIX_W), lambda i, j: (jnp.minimum(i, MP // MS - 1), 0)) for _ in range(N_BRANCH)]
    ys_specs = [pl.BlockSpec((tm, MIX_W), lambda i, j: (0, 0)) for _ in range(N_BRANCH)]
    return pl.pallas_call(
        _gate_merge_kernel,
        grid=(M // tm, nj),
        in_specs=[pl.BlockSpec((tm, D_MODEL), lambda i, j: (i, 0)),
                  pl.BlockSpec((1, D_MODEL), lambda i, j: (0, 0))] + wg_specs + yp_specs + ys_specs
                 + [pl.BlockSpec((1, N_BRANCH, MIX_W, tn), lambda i, j: (layer, 0, 0, j))],
        out_specs=pl.BlockSpec((tm, tn), lambda i, j: (i, j)),
        out_shape=jax.ShapeDtypeStruct((M, D_MODEL), BF16),
        scratch_shapes=[pltpu.VMEM((tm, D_MODEL), BF16)],
        compiler_params=_params(2),
        name="gate_merge",
    )(x, g, wg, wg, wg, wg, *ys_prompt, *ys_sample, wb)


def _mm_res_kernel(a_ref, w_ref, r_ref, o_ref):
    o_ref[...] = r_ref[...] + _dot(a_ref[...], w_ref[0].astype(BF16))


def _mm_res(a, w, layer, res, tm, tn):
    m, k = a.shape
    n = w.shape[2]
    return pl.pallas_call(
        _mm_res_kernel,
        grid=(m // tm, n // tn),
        in_specs=[pl.BlockSpec((tm, k), lambda i, j: (i, 0)),
                  pl.BlockSpec((1, k, tn), lambda i, j: (layer, 0, j)),
                  pl.BlockSpec((tm, tn), lambda i, j: (i, j))],
        out_specs=pl.BlockSpec((tm, tn), lambda i, j: (i, j)),
        out_shape=jax.ShapeDtypeStruct((m, n), F32),
        compiler_params=_params(2),
        name="mm_res",
    )(a, w, res)


def _swiglu_kernel(x_ref, g_ref, wa_ref, wb_ref, o_ref, h_ref):
    @pl.when(pl.program_id(1) == 0)
    def _():
        h_ref[...] = _rms(x_ref[...], g_ref[...]).astype(BF16)

    h = h_ref[...]
    a = _dot(h, wa_ref[0].astype(BF16))
    b = _dot(h, wb_ref[0].astype(BF16))
    o_ref[...] = (a * _sigmoid(a) * b).astype(BF16)


def _swiglu(x, g, w, layer, tm, tn):
    nj = D_FF // tn
    return pl.pallas_call(
        _swiglu_kernel,
        grid=(M // tm, nj),
        in_specs=[pl.BlockSpec((tm, D_MODEL), lambda i, j: (i, 0)),
                  pl.BlockSpec((1, D_MODEL), lambda i, j: (0, 0)),
                  pl.BlockSpec((1, D_MODEL, tn), lambda i, j: (layer, 0, j)),
                  pl.BlockSpec((1, D_MODEL, tn), lambda i, j: (layer, 0, nj + j))],
        out_specs=pl.BlockSpec((tm, tn), lambda i, j: (i, j)),
        out_shape=jax.ShapeDtypeStruct((M, D_FF), BF16),
        scratch_shapes=[pltpu.VMEM((tm, D_MODEL), BF16)],
        compiler_params=_params(2),
        name="swiglu",
    )(x, g, w, w)


def _final_norm_kernel(x_ref, g_ref, o_ref):
    o_ref[...] = _rms(x_ref[...], g_ref[...])


def _final_norm(x, g, row0, rows, tm):
    return pl.pallas_call(
        _final_norm_kernel,
        grid=(rows // tm,),
        in_specs=[pl.BlockSpec((tm, D_MODEL), lambda i: (row0 // tm + i, 0)),
                  pl.BlockSpec((1, D_MODEL), lambda i: (0, 0))],
        out_specs=pl.BlockSpec((tm, D_MODEL), lambda i: (i, 0)),
        out_shape=jax.ShapeDtypeStruct((rows, D_MODEL), F32),
        compiler_params=_params(1),
        name="final_norm",
    )(x, g)


PREP_TM = 256


def _prep_kernel(q_ref, k_ref, v_ref, qi_ref, t_ref, tab_ref, qo, ko, kbo, vo, vbo, qio, to, ki2o):
    lane = lax.broadcasted_iota(I32, (PREP_TM, LANES), 1)

    def rope(x, table, half):
        c, s = tab_ref[2 * table], tab_ref[2 * table + 1]
        upper = (lane & half) != 0
        partner = jnp.where(upper, pltpu.roll(x, half, 1), -pltpu.roll(x, LANES - half, 1))
        return x * c + partner * s

    for h in range(N_HEADS):
        sl = slice(h * LANES, (h + 1) * LANES)
        head_rows = pl.ds(h, PREP_TM, stride=N_HEADS)
        qo[:, sl] = rope(q_ref[:, sl], 0, ROT_DIM // 2).astype(BF16)
        kr = rope(k_ref[:, sl], 0, ROT_DIM // 2)
        ko[head_rows, :] = kr
        kbo[:, sl] = kr.astype(BF16)
        v = v_ref[:, sl]
        vo[head_rows, :] = v
        vbo[:, sl] = v.astype(BF16)
        qio[:, sl] = rope(qi_ref[:, sl], 1, IDX_ROT // 2).astype(BF16)
    tr = rope(t_ref[...], 2, IDX_ROT // 2)
    to[...] = tr
    ki = jnp.where(lane < IDX_DIM, tr, 0.0)
    ki2o[:, :LANES] = ki.astype(BF16)
    ki2o[:, LANES:] = pltpu.roll(ki, IDX_DIM, 1).astype(BF16)


def _prep(proj, tables, row0, rows, table_block):
    b0 = row0 // PREP_TM

    def cb(c):
        return pl.BlockSpec((PREP_TM, MIX_W), lambda i: (b0 + i, c))

    row512 = pl.BlockSpec((PREP_TM, MIX_W), lambda i: (i, 0))
    head_rows = pl.BlockSpec((PREP_TM * N_HEADS, HEAD_DIM), lambda i: (i, 0))
    return pl.pallas_call(
        _prep_kernel,
        grid=(rows // PREP_TM,),
        in_specs=[cb(CB_Q), cb(CB_K), cb(CB_V), cb(CB_QI),
                  pl.BlockSpec((PREP_TM, LANES), lambda i: (b0 + i, TAIL_COL // LANES)),
                  pl.BlockSpec((6, PREP_TM, LANES), lambda i: (0, table_block(i), 0))],
        out_specs=[row512, head_rows, row512, head_rows, row512, row512,
                   pl.BlockSpec((PREP_TM, LANES), lambda i: (i, 0)),
                   pl.BlockSpec((PREP_TM, 2 * LANES), lambda i: (i, 0))],
        out_shape=[jax.ShapeDtypeStruct((rows, MIX_W), BF16),
                   jax.ShapeDtypeStruct((rows * N_HEADS, HEAD_DIM), F32),
                   jax.ShapeDtypeStruct((rows, MIX_W), BF16),
                   jax.ShapeDtypeStruct((rows * N_HEADS, HEAD_DIM), F32),
                   jax.ShapeDtypeStruct((rows, MIX_W), BF16),
                   jax.ShapeDtypeStruct((rows, MIX_W), BF16),
                   jax.ShapeDtypeStruct((rows, LANES), F32),
                   jax.ShapeDtypeStruct((rows, 2 * LANES), BF16)],
        compiler_params=_params(1),
        name="prep",
    )(proj, proj, proj, proj, proj, tables)


KEY_NEG_INF = INT_MIN + 0x7FFFFF


def _key_to_float(key):
    return lax.bitcast_convert_type(jnp.where(key < 0, key ^ 0x7FFFFFFF, key), F32)


def _select_chunks(sc_ref, bias_ref, nc, k, allowed_fn):
    _, rows, cw = sc_ref.shape
    kf = float(k)

    def count_ge(c):
        acc = jnp.where(sc_ref[0] >= c, 1.0, 0.0)
        for j in range(1, nc):
            acc = acc + jnp.where(sc_ref[j] >= c, 1.0, 0.0)
        return jnp.sum(acc, axis=-1, keepdims=True)

    t = jnp.where(count_ge(jnp.zeros((rows, 1), F32)) >= kf, 0, INT_MIN).astype(I32)
    for bit in range(30, -1, -1):
        cand = t + (1 << bit)
        ok = jnp.logical_or(count_ge(_key_to_float(cand)) >= kf, cand <= KEY_NEG_INF)
        t = jnp.where(ok, cand, t)
    lo = _key_to_float(t)
    hi = _key_to_float(t + 1)
    need = kf - count_ge(hi)
    r_io = lax.broadcasted_iota(I32, (LANES, LANES), 0)
    c_io = lax.broadcasted_iota(I32, (LANES, LANES), 1)
    tri = jnp.where(r_io <= c_io, 1.0, 0.0).astype(BF16)

    off = jnp.zeros((rows, 1), F32)
    for j in range(nc):
        for u in range(cw // LANES):
            sl = slice(u * LANES, (u + 1) * LANES)
            su = sc_ref[j, :, sl]
            above = su >= hi
            tied = su >= lo
            e = jnp.where(above, 0.0, jnp.where(tied, 1.0, 0.0))
            rank = _dot(e.astype(BF16), tri) + off
            off = off + jnp.sum(e, axis=-1, keepdims=True)
            bias = jnp.where(above, 0.0, jnp.where(tied, jnp.where(rank <= need, 0.0, NEG_INF), NEG_INF))
            allowed = allowed_fn(j, u, su)
            bias_ref[j, :, sl] = bias if allowed is None else jnp.where(allowed, bias, NEG_INF)


def _index_scores(qi, w_scaled, ki_a, ki_b):
    acc = None
    for p in range(IDX_HEADS // 2):
        qp = qi[:, p * LANES:(p + 1) * LANES]
        for half, ki in enumerate((ki_a, ki_b)):
            hh = 2 * p + half
            term = jnp.maximum(_dot_nt(qp, ki), 0.0) * w_scaled[:, IDX_DIM + hh:IDX_DIM + hh + 1]
            acc = term if acc is None else acc + term
    return acc


KEY_CHUNK = 256
N_KEY_CHUNKS = SEQ // KEY_CHUNK
CAUSAL_VARIANTS = 4


def _dsa_prompt_kernel(q_ref, qi_ref, t_ref, k_ref, v_ref, ki2_ref, o_ref, sc_ref, bias_ref):
    i = pl.program_id(1)
    row = i * QBLOCK + lax.broadcasted_iota(I32, (QBLOCK, KEY_CHUNK), 0)
    lane = lax.broadcasted_iota(I32, (QBLOCK, KEY_CHUNK), 1)
    row_g = i * QBLOCK + lax.broadcasted_iota(I32, (QBLOCK, LANES), 0)
    lane_g = lax.broadcasted_iota(I32, (QBLOCK, LANES), 1)

    def attend(n_keys, n_visible_chunks):
        nc = n_keys // KEY_CHUNK

        def causal(j, u, su):
            return None if j < n_visible_chunks else j * KEY_CHUNK + u * LANES + lane_g <= row_g

        w_scaled = (t_ref[...] * (IDX_HEADS ** -0.5)) * (IDX_DIM ** -0.5)
        sc = _index_scores(qi_ref[...], w_scaled, ki2_ref[:n_keys, :LANES], ki2_ref[:n_keys, LANES:])
        for j in range(nc):
            sc_j = sc[:, j * KEY_CHUNK:(j + 1) * KEY_CHUNK]
            sc_ref[j] = sc_j if j < n_visible_chunks else jnp.where(j * KEY_CHUNK + lane <= row, sc_j, NEG_INF)
        _select_chunks(sc_ref.at[:nc], bias_ref.at[:nc], nc, TOPK, causal)
        bias = jnp.concatenate([bias_ref[j] for j in range(nc)], axis=-1)
        for h in range(N_HEADS):
            sl = slice(h * HEAD_DIM, (h + 1) * HEAD_DIM)
            s = _dot_nt(q_ref[:, sl], k_ref[:n_keys, sl]) * (HEAD_DIM ** -0.5) + bias
            m = jnp.max(s, axis=-1, keepdims=True)
            p = jnp.exp(s - m)
            l = jnp.sum(p, axis=-1, keepdims=True)
            o_ref[:, sl] = (_dot(p.astype(BF16), v_ref[:n_keys, sl]) / l).astype(BF16)

    blocks_per_variant = (SEQ // QBLOCK) // CAUSAL_VARIANTS
    for v in range(CAUSAL_VARIANTS):
        @pl.when(i // blocks_per_variant == v)
        def _(v=v):
            attend((v + 1) * (SEQ // CAUSAL_VARIANTS), (v * blocks_per_variant * QBLOCK) // KEY_CHUNK)


def _dsa_prompt(q, qi, tail, kb, vb, ki2):
    nb = SEQ // QBLOCK

    def qspec(w):
        return pl.BlockSpec((QBLOCK, w), lambda b, i: (b * nb + i, 0))

    def kspec(w):
        return pl.BlockSpec((SEQ, w), lambda b, i: (b, 0))

    return pl.pallas_call(
        _dsa_prompt_kernel,
        grid=(BATCH, nb),
        in_specs=[qspec(MIX_W), qspec(MIX_W), qspec(LANES), kspec(MIX_W), kspec(MIX_W), kspec(2 * LANES)],
        out_specs=qspec(MIX_W),
        out_shape=jax.ShapeDtypeStruct((MP, MIX_W), BF16),
        scratch_shapes=[pltpu.VMEM((N_KEY_CHUNKS, QBLOCK, KEY_CHUNK), F32)] * 2,
        compiler_params=_params(2),
        name="dsa_prompt",
    )(q, qi, tail, kb, vb, ki2)


def _dsa_sample_scores_kernel(pt_ref, qi_ref, w_ref, *rest):
    page_refs, knew_ref, o_ref = rest[:N_PAGES], rest[N_PAGES], rest[N_PAGES + 1]
    qi = qi_ref[0]
    w = w_ref[0] * (IDX_HEADS ** -0.5)

    def chunk_scores(d):
        r = jnp.maximum(d * (IDX_DIM ** -0.5), 0.0) * w
        return jnp.sum(r.reshape(DEC_SEQ, IDX_HEADS, LANES), axis=1)

    for p in range(N_PAGES):
        o_ref[0, :, p * LANES:(p + 1) * LANES] = chunk_scores(_dot(qi, page_refs[p][0, 0].astype(BF16)))
    new = chunk_scores(_dot_nt(qi, knew_ref[0]))
    tq = lax.broadcasted_iota(I32, (DEC_SEQ, LANES), 0)
    jk = lax.broadcasted_iota(I32, (DEC_SEQ, LANES), 1)
    o_ref[0, :, PAST:] = jnp.where(jk <= tq, new, NEG_INF)


def _dsa_sample_scores(page_table, qi32, w32, cache_kidx_t, layer, knew_pad):
    page_specs = [pl.BlockSpec((1, 1, IDX_DIM, PAGE_SIZE),
                               functools.partial(lambda b, pt, p: (layer, pt[b, p], 0, 0), p=p))
                  for p in range(N_PAGES)]
    grid_spec = pltpu.PrefetchScalarGridSpec(
        num_scalar_prefetch=1,
        grid=(DEC_BATCH,),
        in_specs=[pl.BlockSpec((1, DEC_SEQ * IDX_HEADS, IDX_DIM), lambda b, pt: (b, 0, 0)),
                  pl.BlockSpec((1, DEC_SEQ * IDX_HEADS, 1), lambda b, pt: (b, 0, 0))] + page_specs
                 + [pl.BlockSpec((1, LANES, IDX_DIM), lambda b, pt: (b, 0, 0))],
        out_specs=pl.BlockSpec((1, DEC_SEQ, SAMPLE_KEYS), lambda b, pt: (b, 0, 0)),
    )
    return pl.pallas_call(
        _dsa_sample_scores_kernel,
        grid_spec=grid_spec,
        out_shape=jax.ShapeDtypeStruct((DEC_BATCH, DEC_SEQ, SAMPLE_KEYS), F32),
        compiler_params=_params(1),
        name="dsa_sample_scores",
    )(page_table, qi32, w32, *([cache_kidx_t] * N_PAGES), knew_pad)


SEL_ROWS = 128
N_SAMPLE_CHUNKS = SAMPLE_KEYS // LANES


def _sample_select_kernel(in_ref, o_ref, sc_ref, bias_ref):
    for c in range(N_SAMPLE_CHUNKS):
        sc_ref[c] = in_ref[:, c * LANES:(c + 1) * LANES]
    _select_chunks(sc_ref, bias_ref, N_SAMPLE_CHUNKS, TOPK, lambda j, u, su: su > NEG_INF)
    for c in range(N_SAMPLE_CHUNKS):
        o_ref[:, c * LANES:(c + 1) * LANES] = bias_ref[c]


def _sample_select(sc):
    return pl.pallas_call(
        _sample_select_kernel,
        grid=(MS // SEL_ROWS,),
        in_specs=[pl.BlockSpec((SEL_ROWS, SAMPLE_KEYS), lambda i: (i, 0))],
        out_specs=pl.BlockSpec((SEL_ROWS, SAMPLE_KEYS), lambda i: (i, 0)),
        out_shape=jax.ShapeDtypeStruct((MS, SAMPLE_KEYS), F32),
        scratch_shapes=[pltpu.VMEM((N_SAMPLE_CHUNKS, SEL_ROWS, LANES), F32)] * 2,
        compiler_params=_params(1),
        name="sample_select",
    )(sc)


def _dsa_sample_attn_kernel(pt_ref, q_ref, bias_ref, *rest):
    k_refs = rest[:N_PAGES]
    v_refs = rest[N_PAGES:2 * N_PAGES]
    knew_ref, vnew_ref, o_ref, s_ref = rest[2 * N_PAGES:]
    q = q_ref[0]
    def head_rows(h):
        return pl.ds(h, PAGE_SIZE, stride=N_HEADS)

    for h in range(N_HEADS):
        rs = slice(h * DEC_SEQ, (h + 1) * DEC_SEQ)
        hs = slice(h * HEAD_DIM, (h + 1) * HEAD_DIM)
        qh = q[:, hs]
        for p in range(N_PAGES):
            s_ref[rs, p * LANES:(p + 1) * LANES] = _dot_nt(qh, k_refs[p][0, 0, head_rows(h), :].astype(BF16))
        s_ref[rs, PAST:] = _dot_nt(qh, knew_ref[0, :, hs])
    bias = bias_ref[0]
    s = s_ref[...] * (HEAD_DIM ** -0.5) + jnp.concatenate([bias] * N_HEADS, axis=0)
    m = jnp.max(s, axis=-1, keepdims=True)
    pr = jnp.exp(s - m)
    l = jnp.sum(pr, axis=-1, keepdims=True)
    pb = pr.astype(BF16)
    for h in range(N_HEADS):
        rs = slice(h * DEC_SEQ, (h + 1) * DEC_SEQ)
        hs = slice(h * HEAD_DIM, (h + 1) * HEAD_DIM)
        acc = _dot(pb[rs, PAST:], vnew_ref[0, :, hs])
        for p in range(N_PAGES):
            acc = acc + _dot(pb[rs, p * LANES:(p + 1) * LANES], v_refs[p][0, 0, head_rows(h), :].astype(BF16))
        o_ref[0, :, hs] = (acc / l[rs]).astype(BF16)


def _dsa_sample_attn(page_table, q, bias, cache_k, cache_v, layer, knew_pad, vnew_pad):
    def page_spec(p):
        return pl.BlockSpec((1, 1, PAGE_SIZE * N_HEADS, HEAD_DIM),
                            functools.partial(lambda b, pt, p: (layer, pt[b, p], 0, 0), p=p))

    grid_spec = pltpu.PrefetchScalarGridSpec(
        num_scalar_prefetch=1,
        grid=(DEC_BATCH,),
        in_specs=[pl.BlockSpec((1, DEC_SEQ, MIX_W), lambda b, pt: (b, 0, 0)),
                  pl.BlockSpec((1, DEC_SEQ, SAMPLE_KEYS), lambda b, pt: (b, 0, 0))]
                 + [page_spec(p) for p in range(N_PAGES)] + [page_spec(p) for p in range(N_PAGES)]
                 + [pl.BlockSpec((1, LANES, MIX_W), lambda b, pt: (b, 0, 0)),
                    pl.BlockSpec((1, LANES, MIX_W), lambda b, pt: (b, 0, 0))],
        out_specs=pl.BlockSpec((1, DEC_SEQ, MIX_W), lambda b, pt: (b, 0, 0)),
        scratch_shapes=[pltpu.VMEM((N_HEADS * DEC_SEQ, SAMPLE_KEYS), F32)],
    )
    return pl.pallas_call(
        _dsa_sample_attn_kernel,
        grid_spec=grid_spec,
        out_shape=jax.ShapeDtypeStruct((DEC_BATCH, DEC_SEQ, MIX_W), BF16),
        compiler_params=_params(1),
        name="dsa_sample_attn",
    )(page_table, q, bias, *([cache_k] * N_PAGES), *([cache_v] * N_PAGES), knew_pad, vnew_pad)


CONV_TT = 512


def _conv_prompt_kernel(cx_ref, cb_ref, cc_ref, w_ref, yb_ref, buf_ref, carry_ref):
    j = pl.program_id(1)

    @pl.when(j == 0)
    def _():
        carry_ref[...] = jnp.zeros_like(carry_ref)

    z = cc_ref[...] * cx_ref[...]
    row = lax.broadcasted_iota(I32, z.shape, 0)
    c0 = carry_ref[0:1, :]
    c1 = carry_ref[1:2, :]
    zm1 = jnp.where(row == 0, c1, pltpu.roll(z, 1, 0))
    zm2 = jnp.where(row == 0, c0, jnp.where(row == 1, c1, pltpu.roll(z, 2, 0)))
    zc = w_ref[0:1, :] * zm2 + w_ref[1:2, :] * zm1 + w_ref[2:3, :] * z
    yb_ref[...] = (cb_ref[...] * zc).astype(BF16)
    last = z[CONV_TT - 2:CONV_TT, :]
    carry_ref[0:2, :] = last
    buf_ref[0] = last


def _conv_prompt(proj, conv_w):
    nt = SEQ // CONV_TT

    def cb(c):
        return pl.BlockSpec((CONV_TT, MIX_W), lambda b, j: (b * nt + j, c))

    return pl.pallas_call(
        _conv_prompt_kernel,
        grid=(BATCH, nt),
        in_specs=[cb(CB_CX), cb(CB_CB), cb(CB_CC), pl.BlockSpec((3, MIX_W), lambda b, j: (0, 0))],
        out_specs=[pl.BlockSpec((CONV_TT, MIX_W), lambda b, j: (b * nt + j, 0)),
                   pl.BlockSpec((1, 2, MIX_W), lambda b, j: (b, 0, 0))],
        out_shape=[jax.ShapeDtypeStruct((MP, MIX_W), BF16),
                   jax.ShapeDtypeStruct((BATCH, 2, MIX_W), F32)],
        scratch_shapes=[pltpu.VMEM((8, MIX_W), F32)],
        compiler_params=_params(2),
        name="conv_prompt",
    )(proj, proj, proj, conv_w)


def _gmlp_prompt_kernel(gu_ref, gv_ref, w_ref, bt_ref, o_ref):
    r_io = lax.broadcasted_iota(I32, (GMLP_CHUNK, GMLP_CHUNK), 0)
    c_io = lax.broadcasted_iota(I32, (GMLP_CHUNK, GMLP_CHUNK), 1)
    tril = c_io <= r_io
    for g in range(GMLP_GROUPS):
        sl = slice(g * LANES, (g + 1) * LANES)
        wm = jnp.where(tril, w_ref[g], 0.0).astype(BF16)
        for c in range(GMLP_STEP_CHUNKS):
            rows = slice(c * GMLP_CHUNK, (c + 1) * GMLP_CHUNK)
            z = _dot(wm, gv_ref[rows, sl].astype(BF16)) + bt_ref[:, g:g + 1]
            o_ref[rows, sl] = (gu_ref[rows, sl] * z).astype(BF16)


GMLP_STEP_CHUNKS = 4


def _gmlp_prompt(proj, gmlp_w, gmlp_bt):
    rows = GMLP_STEP_CHUNKS * GMLP_CHUNK

    def cb(c):
        return pl.BlockSpec((rows, MIX_W), lambda i: (i, c))

    return pl.pallas_call(
        _gmlp_prompt_kernel,
        grid=(MP // rows,),
        in_specs=[cb(CB_GU), cb(CB_GV),
                  pl.BlockSpec((GMLP_GROUPS, GMLP_CHUNK, GMLP_CHUNK), lambda i: (0, 0, 0)),
                  pl.BlockSpec((GMLP_CHUNK, GMLP_GROUPS), lambda i: (0, 0))],
        out_specs=pl.BlockSpec((rows, MIX_W), lambda i: (i, 0)),
        out_shape=jax.ShapeDtypeStruct((MP, MIX_W), BF16),
        compiler_params=_params(1),
        name="gmlp_prompt",
    )(proj, proj, gmlp_w, gmlp_bt)


def _mix_sample_kernel(cx_ref, cb_ref, cc_ref, gu_ref, gv_ref, buf_ref, cw_ref, gw_ref, gb_ref,
                       yb_ref, nbuf_ref, yd_ref):
    zf = [buf_ref[:, 0, :], buf_ref[:, 1, :]]
    for t in range(DEC_SEQ):
        zf.append(cc_ref[:, t, :] * cx_ref[:, t, :])
    for t in range(DEC_SEQ):
        zc = cw_ref[0:1, :] * zf[t] + cw_ref[1:2, :] * zf[t + 1] + cw_ref[2:3, :] * zf[t + 2]
        yb_ref[:, t, :] = (cb_ref[:, t, :] * zc).astype(BF16)
    nbuf_ref[:, 0, :] = zf[DEC_SEQ]
    nbuf_ref[:, 1, :] = zf[DEC_SEQ + 1]
    v = [gv_ref[:, s, :] for s in range(DEC_SEQ)]
    for t in range(DEC_SEQ):
        z = gb_ref[t:t + 1, :]
        for s in range(t + 1):
            z = z + gw_ref[t * DEC_SEQ + s:t * DEC_SEQ + s + 1, :] * v[s]
        yd_ref[:, t, :] = (gu_ref[:, t, :] * z).astype(BF16)


def _mix_sample(proj_s, state_conv_l, conv_w, gw_lane, gb_lane):
    def cb(c):
        return pl.BlockSpec((DEC_BATCH, DEC_SEQ, MIX_W), lambda i: (0, 0, c))

    full3 = pl.BlockSpec((DEC_BATCH, DEC_SEQ, MIX_W), lambda i: (0, 0, 0))
    buf3 = pl.BlockSpec((DEC_BATCH, 2, MIX_W), lambda i: (0, 0, 0))
    return pl.pallas_call(
        _mix_sample_kernel,
        grid=(1,),
        in_specs=[cb(CB_CX), cb(CB_CB), cb(CB_CC), cb(CB_GU), cb(CB_GV), buf3,
                  pl.BlockSpec((3, MIX_W), lambda i: (0, 0)),
                  pl.BlockSpec((DEC_SEQ * DEC_SEQ, MIX_W), lambda i: (0, 0)),
                  pl.BlockSpec((DEC_SEQ, MIX_W), lambda i: (0, 0))],
        out_specs=[full3, buf3, full3],
        out_shape=[jax.ShapeDtypeStruct((DEC_BATCH, DEC_SEQ, MIX_W), BF16),
                   jax.ShapeDtypeStruct((DEC_BATCH, 2, MIX_W), F32),
                   jax.ShapeDtypeStruct((DEC_BATCH, DEC_SEQ, MIX_W), BF16)],
        compiler_params=_params(1),
        name="mix_sample",
    )(proj_s, proj_s, proj_s, proj_s, proj_s, state_conv_l, conv_w, gw_lane, gb_lane)


def _ssm_params_kernel(ar_ref, ai_ref, ldt_ref, br_ref, bi_ref, ab_ref, bbr_ref, bbi_ref):
    ar, ai = ar_ref[...], ai_ref[...]
    dt = jnp.exp(ldt_ref[...])
    mag = jnp.exp(dt * ar)
    abar_re, abar_im = mag * jnp.cos(dt * ai), mag * jnp.sin(dt * ai)
    den = ar * ar + ai * ai
    nr, ni = abar_re - 1.0, abar_im
    coef_re = (nr * ar + ni * ai) / den
    coef_im = (ni * ar - nr * ai) / den
    ab_ref[...] = jnp.zeros_like(ab_ref)
    ab_ref[0:1, :] = abar_re
    ab_ref[1:2, :] = abar_im
    br, bi = br_ref[...], bi_ref[...]
    bbr_ref[...] = (coef_re * br - coef_im * bi).astype(BF16)
    bbi_ref[...] = (coef_re * bi + coef_im * br).astype(BF16)


def _ssm_params(a_re, a_im, ldt, bd_re, bd_im):
    row = pl.BlockSpec((1, SSM_STATE), lambda i: (0, 0))
    mat = pl.BlockSpec((MIX_W, SSM_STATE), lambda i: (0, 0))
    return pl.pallas_call(
        _ssm_params_kernel,
        grid=(1,),
        in_specs=[row, row, row, mat, mat],
        out_specs=[pl.BlockSpec((8, SSM_STATE), lambda i: (0, 0)), mat, mat],
        out_shape=[jax.ShapeDtypeStruct((8, SSM_STATE), F32),
                   jax.ShapeDtypeStruct((MIX_W, SSM_STATE), BF16),
                   jax.ShapeDtypeStruct((MIX_W, SSM_STATE), BF16)],
        compiler_params=_params(1),
        name="ssm_params",
    )(a_re, a_im, ldt, bd_re, bd_im)


def _gelu_tanh(x):
    return 0.5 * x * (1.0 + jnp.tanh(0.7978845608028654 * (x + 0.044715 * (x * x * x))))


SSM_BLOCK_STATES = SSM_STATE // (MIX_W // LANES)


def _ssm_drive(ub, bb_ref):
    return jnp.concatenate(
        [_dot(ub[:, c * LANES:(c + 1) * LANES],
              bb_ref[c * LANES:(c + 1) * LANES, c * SSM_BLOCK_STATES:(c + 1) * SSM_BLOCK_STATES])
         for c in range(MIX_W // LANES)], axis=-1)


def _ssm_readout(u, hr, hi, cr_ref, ci_ref, d_ref, wg_ref, bg_ref):
    hrb, hib = hr.astype(BF16), hi.astype(BF16)
    parts = []
    for c in range(MIX_W // LANES):
        ss = slice(c * SSM_BLOCK_STATES, (c + 1) * SSM_BLOCK_STATES)
        cs = slice(c * LANES, (c + 1) * LANES)
        parts.append(_dot(hrb[:, ss], cr_ref[ss, cs]) - _dot(hib[:, ss], ci_ref[ss, cs]))
    y = jnp.concatenate(parts, axis=-1) + d_ref[...] * u
    g = _gelu_tanh(y)
    return g * _sigmoid(_dot(g.astype(BF16), wg_ref[...]) + bg_ref[...])


SSM_TC = 128
SSM_SLABS = MIX_W // LANES


def _ssm_prompt_kernel(u0, u1, u2, u3, ab_ref, bbr_ref, bbi_ref, cr_ref, ci_ref, d_ref, wg_ref, bg_ref,
                       y_ref, str_ref, sti_ref, hr_ref, hi_ref, carry_ref, il_ref):
    j = pl.program_id(0)

    @pl.when(j == 0)
    def _():
        carry_ref[...] = jnp.zeros_like(carry_ref)

    def batch_rows(b):
        return pl.ds(b, SSM_TC, stride=BATCH)

    for b, u_ref in enumerate((u0, u1, u2, u3)):
        for c in range(SSM_SLABS):
            il_ref[c, batch_rows(b), :] = u_ref[:, c * LANES:(c + 1) * LANES]
    u = jnp.concatenate([il_ref[c] for c in range(SSM_SLABS)], axis=-1)
    ub = u.astype(BF16)
    hr_ref[...] = _ssm_drive(ub, bbr_ref)
    hi_ref[...] = _ssm_drive(ub, bbi_ref)
    ar, ai = ab_ref[0:1, :], ab_ref[1:2, :]

    def step2(t2, carry):
        pr, pi = carry
        rows = pl.ds(pl.multiple_of(t2 * 2 * BATCH, 2 * BATCH), 2 * BATCH)
        br, bi = hr_ref[rows, :], hi_ref[rows, :]
        er = ar * pr - ai * pi + br[:BATCH]
        ei = ar * pi + ai * pr + bi[:BATCH]
        nr = ar * er - ai * ei + br[BATCH:]
        ni = ar * ei + ai * er + bi[BATCH:]
        hr_ref[rows, :] = jnp.concatenate([er, nr], axis=0)
        hi_ref[rows, :] = jnp.concatenate([ei, ni], axis=0)
        return nr, ni

    fr, fi = lax.fori_loop(0, SSM_TC // 2, step2, (carry_ref[0:BATCH, :], carry_ref[BATCH:2 * BATCH, :]), unroll=4)
    carry_ref[0:BATCH, :] = fr
    carry_ref[BATCH:2 * BATCH, :] = fi
    str_ref[...] = fr
    sti_ref[...] = fi
    y = _ssm_readout(u, hr_ref[...], hi_ref[...], cr_ref, ci_ref, d_ref, wg_ref, bg_ref)
    for c in range(SSM_SLABS):
        il_ref[c] = y[:, c * LANES:(c + 1) * LANES]
    for b in range(BATCH):
        for c in range(SSM_SLABS):
            y_ref[b, :, c * LANES:(c + 1) * LANES] = il_ref[c, batch_rows(b), :].astype(BF16)


def _ssm_prompt(proj, ab, bbr, bbi, cdr, cdi, d, wglu, bglu):
    nt = SEQ // SSM_TC

    def const(shape):
        return pl.BlockSpec(shape, lambda j: (0,) * len(shape))

    u_specs = [pl.BlockSpec((SSM_TC, MIX_W), functools.partial(lambda j, b: (b * nt + j, CB_SU), b=b))
               for b in range(BATCH)]
    state = pl.BlockSpec((BATCH, SSM_STATE), lambda j: (0, 0))
    return pl.pallas_call(
        _ssm_prompt_kernel,
        grid=(nt,),
        in_specs=u_specs + [const((8, SSM_STATE)), const((MIX_W, SSM_STATE)), const((MIX_W, SSM_STATE)),
                            const((SSM_STATE, MIX_W)), const((SSM_STATE, MIX_W)), const((1, MIX_W)),
                            const((MIX_W, MIX_W)), const((1, MIX_W))],
        out_specs=[pl.BlockSpec((BATCH, SSM_TC, MIX_W), lambda j: (0, j, 0)), state, state],
        out_shape=[jax.ShapeDtypeStruct((BATCH, SEQ, MIX_W), BF16),
                   jax.ShapeDtypeStruct((BATCH, SSM_STATE), F32),
                   jax.ShapeDtypeStruct((BATCH, SSM_STATE), F32)],
        scratch_shapes=[pltpu.VMEM((BATCH * SSM_TC, SSM_STATE), F32), pltpu.VMEM((BATCH * SSM_TC, SSM_STATE), F32),
                        pltpu.VMEM((2 * BATCH, SSM_STATE), F32),
                        pltpu.VMEM((SSM_SLABS, BATCH * SSM_TC, LANES), F32)],
        compiler_params=_params(1),
        name="ssm_prompt",
    )(proj, proj, proj, proj, ab, bbr, bbi, cdr, cdi, d, wglu, bglu)


def _ssm_sample_kernel(u_ref, h0r_ref, h0i_ref, ab_ref, bbr_ref, bbi_ref, cr_ref, ci_ref, d_ref, wg_ref, bg_ref,
                       y_ref, nr_ref, ni_ref):
    ar, ai = ab_ref[0:1, :], ab_ref[1:2, :]
    hr, hi = h0r_ref[...], h0i_ref[...]
    for t in range(DEC_SEQ):
        u = u_ref[:, t, :]
        ub = u.astype(BF16)
        hr, hi = (ar * hr - ai * hi + _ssm_drive(ub, bbr_ref),
                  ar * hi + ai * hr + _ssm_drive(ub, bbi_ref))
        y_ref[:, t, :] = _ssm_readout(u, hr, hi, cr_ref, ci_ref, d_ref, wg_ref, bg_ref).astype(BF16)
    nr_ref[...] = hr
    ni_ref[...] = hi


def _ssm_sample(proj_s, h0r, h0i, ab, bbr, bbi, cdr, cdi, d, wglu, bglu):
    def const(shape):
        return pl.BlockSpec(shape, lambda i: (0,) * len(shape))

    st = const((DEC_BATCH, SSM_STATE))
    return pl.pallas_call(
        _ssm_sample_kernel,
        grid=(1,),
        in_specs=[pl.BlockSpec((DEC_BATCH, DEC_SEQ, MIX_W), lambda i: (0, 0, CB_SU)), st, st,
                  const((8, SSM_STATE)), const((MIX_W, SSM_STATE)), const((MIX_W, SSM_STATE)),
                  const((SSM_STATE, MIX_W)), const((SSM_STATE, MIX_W)), const((1, MIX_W)),
                  const((MIX_W, MIX_W)), const((1, MIX_W))],
        out_specs=[const((DEC_BATCH, DEC_SEQ, MIX_W)), st, st],
        out_shape=[jax.ShapeDtypeStruct((DEC_BATCH, DEC_SEQ, MIX_W), BF16),
                   jax.ShapeDtypeStruct((DEC_BATCH, SSM_STATE), F32),
                   jax.ShapeDtypeStruct((DEC_BATCH, SSM_STATE), F32)],
        compiler_params=_params(1),
        name="ssm_sample",
    )(proj_s, h0r, h0i, ab, bbr, bbi, cdr, cdi, d, wglu, bglu)


def _rope_tables():
    lane = jnp.arange(LANES)

    def inv(half):
        return ROPE_THETA ** (-jnp.arange(half, dtype=F32) / half)

    inv_qk = jnp.where(lane < ROT_DIM, inv(ROT_DIM // 2)[lane % (ROT_DIM // 2)], 0.0)
    idx = inv(IDX_ROT // 2)[lane % (IDX_ROT // 2)]
    inv_i = jnp.where((lane % IDX_DIM) < IDX_ROT, idx, 0.0)
    inv_t = jnp.where(lane < IDX_ROT, idx, 0.0)
    pos = jnp.concatenate([jnp.arange(SEQ), PAST + jnp.arange(PREP_TM) % DEC_SEQ]).astype(F32)
    tabs = []
    for inv_row in (inv_qk, inv_i, inv_t):
        ang = pos[:, None] * inv_row[None, :].astype(F32)
        tabs += [jnp.cos(ang), jnp.sin(ang)]
    return jnp.stack(tabs)


def _pad_rows(a, rows):
    return jnp.pad(a, ((0, 0), (0, rows - a.shape[1]), (0, 0)))


def kernel(x_prompt, x_sample, cache_k, cache_v, cache_kidx, state_conv, state_ssm_re, state_ssm_im, page_table,
           norm_mix, w_in, conv_w, ssm_a_re, ssm_a_im, ssm_log_dt, ssm_b_re, ssm_b_im, ssm_c_re, ssm_c_im, ssm_d,
           w_glu, b_glu, gmlp_w, gmlp_b, w_branch, w_out, norm_ffn, w_ffn_in, w_ffn_out, norm_final):
    x = jnp.concatenate([x_prompt.reshape(MP, D_MODEL), x_sample.reshape(MS, D_MODEL)], axis=0)
    tables = _rope_tables()
    eye_g = jnp.eye(SSM_GROUPS, dtype=F32)
    cache_kidx_t = jnp.swapaxes(cache_kidx, 2, 3)
    n_pool = cache_k.shape[1]
    cache_k2 = cache_k.reshape(DEPTH, n_pool, PAGE_SIZE * N_HEADS, HEAD_DIM)
    cache_v2 = cache_v.reshape(DEPTH, n_pool, PAGE_SIZE * N_HEADS, HEAD_DIM)
    n_gate0 = w_in.shape[2] - N_BRANCH * D_MODEL
    c_ki = 3 * MIX_W + IDX_HEADS * IDX_DIM
    c_cx = c_ki + IDX_DIM + IDX_HEADS
    w_ffn_out_b = w_ffn_out.astype(BF16)
    w_branch_b = w_branch.astype(BF16)

    st_p, st_s = [], []
    for l in range(DEPTH):
        wl = w_in[l]
        w_main = jnp.concatenate(
            [wl[:, :c_ki], wl[:, c_cx:n_gate0], wl[:, c_ki:c_cx],
             jnp.zeros((D_MODEL, N_MAIN - n_gate0), F32)], axis=1).astype(BF16)
        w_gate = wl[:, n_gate0:].astype(BF16)

        proj = _norm_mm(x, norm_mix[l][None, :], w_main, 1088, 768)
        q_p, k_p, kb_p, v_p, vb_p, qi_p, tail_p, ki2_p = _prep(
            proj, tables, 0, MP, lambda i: i % (SEQ // PREP_TM))
        q_s, k_s, kb_s, v_s, vb_s, qi_s, tail_s, _ = _prep(
            proj, tables, MP, MS, lambda i: SEQ // PREP_TM)

        ya_p = _dsa_prompt(q_p, qi_p, tail_p, kb_p, vb_p, ki2_p)
        tail_s3 = tail_s.reshape(DEC_BATCH, DEC_SEQ, LANES)
        w_s = tail_s3[:, :, IDX_DIM:IDX_DIM + IDX_HEADS].reshape(DEC_BATCH, DEC_SEQ * IDX_HEADS, 1)
        kinew = _pad_rows(tail_s3[:, :, :IDX_DIM].astype(BF16), LANES)
        sc_s = _dsa_sample_scores(page_table, qi_s.reshape(DEC_BATCH, DEC_SEQ * IDX_HEADS, IDX_DIM), w_s,
                                  cache_kidx_t, l, kinew)
        bias_s = _sample_select(sc_s.reshape(MS, SAMPLE_KEYS)).reshape(DEC_BATCH, DEC_SEQ, SAMPLE_KEYS)
        knew = _pad_rows(kb_s.reshape(DEC_BATCH, DEC_SEQ, MIX_W), LANES)
        vnew = _pad_rows(vb_s.reshape(DEC_BATCH, DEC_SEQ, MIX_W), LANES)
        ya_s = _dsa_sample_attn(page_table, q_s.reshape(DEC_BATCH, DEC_SEQ, MIX_W), bias_s, cache_k2, cache_v2, l,
                                knew, vnew)

        proj_s = proj[MP:].reshape(DEC_BATCH, DEC_SEQ, N_MAIN)
        yb_p, buf_p = _conv_prompt(proj, conv_w[l])
        yd_p = _gmlp_prompt(proj, gmlp_w[l], gmlp_b[l].T)
        gw_lane = jnp.repeat(gmlp_w[l][:, :DEC_SEQ, :DEC_SEQ].transpose(1, 2, 0).reshape(DEC_SEQ * DEC_SEQ, GMLP_GROUPS),
                             LANES, axis=1)
        gb_lane = jnp.repeat(gmlp_b[l][:, :DEC_SEQ].T, LANES, axis=1)
        yb_s, buf_s, yd_s = _mix_sample(proj_s, state_conv[l], conv_w[l], gw_lane, gb_lane)

        bd_re = jnp.einsum('gnp,gh->gphn', ssm_b_re[l], eye_g).reshape(MIX_W, SSM_STATE)
        bd_im = jnp.einsum('gnp,gh->gphn', ssm_b_im[l], eye_g).reshape(MIX_W, SSM_STATE)
        cd_re = jnp.einsum('gpn,gh->gnhp', ssm_c_re[l], eye_g).reshape(SSM_STATE, MIX_W).astype(BF16)
        cd_im = jnp.einsum('gpn,gh->gnhp', ssm_c_im[l], eye_g).reshape(SSM_STATE, MIX_W).astype(BF16)
        ldt = jnp.repeat(ssm_log_dt[l], SSM_N)[None, :]
        ab, bbr, bbi = _ssm_params(ssm_a_re[l].reshape(1, SSM_STATE), ssm_a_im[l].reshape(1, SSM_STATE), ldt,
                                   bd_re, bd_im)
        ssm_consts = (ab, bbr, bbi, cd_re, cd_im, ssm_d[l][None, :], w_glu[l].astype(BF16), b_glu[l][None, :])
        yc_p, hre_p, him_p = _ssm_prompt(proj, *ssm_consts)
        yc_p = yc_p.reshape(MP, MIX_W)
        yc_s, nre_s, nim_s = _ssm_sample(proj_s, state_ssm_re[l].reshape(DEC_BATCH, SSM_STATE),
                                         state_ssm_im[l].reshape(DEC_BATCH, SSM_STATE), *ssm_consts)

        ys_sample = tuple(y.reshape(MS, MIX_W) for y in (ya_s, yb_s, yc_s, yd_s))
        merged = _gate_merge(x, norm_mix[l][None, :], w_gate, (ya_p, yb_p, yc_p, yd_p), ys_sample, w_branch_b, l, 512)
        x = _mm_res(merged, w_out, l, x, 1088, 512)
        act = _swiglu(x, norm_ffn[l][None, :], w_ffn_in, l, 1088, 512)
        x = _mm_res(act, w_ffn_out_b, l, x, 1088, 256)

        gv_s = proj[MP:, CB_GV * MIX_W:(CB_GV + 1) * MIX_W]
        st_p.append((k_p.reshape(BATCH, SEQ, N_HEADS, HEAD_DIM), v_p.reshape(BATCH, SEQ, N_HEADS, HEAD_DIM),
                     tail_p[:, :IDX_DIM].reshape(BATCH, SEQ, IDX_DIM), buf_p,
                     hre_p.reshape(BATCH, SSM_GROUPS, SSM_N), him_p.reshape(BATCH, SSM_GROUPS, SSM_N)))
        st_s.append((k_s.reshape(DEC_BATCH, DEC_SEQ, N_HEADS, HEAD_DIM),
                     v_s.reshape(DEC_BATCH, DEC_SEQ, N_HEADS, HEAD_DIM),
                     tail_s[:, :IDX_DIM].reshape(DEC_BATCH, DEC_SEQ, IDX_DIM), buf_s,
                     nre_s.reshape(DEC_BATCH, SSM_GROUPS, SSM_N), nim_s.reshape(DEC_BATCH, SSM_GROUPS, SSM_N),
                     gv_s.reshape(DEC_BATCH, DEC_SEQ, MIX_W)))

    y_p = _final_norm(x, norm_final[None, :], 0, MP, MS)
    y_s = _final_norm(x, norm_final[None, :], MP, MS, MS)
    outs = [y_p.reshape(BATCH, SEQ, D_MODEL), y_s.reshape(DEC_BATCH, DEC_SEQ, D_MODEL)]
    outs += [jnp.stack([s[i] for s in st_p]) for i in range(6)]
    outs += [jnp.stack([s[i] for s in st_s]) for i in range(7)]
    return tuple(outs)
```
